```python
import math
import jax, jax.numpy as jnp
from jax import lax
import numpy as np

D_MODEL = 2048
BATCH = 4
SEQ = 2048
DEPTH = 2
DEC_BATCH = 128
DEC_SEQ = 8
PAST_LEN = 16384
PAGE_SIZE = 128

N_META = 16
H_A = 8
DK = 128
DV = 128
W_QK = H_A * DK
W_V = H_A * DV
W_QKV = 2 * W_QK + W_V
CONV_A = 4
CHUNK = 64
W_B = 1024
CONV_B = 3
W_C = 1024
CONV_C = 31
N_BRANCH = 3
N_GROUPS = 4
EXP_PER_GROUP = 8
N_EXPERTS = N_GROUPS * EXP_PER_GROUP
TOP_K = 2
D_FF = 512
EPS = 1e-6
LN_EPS = 1e-5
SPLIT_SIZES = (W_QKV, H_A, H_A, W_V, W_B, W_B, W_B, 2 * W_C, N_BRANCH * D_MODEL)
N_IN = W_QKV + 2 * H_A + W_V + 3 * W_B + 2 * W_C + N_BRANCH * D_MODEL

kernel_name = "hybrid_deltanet_conv_hmoe_step"


def _rms(x, w):
    xf = x.astype(jnp.float32)
    y = xf * lax.rsqrt(jnp.mean(xf * xf, axis=-1, keepdims=True) + EPS) * w.astype(jnp.float32)
    return y.astype(x.dtype)


def _layernorm(x, w, b):
    xf = x.astype(jnp.float32)
    mu = jnp.mean(xf, axis=-1, keepdims=True)
    xc = xf - mu
    var = jnp.mean(xc * xc, axis=-1, keepdims=True)
    return (xc * lax.rsqrt(var + LN_EPS) * w.astype(jnp.float32) + b.astype(jnp.float32)).astype(x.dtype)


def _l2norm(x):
    return x * lax.rsqrt(jnp.sum(x * x, axis=-1, keepdims=True) + EPS)


def _split_cols(z):
    outs = []
    off = 0
    for s in SPLIT_SIZES:
        outs.append(z[..., off:off + s])
        off += s
    return outs


def _causal_dwconv(x, buf, w):
    width, ch = w.shape
    xp = jnp.concatenate([buf.astype(x.dtype), x], axis=1)
    y = lax.conv_general_dilated(xp, w[:, None, :].astype(x.dtype), window_strides=(1,), padding='VALID',
                                 dimension_numbers=('NWC', 'WIO', 'NWC'), feature_group_count=ch)
    return y, xp[:, xp.shape[1] - (width - 1):]


def _delta_chunk(S, q, k, v, g, beta):
    C = q.shape[1]
    G = jnp.cumsum(g, axis=1)
    causal = jnp.tril(jnp.ones((C, C), bool))[None, :, :, None]
    strict = jnp.tril(jnp.ones((C, C), bool), -1)[None, :, :, None]
    diff = G[:, :, None, :] - G[:, None, :, :]
    decay = jnp.exp(jnp.where(causal, diff, -jnp.inf))
    kk = jnp.einsum('bihk,bjhk->bijh', k, k)
    M = jnp.where(strict, beta[:, :, None, :] * kk * decay, 0.0).transpose(0, 3, 1, 2)
    rhs = jnp.concatenate([beta[..., None] * v, (beta * jnp.exp(G))[..., None] * k], axis=-1)
    rhs = rhs.transpose(0, 2, 1, 3)
    sol = lax.linalg.triangular_solve(M, rhs, left_side=True, lower=True, unit_diagonal=True)
    u = sol[..., :DV] - jnp.einsum('bhck,bhkv->bhcv', sol[..., DV:], S)
    qk = jnp.einsum('bihk,bjhk->bhij', q, k) * decay.transpose(0, 3, 1, 2)
    o = (jnp.exp(G).transpose(0, 2, 1)[..., None] * jnp.einsum('bchk,bhkv->bhcv', q, S)
         + jnp.einsum('bhij,bhjv->bhiv', qk, u))
    g_last = G[:, -1]
    kd = k * jnp.exp(g_last[:, None, :] - G)[..., None]
    S = jnp.exp(g_last)[..., None, None] * S + jnp.einsum('bchk,bhcv->bhkv', kd, u)
    return S, o.transpose(0, 2, 1, 3)


def _delta_rule(S, q, k, v, g, beta, lead):
    L = q.shape[1]
    S, o_lead = _delta_chunk(S, q[:, :lead], k[:, :lead], v[:, :lead], g[:, :lead], beta[:, :lead])
    if lead == L:
        return S, o_lead
    n = (L - lead) // CHUNK

    def blocks(t):
        t = t[:, lead:]
        return jnp.moveaxis(t.reshape((t.shape[0], n, CHUNK) + t.shape[2:]), 1, 0)

    def step(s, c):
        return _delta_chunk(s, c[0], c[1], c[2], c[3], c[4])

    S, o = lax.scan(step, S, (blocks(q), blocks(k), blocks(v), blocks(g), blocks(beta)))
    o = jnp.moveaxis(o, 0, 1).reshape((q.shape[0], L - lead, H_A, DV))
    return S, jnp.concatenate([o_lead, o], axis=1)


def _moe(h, w_rg, b_rg, w_re, b_re, w_gate_e, w_up_e, w_down_e):
    shp = h.shape
    t = h.reshape(-1, D_MODEL)
    T = t.shape[0]
    lg = (t @ w_rg + b_rg).astype(jnp.float32)
    pg = jax.nn.softmax(lg, axis=-1)
    gsel = jnp.argmax(lg, axis=-1)
    le = (t @ w_re + b_re).astype(jnp.float32).reshape(T, N_GROUPS, EXP_PER_GROUP)
    le_sel = jnp.take_along_axis(le, gsel[:, None, None], axis=1)[:, 0]
    top_v, top_i = lax.top_k(le_sel, TOP_K)
    w_top = jax.nn.softmax(top_v, axis=-1) * jnp.take_along_axis(pg, gsel[:, None], axis=1)
    w_in_group = jnp.sum(jax.nn.one_hot(top_i, EXP_PER_GROUP, dtype=jnp.float32) * w_top[..., None], axis=1)
    comb = (jax.nn.one_hot(gsel, N_GROUPS, dtype=jnp.float32)[:, :, None] * w_in_group[:, None, :])
    comb = comb.reshape(T, N_EXPERTS).astype(t.dtype)
    y = jnp.zeros_like(t)
    for e in range(N_EXPERTS):
        he = jax.nn.silu(t @ w_gate_e[e]) * (t @ w_up_e[e])
        y = y + comb[:, e:e + 1] * (he @ w_down_e[e])
    return y.reshape(shp)


def _layer(x, s0, buf_a, buf_b, buf_c, lead, norm1_w, w_in, conv_a_w, a_log, dt_bias, onorm_a_w, w_out_a,
           conv_b_w, w_out_b, conv_c_w, conv_c_b, ln_c_w, ln_c_b, w_out_c, w_o, norm2_w,
           w_rg, b_rg, w_re, b_re, w_gate_e, w_up_e, w_down_e):
    B, L = x.shape[0], x.shape[1]
    h = _rms(x, norm1_w)
    z = h @ w_in
    qkv, a_col, b_col, ga, gB, gC, xb, glu, gate_col = _split_cols(z)
    qkv_c, new_a = _causal_dwconv(qkv, buf_a, conv_a_w)
    qkv_c = jax.nn.silu(qkv_c).astype(jnp.float32)
    q = _l2norm(qkv_c[..., :W_QK].reshape(B, L, H_A, DK)) * (DK ** -0.5)
    k = _l2norm(qkv_c[..., W_QK:2 * W_QK].reshape(B, L, H_A, DK))
    v = qkv_c[..., 2 * W_QK:].reshape(B, L, H_A, DV)
    beta = jax.nn.sigmoid(b_col.astype(jnp.float32))
    gdec = -jnp.exp(a_log.astype(jnp.float32)) * jax.nn.softplus(a_col.astype(jnp.float32) + dt_bias.astype(jnp.float32))
    S, o = _delta_rule(s0.astype(jnp.float32), q, k, v, gdec, beta, lead)
    o = _rms(o, onorm_a_w) * jax.nn.silu(ga.astype(jnp.float32).reshape(B, L, H_A, DV))
    ya = o.reshape(B, L, W_V).astype(x.dtype) @ w_out_a
    uc, new_b = _causal_dwconv(gC * xb, buf_b, conv_b_w)
    yb = (gB * uc) @ w_out_b
    gl = glu[..., :W_C] * jax.nn.sigmoid(glu[..., W_C:])
    gc, new_c = _causal_dwconv(gl, buf_c, conv_c_w)
    gc = _layernorm(gc + conv_c_b, ln_c_w, ln_c_b)
    yc = jax.nn.silu(gc) @ w_out_c
    gs = jax.nn.sigmoid(gate_col).reshape(B, L, N_BRANCH, D_MODEL)
    m = gs[:, :, 0] * ya + gs[:, :, 1] * yb + gs[:, :, 2] * yc
    x = x + m @ w_o
    x = x + _moe(_rms(x, norm2_w), w_rg, b_rg, w_re, b_re, w_gate_e, w_up_e, w_down_e)
    return x, S.astype(s0.dtype), new_a, new_b, new_c


def setup_inputs(seed: int = 0) -> dict:
    key = jax.random.key(seed)
    ks = iter(jax.random.split(key, 40))
    f32 = jnp.float32

    def nrm(shape, scale):
        return jax.random.normal(next(ks), shape, f32) * scale

    D = D_MODEL
    x_prompt = nrm((BATCH, SEQ, D), 1.0)
    x_sample = nrm((DEC_BATCH, DEC_SEQ, D), 1.0)
    state_delta = nrm((DEPTH, DEC_BATCH, H_A, DK, DV), 0.05)
    state_conv_a = nrm((DEPTH, DEC_BATCH, CONV_A - 1, W_QKV), 1.0)
    state_conv_b = nrm((DEPTH, DEC_BATCH, CONV_B - 1, W_B), 1.0)
    state_conv_c = nrm((DEPTH, DEC_BATCH, CONV_C - 1, W_C), 0.5)
    meta_tokens = nrm((N_META, D), 1.0)
    norm1_w = 1.0 + nrm((DEPTH, D), 0.02)
    w_in = nrm((DEPTH, D, N_IN), D ** -0.5)
    conv_a_w = nrm((DEPTH, CONV_A, W_QKV), CONV_A ** -0.5)
    a_log = jnp.log(jax.random.uniform(next(ks), (DEPTH, H_A), f32, 1.0, 16.0))
    dt = jnp.exp(jax.random.uniform(next(ks), (DEPTH, H_A), f32, math.log(1e-3), math.log(1e-1)))
    dt_bias = dt + jnp.log(-jnp.expm1(-dt))
    onorm_a_w = 1.0 + nrm((DEPTH, DV), 0.02)
    w_out_a = nrm((DEPTH, W_V, D), W_V ** -0.5)
    conv_b_w = nrm((DEPTH, CONV_B, W_B), CONV_B ** -0.5)
    w_out_b = nrm((DEPTH, W_B, D), W_B ** -0.5)
    conv_c_w = nrm((DEPTH, CONV_C, W_C), CONV_C ** -0.5)
    conv_c_b = nrm((DEPTH, W_C), 0.02)
    ln_c_w = 1.0 + nrm((DEPTH, W_C), 0.02)
    ln_c_b = nrm((DEPTH, W_C), 0.02)
    w_out_c = nrm((DEPTH, W_C, D), W_C ** -0.5)
    w_o = nrm((DEPTH, D, D), D ** -0.5)
    norm2_w = 1.0 + nrm((DEPTH, D), 0.02)
    w_rg = nrm((DEPTH, D, N_GROUPS), D ** -0.5)
    b_rg = nrm((DEPTH, N_GROUPS), 0.01)
    w_re = nrm((DEPTH, D, N_EXPERTS), D ** -0.5)
    b_re = nrm((DEPTH, N_EXPERTS), 0.01)
    w_gate_e = nrm((DEPTH, N_EXPERTS, D, D_FF), D ** -0.5)
    w_up_e = nrm((DEPTH, N_EXPERTS, D, D_FF), D ** -0.5)
    w_down_e = nrm((DEPTH, N_EXPERTS, D_FF, D), D_FF ** -0.5)
    final_norm_w = 1.0 + nrm((D,), 0.02)
    return {"x_prompt": x_prompt, "x_sample": x_sample, "state_delta": state_delta,
            "state_conv_a": state_conv_a, "state_conv_b": state_conv_b, "state_conv_c": state_conv_c,
            "meta_tokens": meta_tokens, "norm1_w": norm1_w, "w_in": w_in, "conv_a_w": conv_a_w,
            "a_log": a_log, "dt_bias": dt_bias, "onorm_a_w": onorm_a_w, "w_out_a": w_out_a,
            "conv_b_w": conv_b_w, "w_out_b": w_out_b, "conv_c_w": conv_c_w, "conv_c_b": conv_c_b,
            "ln_c_w": ln_c_w, "ln_c_b": ln_c_b, "w_out_c": w_out_c, "w_o": w_o, "norm2_w": norm2_w,
            "w_rg": w_rg, "b_rg": b_rg, "w_re": w_re, "b_re": b_re, "w_gate_e": w_gate_e,
            "w_up_e": w_up_e, "w_down_e": w_down_e, "final_norm_w": final_norm_w}


def reference(x_prompt, x_sample, state_delta, state_conv_a, state_conv_b, state_conv_c, meta_tokens,
              norm1_w, w_in, conv_a_w, a_log, dt_bias, onorm_a_w, w_out_a, conv_b_w, w_out_b, conv_c_w,
              conv_c_b, ln_c_w, ln_c_b, w_out_c, w_o, norm2_w, w_rg, b_rg, w_re, b_re, w_gate_e, w_up_e,
              w_down_e, final_norm_w):
    bp = x_prompt.shape[0]
    dt_ = x_prompt.dtype
    meta = jnp.broadcast_to(meta_tokens.astype(dt_)[None], (bp, N_META, D_MODEL))
    hp = jnp.concatenate([meta, x_prompt], axis=1)
    hs = x_sample
    dp, ap, bpl, cp = [], [], [], []
    ds, as_, bs, cs = [], [], [], []
    for l in range(DEPTH):
        lw = (norm1_w[l], w_in[l], conv_a_w[l], a_log[l], dt_bias[l], onorm_a_w[l], w_out_a[l],
              conv_b_w[l], w_out_b[l], conv_c_w[l], conv_c_b[l], ln_c_w[l], ln_c_b[l], w_out_c[l], w_o[l],
              norm2_w[l], w_rg[l], b_rg[l], w_re[l], b_re[l], w_gate_e[l], w_up_e[l], w_down_e[l])
        s0 = jnp.zeros((bp, H_A, DK, DV), state_delta.dtype)
        za = jnp.zeros((bp, CONV_A - 1, W_QKV), dt_)
        zb = jnp.zeros((bp, CONV_B - 1, W_B), dt_)
        zc = jnp.zeros((bp, CONV_C - 1, W_C), dt_)
        hp, s_p, a_p, b_p, c_p = _layer(hp, s0, za, zb, zc, N_META, *lw)
        hs, s_s, a_s, b_s, c_s = _layer(hs, state_delta[l], state_conv_a[l], state_conv_b[l], state_conv_c[l],
                                        hs.shape[1], *lw)
        dp.append(s_p); ap.append(a_p); bpl.append(b_p); cp.append(c_p)
        ds.append(s_s); as_.append(a_s); bs.append(b_s); cs.append(c_s)
    y_prompt = _rms(hp, final_norm_w)[:, N_META:]
    y_sample = _rms(hs, final_norm_w)
    return (y_prompt, y_sample, jnp.stack(dp), jnp.stack(ap), jnp.stack(bpl), jnp.stack(cp),
            jnp.stack(ds), jnp.stack(as_), jnp.stack(bs), jnp.stack(cs))
```

```python
import functools

import jax
import jax.numpy as jnp
from jax import lax
from jax.experimental import pallas as pl
from jax.experimental.pallas import tpu as pltpu

F32 = jnp.float32
BF16 = jnp.bfloat16
HIGHEST = lax.Precision.HIGHEST

EPS = 1e-6
LN_EPS = 1e-5
N_META = 16
H_A = 8
DK = 128
DV = 128
CHUNK = 64
LANES = 128
DEC_SEQ = 8
SUBSET = CHUNK // DEC_SEQ
HALO = 32
TOP_K = 2
NEG = -1e30

VMEM_LIMIT = 52 * 1024 * 1024


def _cp(dims, vmem=VMEM_LIMIT):
    return pltpu.CompilerParams(dimension_semantics=dims, vmem_limit_bytes=vmem)


def _silu(x):
    return x * (1.0 / (1.0 + jnp.exp(-x)))


def _sigmoid(x):
    return 1.0 / (1.0 + jnp.exp(-x))


def _pick_tile(n, target, mult):
    best = None
    for t in range(mult, min(n, target) + 1, mult):
        if n % t == 0:
            best = t
    assert best is not None, (n, target, mult)
    return best


def _inproj_kernel(x_ref, nw_ref, wa_ref, wb_ref, wab_ref, z_ref, ab_ref, h_scr, *, bounds, row_chunk):
    j = pl.program_id(1)
    tm = x_ref.shape[0]

    @pl.when(j == 0)
    def _():
        def body(r, c):
            rs = pl.ds(pl.multiple_of(r * row_chunk, row_chunk), row_chunk)
            x = x_ref[rs, :]
            h = x * lax.rsqrt(jnp.mean(x * x, axis=-1, keepdims=True) + EPS) * nw_ref[...]
            h_scr[rs, :] = h.astype(BF16)
            return c
        lax.fori_loop(0, tm // row_chunk, body, 0)
        ab_ref[...] = jnp.dot(h_scr[...], wab_ref[...], preferred_element_type=F32)

    a = jnp.dot(h_scr[...], wa_ref[...], preferred_element_type=F32)
    b_silu, b_id2, b_mul, b_glu, b_sig = bounds

    @pl.when((j < b_silu) | ((j >= b_id2) & (j < b_mul)))
    def _():
        z_ref[...] = a

    @pl.when((j >= b_silu) & (j < b_id2))
    def _():
        z_ref[...] = _silu(a)

    @pl.when((j >= b_mul) & (j < b_glu))
    def _():
        z_ref[...] = a * jnp.dot(h_scr[...], wb_ref[...], preferred_element_type=F32)

    @pl.when((j >= b_glu) & (j < b_sig))
    def _():
        z_ref[...] = a * _sigmoid(jnp.dot(h_scr[...], wb_ref[...], preferred_element_type=F32))

    @pl.when(j >= b_sig)
    def _():
        z_ref[...] = _sigmoid(a)


def _inproj(x, norm_w, w_main, w_ab, *, w_qkv, w_v, w_b, w_c, tm, tn):
    T, D = x.shape
    nq, nv, nb, nc = w_qkv // tn, w_v // tn, w_b // tn, w_c // tn
    b_silu = nq
    b_id2 = nq + nv
    b_mul = b_id2 + nb
    b_glu = b_mul + nb
    b_sig = b_glu + nc
    n_wblocks = w_main.shape[1] // tn
    n_out = n_wblocks - nb - nc
    park_lo = b_mul + nb
    park_hi = b_glu + nb + 2 * nc - 1

    def wa_map(i, j):
        return (0, j + jnp.where(j >= b_glu, nb, 0) + jnp.where(j >= b_sig, nc, 0))

    def wb_map(i, j):
        jb = jnp.where(j < b_mul, park_lo,
                       jnp.where(j < b_glu, j + nb,
                                 jnp.where(j < b_sig, j + nb + nc, park_hi)))
        return (0, jb)

    kern = functools.partial(_inproj_kernel, bounds=(b_silu, b_id2, b_mul, b_glu, b_sig), row_chunk=16)
    return pl.pallas_call(
        kern,
        grid=(T // tm, n_out),
        in_specs=[
            pl.BlockSpec((tm, D), lambda i, j: (i, 0)),
            pl.BlockSpec((1, D), lambda i, j: (0, 0)),
            pl.BlockSpec((D, tn), wa_map),
            pl.BlockSpec((D, tn), wb_map),
            pl.BlockSpec((D, LANES), lambda i, j: (0, 0)),
        ],
        out_specs=[
            pl.BlockSpec((tm, tn), lambda i, j: (i, j)),
            pl.BlockSpec((tm, LANES), lambda i, j: (i, 0)),
        ],
        out_shape=[jax.ShapeDtypeStruct((T, n_out * tn), F32), jax.ShapeDtypeStruct((T, LANES), F32)],
        scratch_shapes=[pltpu.VMEM((tm, D), BF16)],
        compiler_params=_cp(("arbitrary", "arbitrary")),
        name="inproj",
    )(x, norm_w, w_main, w_main, w_ab)


def _conv_taps(xp_scr, base, rows, step, w_ref, width):
    acc = None
    for j in range(width):
        off = base - (width - 1 - j) * step
        term = xp_scr[pl.ds(off, rows), :] * w_ref[j:j + 1, :]
        acc = term if acc is None else acc + term
    return acc


def _conv_tile(i, n_s_tiles, x_ref, halo_ref, hist_ref, w_ref, xp_scr, y_scr, width):
    tr = x_ref.shape[0]
    hsub = (width - 1) * SUBSET
    nsub = tr // CHUNK

    @pl.when(i < n_s_tiles)
    def _():
        for s in range(nsub):
            xp_scr[0:hsub, :] = hist_ref[s * hsub:(s + 1) * hsub, :]
            xp_scr[hsub:hsub + CHUNK, :] = x_ref[s * CHUNK:(s + 1) * CHUNK, :]
            y_scr[s * CHUNK:(s + 1) * CHUNK, :] = _conv_taps(xp_scr, hsub, CHUNK, SUBSET, w_ref, width)

    @pl.when(i >= n_s_tiles)
    def _():
        halo = halo_ref[...]
        xp_scr[0:HALO, :] = jnp.where(i == n_s_tiles, jnp.zeros_like(halo), halo)
        xp_scr[HALO:HALO + tr, :] = x_ref[...]
        y_scr[...] = _conv_taps(xp_scr, HALO, tr, 1, w_ref, width)


def _softplus(x):
    return jnp.maximum(x, 0.0) + jnp.log(1.0 + jnp.exp(-jnp.abs(x)))


def _conva_kernel(x_ref, halo_ref, hist_ref, w_ref, ab_ref, mask_ref, adt_ref, o_ref, gb_ref, xp_scr, y_scr,
                  *, n_s_tiles, width):
    i = pl.program_id(0)
    sec = pl.program_id(1)
    _conv_tile(i, n_s_tiles, x_ref, halo_ref, hist_ref, w_ref, xp_scr, y_scr, width)

    @pl.when(sec == 0)
    def _():
        ab = ab_ref[...]
        lane = lax.broadcasted_iota(jnp.int32, ab.shape, 1)
        g = -jnp.exp(adt_ref[0:1, :]) * _softplus(ab + adt_ref[1:2, :])
        gb_ref[...] = jnp.where(lane < H_A, g, _sigmoid(ab)) * mask_ref[...]

    scale = jnp.where(sec == 0, DK ** -0.5, 1.0).astype(F32)
    for h in range(H_A):
        hs = slice(h * DK, (h + 1) * DK)
        y = _silu(y_scr[:, hs])
        yn = y * (lax.rsqrt(jnp.sum(y * y, axis=-1, keepdims=True) + EPS) * scale)
        o_ref[:, hs] = jnp.where(sec == 2, y, yn)


def _convb_kernel(x_ref, halo_ref, hist_ref, w_ref, gate_ref, o_ref, xp_scr, y_scr, *, n_s_tiles, width):
    i = pl.program_id(0)
    _conv_tile(i, n_s_tiles, x_ref, halo_ref, hist_ref, w_ref, xp_scr, y_scr, width)
    o_ref[...] = (gate_ref[...] * y_scr[...]).astype(o_ref.dtype)


def _convc_kernel(x_ref, halo_ref, hist_ref, w_ref, cb_ref, lnw_ref, lnb_ref, o_ref, xp_scr, y_scr,
                  *, n_s_tiles, width):
    i = pl.program_id(0)
    _conv_tile(i, n_s_tiles, x_ref, halo_ref, hist_ref, w_ref, xp_scr, y_scr, width)
    y = y_scr[...] + cb_ref[...]
    mu = jnp.mean(y, axis=-1, keepdims=True)
    yc = y - mu
    var = jnp.mean(yc * yc, axis=-1, keepdims=True)
    yn = yc * lax.rsqrt(var + LN_EPS) * lnw_ref[...] + lnb_ref[...]
    o_ref[...] = _silu(yn).astype(o_ref.dtype)


def _conv_specs(tr, cw, width, col_of, n_s_tiles, sec_axis):
    hrows = (tr // CHUNK) * (width - 1) * SUBSET
    hb = tr // HALO
    last_hist = max(n_s_tiles - 1, 0)
    if sec_axis:
        x_map = lambda i, s: (i, col_of(s))
        halo_map = lambda i, s: (jnp.maximum(i * hb - 1, 0), col_of(s))
        hist_map = lambda i, s: (jnp.minimum(i, last_hist), s)
    else:
        x_map = lambda i: (i, col_of(0))
        halo_map = lambda i: (jnp.maximum(i * hb - 1, 0), col_of(0))
        hist_map = lambda i: (jnp.minimum(i, last_hist), 0)
    return [pl.BlockSpec((tr, cw), x_map), pl.BlockSpec((HALO, cw), halo_map), pl.BlockSpec((hrows, cw), hist_map)]


def _dot(a, b):
    return jnp.dot(a.astype(BF16), b.astype(BF16), preferred_element_type=F32)


def _dot_nt(a, b):
    return lax.dot_general(a.astype(BF16), b.astype(BF16), (((1,), (1,)), ((), ())), preferred_element_type=F32)


def _dot_hi(a, b):
    return jnp.dot(a, b, preferred_element_type=F32, precision=HIGHEST)


def _unit_lower_inverse(m_strict, eye, n_doublings):
    p = -m_strict
    t = eye + p
    for _ in range(n_doublings):
        p = _dot_hi(p, p)
        t = t + _dot_hi(t, p)
    return t


def _delta_head_common(q, k, v, gc, gr, beta, lmask, strict, eye, n_doublings):
    decay = jnp.exp(jnp.where(lmask, gc - gr, NEG))
    kk = _dot_nt(k, k)
    m = jnp.where(strict, beta * kk * decay, 0.0)
    tinv = _unit_lower_inverse(m, eye, n_doublings)
    eg = jnp.exp(gc)
    rhs = jnp.concatenate([beta * v, (beta * eg) * k], axis=1)
    sol = _dot_hi(tinv, rhs)
    qk = _dot_nt(q, k) * decay
    return sol[:, :DV], sol[:, DV:], qk, eg


def _delta_kernel(q_ref, k_ref, v_ref, gb_ref, sga_ref, onw_ref, s0s_ref, ya_ref, ss_ref, sp_ref, s_scr,
                  *, n_s_steps, n_chunks):
    s = pl.program_id(0)
    C = CHUNK
    row = lax.broadcasted_iota(jnp.int32, (C, C), 0)
    col = lax.broadcasted_iota(jnp.int32, (C, C), 1)
    eye = (row == col).astype(F32)
    gb = gb_ref[...]
    gbt = gb.T[0:H_A, :]

    def finish(o, h):
        hs = slice(h * DV, (h + 1) * DV)
        on = o * lax.rsqrt(jnp.mean(o * o, axis=-1, keepdims=True) + EPS) * onw_ref[...]
        ya_ref[:, hs] = (on * sga_ref[:, hs]).astype(ya_ref.dtype)

    @pl.when(s < n_s_steps)
    def _():
        same = ((row - col) & (DEC_SEQ - 1)) == 0
        lmask = same & (row >= col)
        strict = same & (row > col)
        lm = lmask.astype(F32)
        last = (col == (C - DEC_SEQ) + (row & (DEC_SEQ - 1))).astype(F32)
        gc_all = _dot_hi(lm, gb)
        gr_all = lax.dot_general(gbt, lm, (((1,), (1,)), ((), ())), preferred_element_type=F32, precision=HIGHEST)
        gl_all = _dot_hi(last, gc_all)
        rsub = lax.broadcasted_iota(jnp.int32, (C, 1), 0) & (DEC_SEQ - 1)
        for h in range(H_A):
            hs = slice(h * DK, (h + 1) * DK)
            q, k, v = q_ref[:, hs], k_ref[:, hs], v_ref[:, hs]
            gc = gc_all[:, h:h + 1]
            gr = gr_all[h:h + 1, :]
            beta = gb[:, H_A + h:H_A + h + 1]
            sol_v, sol_k, qk, eg = _delta_head_common(q, k, v, gc, gr, beta, lmask, strict, eye, 2)
            lhs = jnp.concatenate([sol_k, q], axis=0)
            sk = jnp.zeros((2 * C, DV), F32)
            rsub2 = jnp.concatenate([rsub, rsub], axis=0)
            for i in range(SUBSET):
                sk = sk + jnp.where(rsub2 == i, _dot(lhs, s0s_ref[i, h]), 0.0)
            u = sol_v - sk[:C]
            o = eg * sk[C:] + _dot(qk, u)
            gl = gl_all[:, h:h + 1]
            kd = k * jnp.exp(gl - gc)
            for i in range(SUBSET):
                kdi = jnp.where(rsub == i, kd, 0.0)
                gli = gc_all[C - DEC_SEQ + i:C - DEC_SEQ + i + 1, h:h + 1]
                ss_ref[i, h] = jnp.exp(gli) * s0s_ref[i, h] + _dot(kdi.T, u)
            finish(o, h)

    @pl.when(s >= n_s_steps)
    def _():
        c = (s - n_s_steps) % n_chunks

        @pl.when(c == 0)
        def _():
            s_scr[...] = jnp.zeros_like(s_scr)

        lmask = row >= col
        strict = row > col
        lm = lmask.astype(F32)
        gc_all = _dot_hi(lm, gb)
        gr_all = lax.dot_general(gbt, lm, (((1,), (1,)), ((), ())), preferred_element_type=F32, precision=HIGHEST)
        for h in range(H_A):
            hs = slice(h * DK, (h + 1) * DK)
            q, k, v = q_ref[:, hs], k_ref[:, hs], v_ref[:, hs]
            gc = gc_all[:, h:h + 1]
            gr = gr_all[h:h + 1, :]
            beta = gb[:, H_A + h:H_A + h + 1]
            sol_v, sol_k, qk, eg = _delta_head_common(q, k, v, gc, gr, beta, lmask, strict, eye, 5)
            S = s_scr[h]
            sk = _dot(jnp.concatenate([sol_k, q], axis=0), S)
            u = sol_v - sk[:C]
            o = eg * sk[C:] + _dot(qk, u)
            gl = gc_all[C - 1:C, h:h + 1]
            kd = k * jnp.exp(gl - gc)
            s_scr[h] = jnp.exp(gl) * S + _dot(kd.T, u)
            finish(o, h)

        @pl.when(c == n_chunks - 1)
        def _():
            sp_ref[0] = s_scr[...]


def _delta(qkv, gb, z, onorm_w, s0_sample, *, n_s_steps, n_batch, n_chunks, sga_col):
    T = qkv.shape[0]
    W = H_A * DK
    n_steps = n_s_steps + n_batch * n_chunks
    last_s = max(n_s_steps - 1, 0)
    kern = functools.partial(_delta_kernel, n_s_steps=n_s_steps, n_chunks=n_chunks)
    state_blk = (SUBSET, H_A, DK, DV)
    return pl.pallas_call(
        kern,
        grid=(n_steps,),
        in_specs=[
            pl.BlockSpec((CHUNK, W), lambda s: (s, 0)),
            pl.BlockSpec((CHUNK, W), lambda s: (s, 1)),
            pl.BlockSpec((CHUNK, W), lambda s: (s, 2)),
            pl.BlockSpec((CHUNK, LANES), lambda s: (s, 0)),
            pl.BlockSpec((CHUNK, W), lambda s: (s, sga_col)),
            pl.BlockSpec((1, DV), lambda s: (0, 0)),
            pl.BlockSpec(state_blk, lambda s: (jnp.minimum(s, last_s), 0, 0, 0)),
        ],
        out_specs=[
            pl.BlockSpec((CHUNK, W), lambda s: (s, 0)),
            pl.BlockSpec(state_blk, lambda s: (jnp.minimum(s, last_s), 0, 0, 0)),
            pl.BlockSpec((1, H_A, DK, DV), lambda s: (jnp.maximum(s - n_s_steps, 0) // n_chunks, 0, 0, 0)),
        ],
        out_shape=[
            jax.ShapeDtypeStruct((T, W), BF16),
            jax.ShapeDtypeStruct(s0_sample.shape, F32),
            jax.ShapeDtypeStruct((n_batch, H_A, DK, DV), F32),
        ],
        scratch_shapes=[pltpu.VMEM((H_A, DK, DV), F32)],
        compiler_params=_cp(("arbitrary",)),
        name="delta",
    )(qkv, qkv, qkv, gb, z, onorm_w, s0_sample)


def _merge_kernel(ya_ref, yb_ref, yc_ref, wa_ref, wb_ref, wc_ref, g0_ref, g1_ref, g2_ref, m_ref):
    m = g0_ref[...] * jnp.dot(ya_ref[...], wa_ref[...], preferred_element_type=F32)
    m = m + g1_ref[...] * jnp.dot(yb_ref[...], wb_ref[...], preferred_element_type=F32)
    m = m + g2_ref[...] * jnp.dot(yc_ref[...], wc_ref[...], preferred_element_type=F32)
    m_ref[...] = m.astype(m_ref.dtype)


def _merge(ya, yb, yc, wa, wb, wc, z, *, gs_col, tm, tn):
    T, K = ya.shape
    D = wa.shape[1]
    nb = D // tn
    g0 = gs_col // tn
    act = pl.BlockSpec((tm, K), lambda i, j: (i, 0))
    wsp = pl.BlockSpec((K, tn), lambda i, j: (0, j))
    return pl.pallas_call(
        _merge_kernel,
        grid=(T // tm, nb),
        in_specs=[act, act, act, wsp, wsp, wsp,
                  pl.BlockSpec((tm, tn), lambda i, j: (i, g0 + j)),
                  pl.BlockSpec((tm, tn), lambda i, j: (i, g0 + nb + j)),
                  pl.BlockSpec((tm, tn), lambda i, j: (i, g0 + 2 * nb + j))],
        out_specs=pl.BlockSpec((tm, tn), lambda i, j: (i, j)),
        out_shape=jax.ShapeDtypeStruct((T, D), BF16),
        compiler_params=_cp(("arbitrary", "arbitrary")),
        name="merge",
    )(ya, yb, yc, wa, wb, wc, z, z, z)


def _oproj_kernel(m_ref, w_ref, x_ref, o_ref):
    o_ref[...] = x_ref[...] + jnp.dot(m_ref[...], w_ref[...], preferred_element_type=F32)


def _oproj(m, w_o, x, *, tm, tn):
    T, D = x.shape
    return pl.pallas_call(
        _oproj_kernel,
        grid=(T // tm, D // tn),
        in_specs=[pl.BlockSpec((tm, D), lambda i, j: (i, 0)),
                  pl.BlockSpec((D, tn), lambda i, j: (0, j)),
                  pl.BlockSpec((tm, tn), lambda i, j: (i, j))],
        out_specs=pl.BlockSpec((tm, tn), lambda i, j: (i, j)),
        out_shape=jax.ShapeDtypeStruct((T, D), F32),
        compiler_params=_cp(("arbitrary", "arbitrary")),
        name="oproj",
    )(m, w_o, x)


def _router_kernel(x_ref, nw_ref, wr_ref, br_ref, ri_ref, rw_ref, cnt_ref, carry_scr, *, n_groups, per_group):
    i = pl.program_id(0)
    tr = x_ref.shape[0]

    @pl.when(i == 0)
    def _():
        carry_scr[...] = jnp.zeros_like(carry_scr)

    x = x_ref[...]
    h = x * lax.rsqrt(jnp.mean(x * x, axis=-1, keepdims=True) + EPS) * nw_ref[...]
    logits = _dot_hi(h, wr_ref[...]) + br_ref[...]
    lane = lax.broadcasted_iota(jnp.int32, logits.shape, 1)
    big = jnp.int32(1 << 20)

    def argmax_first(vals):
        m = jnp.max(vals, axis=-1, keepdims=True)
        idx = jnp.min(jnp.where(vals == m, lane, big), axis=-1, keepdims=True)
        return m, idx

    is_g = lane < n_groups
    mg, gsel = argmax_first(jnp.where(is_g, logits, NEG))
    pg = 1.0 / jnp.sum(jnp.where(is_g, jnp.exp(logits - mg), 0.0), axis=-1, keepdims=True)
    lo = n_groups + gsel * per_group
    in_group = (lane >= lo) & (lane < lo + per_group)
    le = jnp.where(in_group, logits, NEG)
    m1, i1 = argmax_first(le)
    m2, i2 = argmax_first(jnp.where(lane == i1, NEG, le))
    e21 = jnp.exp(m2 - m1)
    w1 = pg / (1.0 + e21)
    w2 = w1 * e21

    oh = ((lane == i1) | (lane == i2)).astype(F32)
    r_i = lax.broadcasted_iota(jnp.int32, (tr, tr), 0)
    c_i = lax.broadcasted_iota(jnp.int32, (tr, tr), 1)
    before = _dot((r_i > c_i).astype(F32), oh) + carry_scr[...]
    rank1 = jnp.sum(jnp.where(lane == i1, before, 0.0), axis=-1, keepdims=True)
    rank2 = jnp.sum(jnp.where(lane == i2, before, 0.0), axis=-1, keepdims=True)
    carry_scr[...] = carry_scr[...] + jnp.sum(oh, axis=0, keepdims=True)
    cnt_ref[...] = jnp.broadcast_to(carry_scr[...], cnt_ref.shape)

    ri = jnp.where(lane == 0, i1 - n_groups,
                   jnp.where(lane == 1, i2 - n_groups,
                             jnp.where(lane == 2, rank1.astype(jnp.int32),
                                       jnp.where(lane == 3, rank2.astype(jnp.int32), 0))))
    ri_ref[...] = ri
    rw_ref[...] = jnp.where(lane == 0, w1, jnp.where(lane == 1, w2, 0.0))


def _router(x, norm_w, w_r, b_r, *, n_groups, per_group, tr):
    T, D = x.shape
    kern = functools.partial(_router_kernel, n_groups=n_groups, per_group=per_group)
    return pl.pallas_call(
        kern,
        grid=(T // tr,),
        in_specs=[pl.BlockSpec((tr, D), lambda i: (i, 0)),
                  pl.BlockSpec((1, D), lambda i: (0, 0)),
                  pl.BlockSpec((D, LANES), lambda i: (0, 0)),
                  pl.BlockSpec((1, LANES), lambda i: (0, 0))],
        out_specs=[pl.BlockSpec((tr, LANES), lambda i: (i, 0)),
                   pl.BlockSpec((tr, LANES), lambda i: (i, 0)),
                   pl.BlockSpec((8, LANES), lambda i: (0, 0))],
        out_shape=[jax.ShapeDtypeStruct((T, LANES), jnp.int32),
                   jax.ShapeDtypeStruct((T, LANES), F32),
                   jax.ShapeDtypeStruct((8, LANES), F32)],
        scratch_shapes=[pltpu.VMEM((1, LANES), F32)],
        compiler_params=_cp(("arbitrary",)),
        name="router",
    )(x, norm_w, w_r, b_r)


def _row_copy(src, src_row, dst, dst_row, sem):
    return pltpu.make_async_copy(src.at[pl.ds(src_row, 1), :], dst.at[pl.ds(dst_row, 1), :], sem)


def _dispatch_kernel(p1_ref, p2_ref, x_ref, nw_ref, xs_in_ref, xs_ref, h_scr, sem):
    del xs_in_ref
    i = pl.program_id(0)
    tr = x_ref.shape[0]
    x = x_ref[...]
    h_scr[...] = x * lax.rsqrt(jnp.mean(x * x, axis=-1, keepdims=True) + EPS) * nw_ref[...]
    base = i * tr

    def issue(r, c):
        _row_copy(h_scr, r, xs_ref, p1_ref[base + r], sem.at[0]).start()
        _row_copy(h_scr, r, xs_ref, p2_ref[base + r], sem.at[1]).start()
        return c
    lax.fori_loop(0, tr, issue, 0)
    pltpu.make_async_copy(h_scr, xs_ref.at[pl.ds(0, tr), :], sem.at[0]).wait()
    pltpu.make_async_copy(h_scr, xs_ref.at[pl.ds(0, tr), :], sem.at[1]).wait()


def _dispatch(pos1, pos2, x, norm_w, xs_init, *, tr):
    T, D = x.shape
    grid_spec = pltpu.PrefetchScalarGridSpec(
        num_scalar_prefetch=2,
        grid=(T // tr,),
        in_specs=[pl.BlockSpec((tr, D), lambda i, p1, p2: (i, 0)),
                  pl.BlockSpec((1, D), lambda i, p1, p2: (0, 0)),
                  pl.BlockSpec(memory_space=pl.ANY)],
        out_specs=pl.BlockSpec(memory_space=pl.ANY),
        scratch_shapes=[pltpu.VMEM((tr, D), F32), pltpu.SemaphoreType.DMA((2,))],
    )
    return pl.pallas_call(
        _dispatch_kernel,
        grid_spec=grid_spec,
        out_shape=jax.ShapeDtypeStruct(xs_init.shape, F32),
        input_output_aliases={4: 0},
        compiler_params=_cp(("arbitrary",)),
        name="dispatch",
    )(pos1, pos2, x, norm_w, xs_init)


def _expert_kernel(te_ref, nu_ref, xs_ref, wg_ref, wu_ref, wd_ref, o_ref, wg_scr, wu_scr, wd_scr):
    i = pl.program_id(0)
    prev = te_ref[jnp.maximum(i - 1, 0)]

    @pl.when((i == 0) | (te_ref[i] != prev))
    def _():
        wg_scr[...] = wg_ref[...].astype(BF16)
        wu_scr[...] = wu_ref[...].astype(BF16)
        wd_scr[...] = wd_ref[...].astype(BF16)

    @pl.when(i < nu_ref[0])
    def _():
        h = xs_ref[...].astype(BF16)
        g = jnp.dot(h, wg_scr[...], preferred_element_type=F32)
        u = jnp.dot(h, wu_scr[...], preferred_element_type=F32)
        a = (_silu(g) * u).astype(BF16)
        o_ref[...] = jnp.dot(a, wd_scr[...], preferred_element_type=F32)

    @pl.when(i >= nu_ref[0])
    def _():
        o_ref[...] = jnp.zeros_like(o_ref)


def _experts(tile_e, n_used, xs, w_gate, w_up, w_down, *, tg):
    P, D = xs.shape
    E, _, F = w_gate.shape
    n_tiles = P // tg

    def xs_map(i, te, nu):
        return (jnp.minimum(i, nu[0] - 1), 0)

    grid_spec = pltpu.PrefetchScalarGridSpec(
        num_scalar_prefetch=2,
        grid=(n_tiles,),
        in_specs=[pl.BlockSpec((tg, D), xs_map),
                  pl.BlockSpec((None, D, F), lambda i, te, nu: (te[i], 0, 0)),
                  pl.BlockSpec((None, D, F), lambda i, te, nu: (te[i], 0, 0)),
                  pl.BlockSpec((None, F, D), lambda i, te, nu: (te[i], 0, 0))],
        out_specs=pl.BlockSpec((tg, D), lambda i, te, nu: (i, 0)),
        scratch_shapes=[pltpu.VMEM((D, F), BF16), pltpu.VMEM((D, F), BF16), pltpu.VMEM((F, D), BF16)],
    )
    return pl.pallas_call(
        _expert_kernel,
        grid_spec=grid_spec,
        out_shape=jax.ShapeDtypeStruct((P, D), F32),
        compiler_params=_cp(("arbitrary",)),
        name="experts",
    )(tile_e, n_used, xs, w_gate, w_up, w_down)


def _combine_kernel(p1_ref, p2_ref, x_ref, rw_ref, mask_ref, fw_ref, eo_ref, x2_ref, y_ref, b1_scr, b2_scr, sem):
    i = pl.program_id(0)
    tr = x_ref.shape[0]
    base = i * tr

    def issue(r, c):
        _row_copy(eo_ref, p1_ref[base + r], b1_scr, r, sem.at[0]).start()
        _row_copy(eo_ref, p2_ref[base + r], b2_scr, r, sem.at[1]).start()
        return c
    lax.fori_loop(0, tr, issue, 0)
    pltpu.make_async_copy(eo_ref.at[pl.ds(0, tr), :], b1_scr, sem.at[0]).wait()
    pltpu.make_async_copy(eo_ref.at[pl.ds(0, tr), :], b2_scr, sem.at[1]).wait()
    rw = rw_ref[...]
    x2 = (x_ref[...] + rw[:, 0:1] * b1_scr[...] + rw[:, 1:2] * b2_scr[...]) * mask_ref[:, 0:1]
    x2_ref[...] = x2
    y_ref[...] = x2 * lax.rsqrt(jnp.mean(x2 * x2, axis=-1, keepdims=True) + EPS) * fw_ref[...]


def _combine(pos1, pos2, x, rw, mask, final_w, eo, *, tr):
    T, D = x.shape
    row = lambda w: pl.BlockSpec((tr, w), lambda i, p1, p2: (i, 0))
    grid_spec = pltpu.PrefetchScalarGridSpec(
        num_scalar_prefetch=2,
        grid=(T // tr,),
        in_specs=[row(D), row(LANES), row(LANES),
                  pl.BlockSpec((1, D), lambda i, p1, p2: (0, 0)),
                  pl.BlockSpec(memory_space=pl.ANY)],
        out_specs=[row(D), row(D)],
        scratch_shapes=[pltpu.VMEM((tr, D), F32), pltpu.VMEM((tr, D), F32), pltpu.SemaphoreType.DMA((2,))],
    )
    return pl.pallas_call(
        _combine_kernel,
        grid_spec=grid_spec,
        out_shape=[jax.ShapeDtypeStruct((T, D), F32), jax.ShapeDtypeStruct((T, D), F32)],
        compiler_params=_cp(("arbitrary",)),
        name="combine",
    )(pos1, pos2, x, rw, mask, final_w, eo)


def _sample_to_rows(a):
    nb, L, C = a.shape
    return a.reshape(nb // SUBSET, SUBSET, L, C).transpose(0, 2, 1, 3).reshape(nb * L, C)


def _rows_to_sample(r, L):
    n, C = r.shape
    nb = n // L
    return r.reshape(nb // SUBSET, L, SUBSET, C).transpose(0, 2, 1, 3).reshape(nb, L, C)


def kernel(x_prompt, x_sample, state_delta, state_conv_a, state_conv_b, state_conv_c, meta_tokens, norm1_w, w_in, conv_a_w, a_log, dt_bias, onorm_a_w, w_out_a, conv_b_w, w_out_b, conv_c_w, conv_c_b, ln_c_w, ln_c_b, w_out_c, w_o, norm2_w, w_rg, b_rg, w_re, b_re, w_gate_e, w_up_e, w_down_e, final_norm_w):
    B, SEQ, D = x_prompt.shape
    NB, L, _ = x_sample.shape
    depth = w_in.shape[0]
    W_QKV = conv_a_w.shape[-1]
    W_V = H_A * DV
    W_B = conv_b_w.shape[-1]
    W_C = conv_c_w.shape[-1]
    CA, CB, CC = conv_a_w.shape[1], conv_b_w.shape[1], conv_c_w.shape[1]
    G = w_rg.shape[-1]
    E = w_re.shape[-1]
    assert L == DEC_SEQ and NB % SUBSET == 0 and SEQ % CHUNK == 0 and W_QKV == 3 * W_V
    assert W_V == W_B == W_C and max(CA, CB, CC) - 1 <= min(HALO, CHUNK - N_META)
    assert G + E <= LANES and D % LANES == 0

    LP = CHUNK + SEQ
    lead = CHUNK - N_META
    Ts, Tp = NB * L, B * LP
    T = Ts + Tp
    n_chunks = LP // CHUNK
    n_s_steps = Ts // CHUNK

    tr = _pick_tile(_gcd(Ts, Tp), 256, CHUNK)
    tm = _pick_tile(T, 1184, 16)
    tn = 512
    tg = 256
    n_s_tiles = Ts // tr

    dt_ = x_prompt.dtype
    lead_rows = jnp.concatenate([jnp.zeros((lead, D), dt_), meta_tokens.astype(dt_)], axis=0)
    xp = jnp.concatenate([jnp.broadcast_to(lead_rows[None], (B, CHUNK, D)), x_prompt], axis=1).reshape(Tp, D)
    x = jnp.concatenate([_sample_to_rows(x_sample), xp], axis=0)
    pos_in_seq = jnp.arange(Tp, dtype=jnp.int32) % LP
    real = jnp.concatenate([jnp.ones((Ts,), F32), (pos_in_seq >= lead).astype(F32)])
    mask = jnp.broadcast_to(real[:, None], (T, LANES))

    n_ab = 2 * H_A
    w_main = jnp.concatenate([w_in[:, :, :W_QKV], w_in[:, :, W_QKV + n_ab:]], axis=2).astype(BF16)
    w_ab = jnp.pad(w_in[:, :, W_QKV:W_QKV + n_ab], ((0, 0), (0, 0), (0, LANES - n_ab))).astype(BF16)
    w_r = jnp.pad(jnp.concatenate([w_rg, w_re], axis=2), ((0, 0), (0, 0), (0, LANES - G - E)))
    b_r = jnp.pad(jnp.concatenate([b_rg, b_re], axis=1), ((0, 0), (0, LANES - G - E)))[:, None, :]
    adt = jnp.pad(jnp.stack([a_log, dt_bias], axis=1), ((0, 0), (0, 0), (0, LANES - H_A)))
    adt = jnp.pad(adt, ((0, 0), (0, 6), (0, 0)))
    woa, wob, woc, wo = (w.astype(BF16) for w in (w_out_a, w_out_b, w_out_c, w_o))

    col_sga, col_gb, col_u, col_gl, col_gs = W_QKV, W_QKV + W_V, W_QKV + W_V + W_B, W_QKV + W_V + 2 * W_B, W_QKV + W_V + 2 * W_B + W_C
    cw = W_V
    P = TOP_K * T + E * tg
    n_tiles = P // tg

    dp, ap, bpl, cpl, ds, as_, bs, cs = [], [], [], [], [], [], [], []
    y = None
    for l in range(depth):
        z, ab = _inproj(x, norm1_w[l][None], w_main[l], w_ab[l],
                        w_qkv=W_QKV, w_v=W_V, w_b=W_B, w_c=W_C, tm=tm, tn=tn)

        def hist_rows(st):
            nb, hw, C = st.shape
            return st.reshape(nb // SUBSET, SUBSET, hw, C).transpose(0, 2, 1, 3).reshape(nb * hw, C)
        hist_a, hist_b, hist_c = hist_rows(state_conv_a[l]), hist_rows(state_conv_b[l]), hist_rows(state_conv_c[l])

        xs_, hs_, hi_ = _conv_specs(tr, cw, CA, lambda s: s, n_s_tiles, True)
        qkv, gb = pl.pallas_call(
            functools.partial(_conva_kernel, n_s_tiles=n_s_tiles, width=CA),
            grid=(T // tr, W_QKV // cw),
            in_specs=[xs_, hs_, hi_,
                      pl.BlockSpec((CA, cw), lambda i, s: (0, s)),
                      pl.BlockSpec((tr, LANES), lambda i, s: (i, 0)),
                      pl.BlockSpec((tr, LANES), lambda i, s: (i, 0)),
                      pl.BlockSpec((8, LANES), lambda i, s: (0, 0))],
            out_specs=[pl.BlockSpec((tr, cw), lambda i, s: (i, s)),
                       pl.BlockSpec((tr, LANES), lambda i, s: (i, 0))],
            out_shape=[jax.ShapeDtypeStruct((T, W_QKV), F32), jax.ShapeDtypeStruct((T, LANES), F32)],
            scratch_shapes=[pltpu.VMEM((HALO + tr, cw), F32), pltpu.VMEM((tr, cw), F32)],
            compiler_params=_cp(("arbitrary", "arbitrary")),
            name="conv_a",
        )(z, z, hist_a, conv_a_w[l], ab, mask, adt[l])

        ya, s_s, s_p = _delta(qkv, gb, z, onorm_a_w[l][None], state_delta[l],
                              n_s_steps=n_s_steps, n_batch=B, n_chunks=n_chunks, sga_col=col_sga // cw)

        xs_, hs_, hi_ = _conv_specs(tr, cw, CB, lambda s: col_u // cw, n_s_tiles, False)
        yb = pl.pallas_call(
            functools.partial(_convb_kernel, n_s_tiles=n_s_tiles, width=CB),
            grid=(T // tr,),
            in_specs=[xs_, hs_, hi_,
                      pl.BlockSpec((CB, cw), lambda i: (0, 0)),
                      pl.BlockSpec((tr, cw), lambda i: (i, col_gb // cw))],
            out_specs=pl.BlockSpec((tr, cw), lambda i: (i, 0)),
            out_shape=jax.ShapeDtypeStruct((T, cw), BF16),
            scratch_shapes=[pltpu.VMEM((HALO + tr, cw), F32), pltpu.VMEM((tr, cw), F32)],
            compiler_params=_cp(("arbitrary",)),
            name="conv_b",
        )(z, z, hist_b, conv_b_w[l], z)

        xs_, hs_, hi_ = _conv_specs(tr, cw, CC, lambda s: col_gl // cw, n_s_tiles, False)
        hc_rows = max(HALO + tr, (CC - 1) * SUBSET + CHUNK)
        yc = pl.pallas_call(
            functools.partial(_convc_kernel, n_s_tiles=n_s_tiles, width=CC),
            grid=(T // tr,),
            in_specs=[xs_, hs_, hi_,
                      pl.BlockSpec((CC, cw), lambda i: (0, 0)),
                      pl.BlockSpec((1, cw), lambda i: (0, 0)),
                      pl.BlockSpec((1, cw), lambda i: (0, 0)),
                      pl.BlockSpec((1, cw), lambda i: (0, 0))],
            out_specs=pl.BlockSpec((tr, cw), lambda i: (i, 0)),
            out_shape=jax.ShapeDtypeStruct((T, cw), BF16),
            scratch_shapes=[pltpu.VMEM((hc_rows, cw), F32), pltpu.VMEM((tr, cw), F32)],
            compiler_params=_cp(("arbitrary",)),
            name="conv_c",
        )(z, z, hist_c, conv_c_w[l], conv_c_b[l][None], ln_c_w[l][None], ln_c_b[l][None])

        m = _merge(ya, yb, yc, woa[l], wob[l], woc[l], z, gs_col=col_gs, tm=tm, tn=tn)
        x1 = _oproj(m, wo[l], x, tm=tm, tn=tn)

        ri, rw, cnt = _router(x1, norm2_w[l][None], w_r[l], b_r[l], n_groups=G, per_group=E // G, tr=tr)
        counts = cnt[0, G:G + E].astype(jnp.int32)
        padded = ((counts + tg - 1) // tg) * tg
        ends = jnp.cumsum(padded)
        offs = ends - padded
        pos1 = offs[ri[:, 0]] + ri[:, 2]
        pos2 = offs[ri[:, 1]] + ri[:, 3]
        n_used = (ends[-1] // tg).astype(jnp.int32)
        tile_start = jnp.arange(n_tiles, dtype=jnp.int32) * tg
        tile_e = jnp.sum((tile_start[:, None] >= ends[None, :]).astype(jnp.int32), axis=1)
        last_e = jnp.sum((((n_used - 1) * tg) >= ends).astype(jnp.int32))
        tile_e = jnp.where(jnp.arange(n_tiles) < n_used, tile_e, last_e).astype(jnp.int32)

        xs = _dispatch(pos1, pos2, x1, norm2_w[l][None], jnp.zeros((P, D), F32), tr=tr)
        eo = _experts(tile_e, n_used.reshape(1), xs, w_gate_e[l], w_up_e[l], w_down_e[l], tg=tg)
        x, y = _combine(pos1, pos2, x1, rw, mask, final_norm_w[None], eo, tr=tr)

        zp = z[Ts:].reshape(B, LP, -1)
        zs = z[:Ts]
        dp.append(s_p)
        ap.append(zp[:, LP - (CA - 1):, :W_QKV])
        bpl.append(zp[:, LP - (CB - 1):, col_u:col_u + W_B])
        cpl.append(zp[:, LP - (CC - 1):, col_gl:col_gl + W_C])
        ds.append(s_s)

        def new_hist(st, cols, width):
            seq = _rows_to_sample(zs[:, cols[0]:cols[1]], L)
            return jnp.concatenate([st.astype(dt_), seq], axis=1)[:, L:]
        as_.append(new_hist(state_conv_a[l], (0, W_QKV), CA))
        bs.append(new_hist(state_conv_b[l], (col_u, col_u + W_B), CB))
        cs.append(new_hist(state_conv_c[l], (col_gl, col_gl + W_C), CC))

    y_sample = _rows_to_sample(y[:Ts], L)
    y_prompt = y[Ts:].reshape(B, LP, D)[:, CHUNK:]
    return (y_prompt, y_sample, jnp.stack(dp), jnp.stack(ap), jnp.stack(bpl), jnp.stack(cpl),
            jnp.stack(ds), jnp.stack(as_), jnp.stack(bs), jnp.stack(cs))


def _gcd(a, b):
    while b:
        a, b = b, a % b
    return a
```

```python
import functools

import jax
import jax.numpy as jnp
from jax import lax
from jax.experimental import pallas as pl
from jax.experimental.pallas import tpu as pltpu

F32 = jnp.float32
BF16 = jnp.bfloat16
HIGHEST = lax.Precision.HIGHEST

EPS = 1e-6
LN_EPS = 1e-5
N_META = 16
H_A = 8
DK = 128
DV = 128
CHUNK = 64
LANES = 128
DEC_SEQ = 8
SUBSET = CHUNK // DEC_SEQ
HALO = 32
TOP_K = 2
NEG = -1e30

VMEM_LIMIT = 52 * 1024 * 1024


def _cp(dims, vmem=VMEM_LIMIT):
    return pltpu.CompilerParams(dimension_semantics=dims, vmem_limit_bytes=vmem)


def _sigmoid(x):
    return 0.5 * jnp.tanh(0.5 * x) + 0.5


def _silu(x):
    return x * _sigmoid(x)


def _pick_tile(n, target, mult):
    best = None
    for t in range(mult, min(n, target) + 1, mult):
        if n % t == 0:
            best = t
    assert best is not None, (n, target, mult)
    return best


def _inproj_kernel(x_ref, nw_ref, wa_ref, wb_ref, wab_ref, z_ref, ab_ref, h_scr, *, bounds, row_chunk):
    j = pl.program_id(1)
    tm = x_ref.shape[0]

    @pl.when(j == 0)
    def _():
        def body(r, c):
            rs = pl.ds(pl.multiple_of(r * row_chunk, row_chunk), row_chunk)
            x = x_ref[rs, :]
            h = x * lax.rsqrt(jnp.mean(x * x, axis=-1, keepdims=True) + EPS) * nw_ref[...]
            h_scr[rs, :] = h.astype(BF16)
            return c
        lax.fori_loop(0, tm // row_chunk, body, 0)
        ab_ref[...] = jnp.dot(h_scr[...], wab_ref[...], preferred_element_type=F32)

    b_silu, b_id2, b_mul, b_glu, b_sig = bounds

    def proj(w_ref):
        return jnp.dot(h_scr[...], w_ref[...], preferred_element_type=F32)

    @pl.when((j < b_silu) | ((j >= b_id2) & (j < b_mul)))
    def _():
        z_ref[...] = proj(wa_ref)

    @pl.when((j >= b_silu) & (j < b_id2))
    def _():
        z_ref[...] = _silu(proj(wa_ref))

    @pl.when((j >= b_mul) & (j < b_glu))
    def _():
        z_ref[...] = proj(wa_ref) * proj(wb_ref)

    @pl.when((j >= b_glu) & (j < b_sig))
    def _():
        z_ref[...] = proj(wa_ref) * _sigmoid(proj(wb_ref))

    @pl.when(j >= b_sig)
    def _():
        z_ref[...] = _sigmoid(proj(wa_ref))


def _inproj(x, norm_w, w_main, w_ab, *, w_qkv, w_v, w_b, w_c, tm, tn):
    T, D = x.shape
    nq, nv, nb, nc = w_qkv // tn, w_v // tn, w_b // tn, w_c // tn
    b_silu = nq
    b_id2 = nq + nv
    b_mul = b_id2 + nb
    b_glu = b_mul + nb
    b_sig = b_glu + nc
    n_wblocks = w_main.shape[1] // tn
    n_out = n_wblocks - nb - nc
    park_lo = b_mul + nb
    park_hi = b_glu + nb + 2 * nc - 1

    def wa_map(i, j):
        return (0, j + jnp.where(j >= b_glu, nb, 0) + jnp.where(j >= b_sig, nc, 0))

    def wb_map(i, j):
        jb = jnp.where(j < b_mul, park_lo,
                       jnp.where(j < b_glu, j + nb,
                                 jnp.where(j < b_sig, j + nb + nc, park_hi)))
        return (0, jb)

    kern = functools.partial(_inproj_kernel, bounds=(b_silu, b_id2, b_mul, b_glu, b_sig), row_chunk=16)
    return pl.pallas_call(
        kern,
        grid=(T // tm, n_out),
        in_specs=[
            pl.BlockSpec((tm, D), lambda i, j: (i, 0)),
            pl.BlockSpec((1, D), lambda i, j: (0, 0)),
            pl.BlockSpec((D, tn), wa_map),
            pl.BlockSpec((D, tn), wb_map),
            pl.BlockSpec((D, LANES), lambda i, j: (0, 0)),
        ],
        out_specs=[
            pl.BlockSpec((tm, tn), lambda i, j: (i, j)),
            pl.BlockSpec((tm, LANES), lambda i, j: (i, 0)),
        ],
        out_shape=[jax.ShapeDtypeStruct((T, n_out * tn), F32), jax.ShapeDtypeStruct((T, LANES), F32)],
        scratch_shapes=[pltpu.VMEM((tm, D), BF16)],
        compiler_params=_cp(("arbitrary", "arbitrary")),
        name="inproj",
    )(x, norm_w, w_main, w_main, w_ab)


def _conv_taps(xp_scr, base, rows, step, w_ref, width):
    acc = None
    for j in range(width):
        off = base - (width - 1 - j) * step
        term = xp_scr[pl.ds(off, rows), :] * w_ref[j:j + 1, :]
        acc = term if acc is None else acc + term
    return acc


def _conv_tile(i, n_s_tiles, x_ref, halo_ref, hist_ref, w_ref, xp_scr, y_scr, width):
    tr = x_ref.shape[0]
    hsub = (width - 1) * SUBSET
    nsub = tr // CHUNK

    @pl.when(i < n_s_tiles)
    def _():
        for s in range(nsub):
            xp_scr[0:hsub, :] = hist_ref[s * hsub:(s + 1) * hsub, :]
            xp_scr[hsub:hsub + CHUNK, :] = x_ref[s * CHUNK:(s + 1) * CHUNK, :]
            y_scr[s * CHUNK:(s + 1) * CHUNK, :] = _conv_taps(xp_scr, hsub, CHUNK, SUBSET, w_ref, width)

    @pl.when(i >= n_s_tiles)
    def _():
        halo = halo_ref[...]
        xp_scr[0:HALO, :] = jnp.where(i == n_s_tiles, jnp.zeros_like(halo), halo)
        xp_scr[HALO:HALO + tr, :] = x_ref[...]
        y_scr[...] = _conv_taps(xp_scr, HALO, tr, 1, w_ref, width)


def _softplus(x):
    return jnp.maximum(x, 0.0) + jnp.log(1.0 + jnp.exp(-jnp.abs(x)))


def _conva_kernel(x_ref, halo_ref, hist_ref, w_ref, ab_ref, mask_ref, adt_ref, o_ref, gb_ref, xp_scr, y_scr,
                  *, n_s_tiles, width):
    i = pl.program_id(0)
    sec = pl.program_id(1)
    _conv_tile(i, n_s_tiles, x_ref, halo_ref, hist_ref, w_ref, xp_scr, y_scr, width)

    @pl.when(sec == 0)
    def _():
        ab = ab_ref[...]
        lane = lax.broadcasted_iota(jnp.int32, ab.shape, 1)
        g = -jnp.exp(adt_ref[0:1, :]) * _softplus(ab + adt_ref[1:2, :])
        gb_ref[...] = jnp.where(lane < H_A, g, _sigmoid(ab)) * mask_ref[...]

    scale = jnp.where(sec == 0, DK ** -0.5, 1.0).astype(F32)
    for h in range(H_A):
        hs = slice(h * DK, (h + 1) * DK)
        y = _silu(y_scr[:, hs])
        yn = y * (lax.rsqrt(jnp.sum(y * y, axis=-1, keepdims=True) + EPS) * scale)
        o_ref[:, hs] = jnp.where(sec == 2, y, yn)


def _convb_kernel(x_ref, halo_ref, hist_ref, w_ref, gate_ref, o_ref, xp_scr, y_scr, *, n_s_tiles, width):
    i = pl.program_id(0)
    _conv_tile(i, n_s_tiles, x_ref, halo_ref, hist_ref, w_ref, xp_scr, y_scr, width)
    o_ref[...] = (gate_ref[...] * y_scr[...]).astype(o_ref.dtype)


def _convc_kernel(x_ref, halo_ref, hist_ref, w_ref, cb_ref, lnw_ref, lnb_ref, o_ref, xp_scr, y_scr,
                  *, n_s_tiles, width):
    i = pl.program_id(0)
    _conv_tile(i, n_s_tiles, x_ref, halo_ref, hist_ref, w_ref, xp_scr, y_scr, width)
    y = y_scr[...] + cb_ref[...]
    mu = jnp.mean(y, axis=-1, keepdims=True)
    yc = y - mu
    var = jnp.mean(yc * yc, axis=-1, keepdims=True)
    yn = yc * lax.rsqrt(var + LN_EPS) * lnw_ref[...] + lnb_ref[...]
    o_ref[...] = _silu(yn).astype(o_ref.dtype)


def _conv_specs(tr, cw, width, col_of, n_s_tiles, sec_axis):
    hrows = (tr // CHUNK) * (width - 1) * SUBSET
    hb = tr // HALO
    last_hist = max(n_s_tiles - 1, 0)
    if sec_axis:
        x_map = lambda i, s: (i, col_of(s))
        halo_map = lambda i, s: (jnp.maximum(i * hb - 1, 0), col_of(s))
        hist_map = lambda i, s: (jnp.minimum(i, last_hist), s)
    else:
        x_map = lambda i: (i, col_of(0))
        halo_map = lambda i: (jnp.maximum(i * hb - 1, 0), col_of(0))
        hist_map = lambda i: (jnp.minimum(i, last_hist), 0)
    return [pl.BlockSpec((tr, cw), x_map), pl.BlockSpec((HALO, cw), halo_map), pl.BlockSpec((hrows, cw), hist_map)]


def _dot(a, b):
    return jnp.dot(a.astype(BF16), b.astype(BF16), preferred_element_type=F32)


def _dot_nt(a, b):
    return lax.dot_general(a.astype(BF16), b.astype(BF16), (((1,), (1,)), ((), ())), preferred_element_type=F32)


def _split(x):
    hi = x.astype(BF16)
    return hi, (x - hi.astype(F32)).astype(BF16)


def _mask_dot(mask_bf, x):
    hi, lo = _split(x)
    return (jnp.dot(mask_bf, hi, preferred_element_type=F32)
            + jnp.dot(mask_bf, lo, preferred_element_type=F32))


def _mask_dot_nt(xt, mask_bf):
    hi, lo = _split(xt)
    dn = (((1,), (1,)), ((), ()))
    return (lax.dot_general(hi, mask_bf, dn, preferred_element_type=F32)
            + lax.dot_general(lo, mask_bf, dn, preferred_element_type=F32))


def _bf_mask(m):
    return jnp.where(m, 1.0, 0.0).astype(BF16)


def _chunk_common(q_ref, k_ref, v_ref, gb, gc_all, gr_all, lmask, strict):
    heads = range(H_A)
    hs = [slice(h * DK, (h + 1) * DK) for h in heads]
    q = [q_ref[:, hs[h]] for h in heads]
    k = [k_ref[:, hs[h]] for h in heads]
    v = [v_ref[:, hs[h]] for h in heads]
    gc = [gc_all[:, h:h + 1] for h in heads]
    beta = [gb[:, H_A + h:H_A + h + 1] for h in heads]
    decay = [jnp.exp(jnp.where(lmask, gc[h] - gr_all[h:h + 1, :], NEG)) for h in heads]
    kk = [_dot_nt(k[h], k[h]) for h in heads]
    qk = [_dot_nt(q[h], k[h]) * decay[h] for h in heads]
    m = [jnp.where(strict, beta[h] * kk[h] * decay[h], 0.0) for h in heads]
    eg = [jnp.exp(gc[h]) for h in heads]
    rhs = [jnp.concatenate([beta[h] * v[h], (beta[h] * eg[h]) * k[h]], axis=1) for h in heads]
    return q, k, gc, qk, m, eg, rhs


def _delta_kernel(q_ref, k_ref, v_ref, gb_ref, sga_ref, onw_ref, s0s_ref, ya_ref, ss_ref, sp_ref, s_scr,
                  *, n_s_steps, n_chunks):
    s = pl.program_id(0)
    C = CHUNK
    row = lax.broadcasted_iota(jnp.int32, (C, C), 0)
    col = lax.broadcasted_iota(jnp.int32, (C, C), 1)
    eye = (row == col).astype(F32)
    gb = gb_ref[...]
    gbt = gb.T[0:H_A, :]

    def finish(o, h):
        hs = slice(h * DV, (h + 1) * DV)
        on = o * lax.rsqrt(jnp.mean(o * o, axis=-1, keepdims=True) + EPS) * onw_ref[...]
        ya_ref[:, hs] = (on * sga_ref[:, hs]).astype(ya_ref.dtype)

    heads = range(H_A)

    @pl.when(s < n_s_steps)
    def _():
        same = ((row - col) & (DEC_SEQ - 1)) == 0
        lmask = same & (row >= col)
        strict = same & (row > col)
        lm = _bf_mask(lmask)
        last = _bf_mask(col == (C - DEC_SEQ) + (row & (DEC_SEQ - 1)))
        gc_all = _mask_dot(lm, gb)
        gr_all = _mask_dot_nt(gbt, lm)
        gl_all = _mask_dot(last, gc_all)
        rsub = lax.broadcasted_iota(jnp.int32, (C, 1), 0) & (DEC_SEQ - 1)
        rsub2 = jnp.concatenate([rsub, rsub], axis=0)
        q, k, gc, qk, m, eg, rhs = _chunk_common(q_ref, k_ref, v_ref, gb, gc_all, gr_all, lmask, strict)
        m2 = [_dot(m[h], m[h]) for h in heads]
        b1 = [eye - m[h] for h in heads]
        b2 = [b1[h] + _dot(b1[h], m2[h]) for h in heads]
        m4 = [_dot(m2[h], m2[h]) for h in heads]
        tinv = [b2[h] + _dot(b2[h], m4[h]) for h in heads]
        x = [_dot(tinv[h], rhs[h]) for h in heads]
        lhs = [jnp.concatenate([x[h][:, DV:], q[h]], axis=0) for h in heads]
        sk = []
        for h in heads:
            acc = jnp.zeros((2 * C, DV), F32)
            for i in range(SUBSET):
                acc = acc + jnp.where(rsub2 == i, _dot(lhs[h], s0s_ref[i, h]), 0.0)
            sk.append(acc)
        u = [x[h][:, :DV] - sk[h][:C] for h in heads]
        o = [eg[h] * sk[h][C:] + _dot(qk[h], u[h]) for h in heads]
        kd = [k[h] * jnp.exp(gl_all[:, h:h + 1] - gc[h]) for h in heads]
        for h in heads:
            for i in range(SUBSET):
                kdi = jnp.where(rsub == i, kd[h], 0.0)
                gli = gc_all[C - DEC_SEQ + i:C - DEC_SEQ + i + 1, h:h + 1]
                ss_ref[i, h] = jnp.exp(gli) * s0s_ref[i, h] + _dot(kdi.T, u[h])
        for h in heads:
            finish(o[h], h)

    @pl.when(s >= n_s_steps)
    def _():
        c = (s - n_s_steps) % n_chunks

        @pl.when(c == 0)
        def _():
            s_scr[...] = jnp.zeros_like(s_scr)

        lmask = row >= col
        strict = row > col
        lm = _bf_mask(lmask)
        blk = (row >> 4) == (col >> 4)
        gc_all = _mask_dot(lm, gb)
        gr_all = _mask_dot_nt(gbt, lm)
        q, k, gc, qk, m, eg, rhs = _chunk_common(q_ref, k_ref, v_ref, gb, gc_all, gr_all, lmask, strict)
        nd = [jnp.where(blk, -m[h], 0.0) for h in heads]
        lo = [jnp.where(blk, 0.0, m[h]) for h in heads]
        p2 = [_dot(nd[h], nd[h]) for h in heads]
        a1 = [eye + nd[h] for h in heads]
        t1 = [a1[h] + _dot(a1[h], p2[h]) for h in heads]
        p4 = [_dot(p2[h], p2[h]) for h in heads]
        t2 = [t1[h] + _dot(t1[h], p4[h]) for h in heads]
        p8 = [_dot(p4[h], p4[h]) for h in heads]
        dinv = [t2[h] + _dot(t2[h], p8[h]) for h in heads]
        y = [_dot(dinv[h], rhs[h]) for h in heads]
        a = [_dot(dinv[h], lo[h]) for h in heads]
        w = [y[h] - _dot(a[h], y[h]) for h in heads]
        a2 = [_dot(a[h], a[h]) for h in heads]
        x = [w[h] + _dot(a2[h], w[h]) for h in heads]
        S = [s_scr[h] for h in heads]
        sk = [_dot(jnp.concatenate([x[h][:, DV:], q[h]], axis=0), S[h]) for h in heads]
        u = [x[h][:, :DV] - sk[h][:C] for h in heads]
        o = [eg[h] * sk[h][C:] + _dot(qk[h], u[h]) for h in heads]
        gl = [gc_all[C - 1:C, h:h + 1] for h in heads]
        kd = [k[h] * jnp.exp(gl[h] - gc[h]) for h in heads]
        for h in heads:
            s_scr[h] = jnp.exp(gl[h]) * S[h] + _dot(kd[h].T, u[h])
        for h in heads:
            finish(o[h], h)

        @pl.when(c == n_chunks - 1)
        def _():
            sp_ref[0] = s_scr[...]


def _delta(qkv, gb, z, onorm_w, s0_sample, *, n_s_steps, n_batch, n_chunks, sga_col):
    T = qkv.shape[0]
    W = H_A * DK
    n_steps = n_s_steps + n_batch * n_chunks
    last_s = max(n_s_steps - 1, 0)
    kern = functools.partial(_delta_kernel, n_s_steps=n_s_steps, n_chunks=n_chunks)
    state_blk = (SUBSET, H_A, DK, DV)
    return pl.pallas_call(
        kern,
        grid=(n_steps,),
        in_specs=[
            pl.BlockSpec((CHUNK, W), lambda s: (s, 0)),
            pl.BlockSpec((CHUNK, W), lambda s: (s, 1)),
            pl.BlockSpec((CHUNK, W), lambda s: (s, 2)),
            pl.BlockSpec((CHUNK, LANES), lambda s: (s, 0)),
            pl.BlockSpec((CHUNK, W), lambda s: (s, sga_col)),
            pl.BlockSpec((1, DV), lambda s: (0, 0)),
            pl.BlockSpec(state_blk, lambda s: (jnp.minimum(s, last_s), 0, 0, 0)),
        ],
        out_specs=[
            pl.BlockSpec((CHUNK, W), lambda s: (s, 0)),
            pl.BlockSpec(state_blk, lambda s: (jnp.minimum(s, last_s), 0, 0, 0)),
            pl.BlockSpec((1, H_A, DK, DV), lambda s: (jnp.maximum(s - n_s_steps, 0) // n_chunks, 0, 0, 0)),
        ],
        out_shape=[
            jax.ShapeDtypeStruct((T, W), BF16),
            jax.ShapeDtypeStruct(s0_sample.shape, F32),
            jax.ShapeDtypeStruct((n_batch, H_A, DK, DV), F32),
        ],
        scratch_shapes=[pltpu.VMEM((H_A, DK, DV), F32)],
        compiler_params=_cp(("arbitrary",)),
        name="delta",
    )(qkv, qkv, qkv, gb, z, onorm_w, s0_sample)


def _merge_kernel(ya_ref, yb_ref, yc_ref, wa_ref, wb_ref, wc_ref, g0_ref, g1_ref, g2_ref, m_ref):
    m = g0_ref[...] * jnp.dot(ya_ref[...], wa_ref[...], preferred_element_type=F32)
    m = m + g1_ref[...] * jnp.dot(yb_ref[...], wb_ref[...], preferred_element_type=F32)
    m = m + g2_ref[...] * jnp.dot(yc_ref[...], wc_ref[...], preferred_element_type=F32)
    m_ref[...] = m.astype(m_ref.dtype)


def _merge(ya, yb, yc, wa, wb, wc, z, *, gs_col, tm, tn):
    T, K = ya.shape
    D = wa.shape[1]
    nb = D // tn
    g0 = gs_col // tn
    act = pl.BlockSpec((tm, K), lambda i, j: (i, 0))
    wsp = pl.BlockSpec((K, tn), lambda i, j: (0, j))
    return pl.pallas_call(
        _merge_kernel,
        grid=(T // tm, nb),
        in_specs=[act, act, act, wsp, wsp, wsp,
                  pl.BlockSpec((tm, tn), lambda i, j: (i, g0 + j)),
                  pl.BlockSpec((tm, tn), lambda i, j: (i, g0 + nb + j)),
                  pl.BlockSpec((tm, tn), lambda i, j: (i, g0 + 2 * nb + j))],
        out_specs=pl.BlockSpec((tm, tn), lambda i, j: (i, j)),
        out_shape=jax.ShapeDtypeStruct((T, D), BF16),
        compiler_params=_cp(("arbitrary", "arbitrary")),
        name="merge",
    )(ya, yb, yc, wa, wb, wc, z, z, z)


def _oproj_kernel(m_ref, w_ref, x_ref, o_ref):
    o_ref[...] = x_ref[...] + jnp.dot(m_ref[...], w_ref[...], preferred_element_type=F32)


def _oproj(m, w_o, x, *, tm, tn):
    T, D = x.shape
    return pl.pallas_call(
        _oproj_kernel,
        grid=(T // tm, D // tn),
        in_specs=[pl.BlockSpec((tm, D), lambda i, j: (i, 0)),
                  pl.BlockSpec((D, tn), lambda i, j: (0, j)),
                  pl.BlockSpec((tm, tn), lambda i, j: (i, j))],
        out_specs=pl.BlockSpec((tm, tn), lambda i, j: (i, j)),
        out_shape=jax.ShapeDtypeStruct((T, D), F32),
        compiler_params=_cp(("arbitrary", "arbitrary")),
        name="oproj",
    )(m, w_o, x)


def _dot_hi(a, b):
    return jnp.dot(a, b, preferred_element_type=F32, precision=HIGHEST)


def _router_kernel(x_ref, nw_ref, wr_ref, br_ref, ri_ref, rw_ref, cnt_ref, carry_scr, *, n_groups, per_group):
    i = pl.program_id(0)
    tr = x_ref.shape[0]

    @pl.when(i == 0)
    def _():
        carry_scr[...] = jnp.zeros_like(carry_scr)

    x = x_ref[...]
    h = x * lax.rsqrt(jnp.mean(x * x, axis=-1, keepdims=True) + EPS) * nw_ref[...]
    logits = _dot_hi(h, wr_ref[...]) + br_ref[...]
    lane = lax.broadcasted_iota(jnp.int32, logits.shape, 1)
    big = jnp.int32(1 << 20)

    def argmax_first(vals):
        m = jnp.max(vals, axis=-1, keepdims=True)
        idx = jnp.min(jnp.where(vals == m, lane, big), axis=-1, keepdims=True)
        return m, idx

    is_g = lane < n_groups
    mg, gsel = argmax_first(jnp.where(is_g, logits, NEG))
    pg = 1.0 / jnp.sum(jnp.where(is_g, jnp.exp(logits - mg), 0.0), axis=-1, keepdims=True)
    lo = n_groups + gsel * per_group
    in_group = (lane >= lo) & (lane < lo + per_group)
    le = jnp.where(in_group, logits, NEG)
    m1, i1 = argmax_first(le)
    m2, i2 = argmax_first(jnp.where(lane == i1, NEG, le))
    e21 = jnp.exp(m2 - m1)
    w1 = pg / (1.0 + e21)
    w2 = w1 * e21

    oh = ((lane == i1) | (lane == i2)).astype(F32)
    r_i = lax.broadcasted_iota(jnp.int32, (tr, tr), 0)
    c_i = lax.broadcasted_iota(jnp.int32, (tr, tr), 1)
    before = _dot((r_i > c_i).astype(F32), oh) + carry_scr[...]
    rank1 = jnp.sum(jnp.where(lane == i1, before, 0.0), axis=-1, keepdims=True)
    rank2 = jnp.sum(jnp.where(lane == i2, before, 0.0), axis=-1, keepdims=True)
    carry_scr[...] = carry_scr[...] + jnp.sum(oh, axis=0, keepdims=True)
    cnt_ref[...] = jnp.broadcast_to(carry_scr[...], cnt_ref.shape)

    ri = jnp.where(lane == 0, i1 - n_groups,
                   jnp.where(lane == 1, i2 - n_groups,
                             jnp.where(lane == 2, rank1.astype(jnp.int32),
                                       jnp.where(lane == 3, rank2.astype(jnp.int32), 0))))
    ri_ref[...] = ri
    rw_ref[...] = jnp.where(lane == 0, w1, jnp.where(lane == 1, w2, 0.0))


def _router(x, norm_w, w_r, b_r, *, n_groups, per_group, tr):
    T, D = x.shape
    kern = functools.partial(_router_kernel, n_groups=n_groups, per_group=per_group)
    return pl.pallas_call(
        kern,
        grid=(T // tr,),
        in_specs=[pl.BlockSpec((tr, D), lambda i: (i, 0)),
                  pl.BlockSpec((1, D), lambda i: (0, 0)),
                  pl.BlockSpec((D, LANES), lambda i: (0, 0)),
                  pl.BlockSpec((1, LANES), lambda i: (0, 0))],
        out_specs=[pl.BlockSpec((tr, LANES), lambda i: (i, 0)),
                   pl.BlockSpec((tr, LANES), lambda i: (i, 0)),
                   pl.BlockSpec((8, LANES), lambda i: (0, 0))],
        out_shape=[jax.ShapeDtypeStruct((T, LANES), jnp.int32),
                   jax.ShapeDtypeStruct((T, LANES), F32),
                   jax.ShapeDtypeStruct((8, LANES), F32)],
        scratch_shapes=[pltpu.VMEM((1, LANES), F32)],
        compiler_params=_cp(("arbitrary",)),
        name="router",
    )(x, norm_w, w_r, b_r)


def _row_copy(src, src_row, dst, dst_row, sem):
    return pltpu.make_async_copy(src.at[pl.ds(src_row, 1), :], dst.at[pl.ds(dst_row, 1), :], sem)


def _dispatch_kernel(p1_ref, p2_ref, x_ref, nw_ref, xs_in_ref, xs_ref, h_scr, sem):
    del xs_in_ref
    i = pl.program_id(0)
    tr = x_ref.shape[0]
    x = x_ref[...]
    h_scr[...] = x * lax.rsqrt(jnp.mean(x * x, axis=-1, keepdims=True) + EPS) * nw_ref[...]
    base = i * tr

    def issue(r, c):
        _row_copy(h_scr, r, xs_ref, p1_ref[base + r], sem.at[0]).start()
        _row_copy(h_scr, r, xs_ref, p2_ref[base + r], sem.at[1]).start()
        return c
    lax.fori_loop(0, tr, issue, 0)
    pltpu.make_async_copy(h_scr, xs_ref.at[pl.ds(0, tr), :], sem.at[0]).wait()
    pltpu.make_async_copy(h_scr, xs_ref.at[pl.ds(0, tr), :], sem.at[1]).wait()


def _dispatch(pos1, pos2, x, norm_w, xs_init, *, tr):
    T, D = x.shape
    grid_spec = pltpu.PrefetchScalarGridSpec(
        num_scalar_prefetch=2,
        grid=(T // tr,),
        in_specs=[pl.BlockSpec((tr, D), lambda i, p1, p2: (i, 0)),
                  pl.BlockSpec((1, D), lambda i, p1, p2: (0, 0)),
                  pl.BlockSpec(memory_space=pl.ANY)],
        out_specs=pl.BlockSpec(memory_space=pl.ANY),
        scratch_shapes=[pltpu.VMEM((tr, D), F32), pltpu.SemaphoreType.DMA((2,))],
    )
    return pl.pallas_call(
        _dispatch_kernel,
        grid_spec=grid_spec,
        out_shape=jax.ShapeDtypeStruct(xs_init.shape, F32),
        input_output_aliases={4: 0},
        compiler_params=_cp(("arbitrary",)),
        name="dispatch",
    )(pos1, pos2, x, norm_w, xs_init)


def _expert_kernel(te_ref, nu_ref, xs_ref, wg_ref, wu_ref, wd_ref, o_ref, wg_scr, wu_scr, wd_scr):
    i = pl.program_id(0)
    prev = te_ref[jnp.maximum(i - 1, 0)]

    @pl.when((i == 0) | (te_ref[i] != prev))
    def _():
        wg_scr[...] = wg_ref[...].astype(BF16)
        wu_scr[...] = wu_ref[...].astype(BF16)
        wd_scr[...] = wd_ref[...].astype(BF16)

    @pl.when(i < nu_ref[0])
    def _():
        h = xs_ref[...].astype(BF16)
        g = jnp.dot(h, wg_scr[...], preferred_element_type=F32)
        u = jnp.dot(h, wu_scr[...], preferred_element_type=F32)
        a = (_silu(g) * u).astype(BF16)
        o_ref[...] = jnp.dot(a, wd_scr[...], preferred_element_type=F32)

    @pl.when(i >= nu_ref[0])
    def _():
        o_ref[...] = jnp.zeros_like(o_ref)


def _experts(tile_e, n_used, xs, w_gate, w_up, w_down, *, tg):
    P, D = xs.shape
    E, _, F = w_gate.shape
    n_tiles = P // tg

    def xs_map(i, te, nu):
        return (jnp.minimum(i, nu[0] - 1), 0)

    grid_spec = pltpu.PrefetchScalarGridSpec(
        num_scalar_prefetch=2,
        grid=(n_tiles,),
        in_specs=[pl.BlockSpec((tg, D), xs_map),
                  pl.BlockSpec((None, D, F), lambda i, te, nu: (te[i], 0, 0)),
                  pl.BlockSpec((None, D, F), lambda i, te, nu: (te[i], 0, 0)),
                  pl.BlockSpec((None, F, D), lambda i, te, nu: (te[i], 0, 0))],
        out_specs=pl.BlockSpec((tg, D), lambda i, te, nu: (i, 0)),
        scratch_shapes=[pltpu.VMEM((D, F), BF16), pltpu.VMEM((D, F), BF16), pltpu.VMEM((F, D), BF16)],
    )
    return pl.pallas_call(
        _expert_kernel,
        grid_spec=grid_spec,
        out_shape=jax.ShapeDtypeStruct((P, D), F32),
        compiler_params=_cp(("arbitrary",)),
        name="experts",
    )(tile_e, n_used, xs, w_gate, w_up, w_down)


def _combine_kernel(p1_ref, p2_ref, x_ref, rw_ref, mask_ref, fw_ref, eo_ref, out_ref, b1_scr, b2_scr, sem, *, final):
    i = pl.program_id(0)
    tr = x_ref.shape[0]
    base = i * tr

    def issue(r, c):
        _row_copy(eo_ref, p1_ref[base + r], b1_scr, r, sem.at[0]).start()
        _row_copy(eo_ref, p2_ref[base + r], b2_scr, r, sem.at[1]).start()
        return c
    lax.fori_loop(0, tr, issue, 0)
    pltpu.make_async_copy(eo_ref.at[pl.ds(0, tr), :], b1_scr, sem.at[0]).wait()
    pltpu.make_async_copy(eo_ref.at[pl.ds(0, tr), :], b2_scr, sem.at[1]).wait()
    rw = rw_ref[...]
    x2 = (x_ref[...] + rw[:, 0:1] * b1_scr[...] + rw[:, 1:2] * b2_scr[...]) * mask_ref[:, 0:1]
    if final:
        out_ref[...] = x2 * lax.rsqrt(jnp.mean(x2 * x2, axis=-1, keepdims=True) + EPS) * fw_ref[...]
    else:
        out_ref[...] = x2


def _combine(pos1, pos2, x, rw, mask, final_w, eo, *, tr, final):
    T, D = x.shape
    row = lambda w: pl.BlockSpec((tr, w), lambda i, p1, p2: (i, 0))
    grid_spec = pltpu.PrefetchScalarGridSpec(
        num_scalar_prefetch=2,
        grid=(T // tr,),
        in_specs=[row(D), row(LANES), row(LANES),
                  pl.BlockSpec((1, D), lambda i, p1, p2: (0, 0)),
                  pl.BlockSpec(memory_space=pl.ANY)],
        out_specs=row(D),
        scratch_shapes=[pltpu.VMEM((tr, D), F32), pltpu.VMEM((tr, D), F32), pltpu.SemaphoreType.DMA((2,))],
    )
    return pl.pallas_call(
        functools.partial(_combine_kernel, final=final),
        grid_spec=grid_spec,
        out_shape=jax.ShapeDtypeStruct((T, D), F32),
        compiler_params=_cp(("arbitrary",)),
        name="combine",
    )(pos1, pos2, x, rw, mask, final_w, eo)


def _sample_to_rows(a):
    nb, L, C = a.shape
    return a.reshape(nb // SUBSET, SUBSET, L, C).transpose(0, 2, 1, 3).reshape(nb * L, C)


def _rows_to_sample(r, L):
    n, C = r.shape
    nb = n // L
    return r.reshape(nb // SUBSET, L, SUBSET, C).transpose(0, 2, 1, 3).reshape(nb, L, C)


def kernel(x_prompt, x_sample, state_delta, state_conv_a, state_conv_b, state_conv_c, meta_tokens, norm1_w, w_in, conv_a_w, a_log, dt_bias, onorm_a_w, w_out_a, conv_b_w, w_out_b, conv_c_w, conv_c_b, ln_c_w, ln_c_b, w_out_c, w_o, norm2_w, w_rg, b_rg, w_re, b_re, w_gate_e, w_up_e, w_down_e, final_norm_w):
    B, SEQ, D = x_prompt.shape
    NB, L, _ = x_sample.shape
    depth = w_in.shape[0]
    W_QKV = conv_a_w.shape[-1]
    W_V = H_A * DV
    W_B = conv_b_w.shape[-1]
    W_C = conv_c_w.shape[-1]
    CA, CB, CC = conv_a_w.shape[1], conv_b_w.shape[1], conv_c_w.shape[1]
    G = w_rg.shape[-1]
    E = w_re.shape[-1]
    assert L == DEC_SEQ and NB % SUBSET == 0 and SEQ % CHUNK == 0 and W_QKV == 3 * W_V
    assert W_V == W_B == W_C and max(CA, CB, CC) - 1 <= min(HALO, CHUNK - N_META)
    assert G + E <= LANES and D % LANES == 0

    LP = CHUNK + SEQ
    lead = CHUNK - N_META
    Ts, Tp = NB * L, B * LP
    T = Ts + Tp
    n_chunks = LP // CHUNK
    n_s_steps = Ts // CHUNK

    tr = _pick_tile(_gcd(Ts, Tp), 256, CHUNK)
    tm = _pick_tile(T, 1184, 16)
    tn = 512
    tg = 256
    n_s_tiles = Ts // tr

    dt_ = x_prompt.dtype
    lead_rows = jnp.concatenate([jnp.zeros((lead, D), dt_), meta_tokens.astype(dt_)], axis=0)
    xp = jnp.concatenate([jnp.broadcast_to(lead_rows[None], (B, CHUNK, D)), x_prompt], axis=1).reshape(Tp, D)
    x = jnp.concatenate([_sample_to_rows(x_sample), xp], axis=0)
    pos_in_seq = jnp.arange(Tp, dtype=jnp.int32) % LP
    real = jnp.concatenate([jnp.ones((Ts,), F32), (pos_in_seq >= lead).astype(F32)])
    mask = jnp.broadcast_to(real[:, None], (T, LANES))

    n_ab = 2 * H_A
    w_main = jnp.concatenate([w_in[:, :, :W_QKV], w_in[:, :, W_QKV + n_ab:]], axis=2).astype(BF16)
    w_ab = jnp.pad(w_in[:, :, W_QKV:W_QKV + n_ab], ((0, 0), (0, 0), (0, LANES - n_ab))).astype(BF16)
    w_r = jnp.pad(jnp.concatenate([w_rg, w_re], axis=2), ((0, 0), (0, 0), (0, LANES - G - E)))
    b_r = jnp.pad(jnp.concatenate([b_rg, b_re], axis=1), ((0, 0), (0, LANES - G - E)))[:, None, :]
    adt = jnp.pad(jnp.stack([a_log, dt_bias], axis=1), ((0, 0), (0, 0), (0, LANES - H_A)))
    adt = jnp.pad(adt, ((0, 0), (0, 6), (0, 0)))
    woa, wob, woc, wo = (w.astype(BF16) for w in (w_out_a, w_out_b, w_out_c, w_o))

    col_sga, col_gb, col_u, col_gl, col_gs = W_QKV, W_QKV + W_V, W_QKV + W_V + W_B, W_QKV + W_V + 2 * W_B, W_QKV + W_V + 2 * W_B + W_C
    cw = W_V
    P = TOP_K * T + E * tg
    n_tiles = P // tg

    dp, ap, bpl, cpl, ds, as_, bs, cs = [], [], [], [], [], [], [], []
    y = None
    for l in range(depth):
        z, ab = _inproj(x, norm1_w[l][None], w_main[l], w_ab[l],
                        w_qkv=W_QKV, w_v=W_V, w_b=W_B, w_c=W_C, tm=tm, tn=tn)

        def hist_rows(st):
            nb, hw, C = st.shape
            return st.reshape(nb // SUBSET, SUBSET, hw, C).transpose(0, 2, 1, 3).reshape(nb * hw, C)
        hist_a, hist_b, hist_c = hist_rows(state_conv_a[l]), hist_rows(state_conv_b[l]), hist_rows(state_conv_c[l])

        xs_, hs_, hi_ = _conv_specs(tr, cw, CA, lambda s: s, n_s_tiles, True)
        qkv, gb = pl.pallas_call(
            functools.partial(_conva_kernel, n_s_tiles=n_s_tiles, width=CA),
            grid=(T // tr, W_QKV // cw),
            in_specs=[xs_, hs_, hi_,
                      pl.BlockSpec((CA, cw), lambda i, s: (0, s)),
                      pl.BlockSpec((tr, LANES), lambda i, s: (i, 0)),
                      pl.BlockSpec((tr, LANES), lambda i, s: (i, 0)),
                      pl.BlockSpec((8, LANES), lambda i, s: (0, 0))],
            out_specs=[pl.BlockSpec((tr, cw), lambda i, s: (i, s)),
                       pl.BlockSpec((tr, LANES), lambda i, s: (i, 0))],
            out_shape=[jax.ShapeDtypeStruct((T, W_QKV), F32), jax.ShapeDtypeStruct((T, LANES), F32)],
            scratch_shapes=[pltpu.VMEM((HALO + tr, cw), F32), pltpu.VMEM((tr, cw), F32)],
            compiler_params=_cp(("arbitrary", "arbitrary")),
            name="conv_a",
        )(z, z, hist_a, conv_a_w[l], ab, mask, adt[l])

        ya, s_s, s_p = _delta(qkv, gb, z, onorm_a_w[l][None], state_delta[l],
                              n_s_steps=n_s_steps, n_batch=B, n_chunks=n_chunks, sga_col=col_sga // cw)

        xs_, hs_, hi_ = _conv_specs(tr, cw, CB, lambda s: col_u // cw, n_s_tiles, False)
        yb = pl.pallas_call(
            functools.partial(_convb_kernel, n_s_tiles=n_s_tiles, width=CB),
            grid=(T // tr,),
            in_specs=[xs_, hs_, hi_,
                      pl.BlockSpec((CB, cw), lambda i: (0, 0)),
                      pl.BlockSpec((tr, cw), lambda i: (i, col_gb // cw))],
            out_specs=pl.BlockSpec((tr, cw), lambda i: (i, 0)),
            out_shape=jax.ShapeDtypeStruct((T, cw), BF16),
            scratch_shapes=[pltpu.VMEM((HALO + tr, cw), F32), pltpu.VMEM((tr, cw), F32)],
            compiler_params=_cp(("arbitrary",)),
            name="conv_b",
        )(z, z, hist_b, conv_b_w[l], z)

        xs_, hs_, hi_ = _conv_specs(tr, cw, CC, lambda s: col_gl // cw, n_s_tiles, False)
        hc_rows = max(HALO + tr, (CC - 1) * SUBSET + CHUNK)
        yc = pl.pallas_call(
            functools.partial(_convc_kernel, n_s_tiles=n_s_tiles, width=CC),
            grid=(T // tr,),
            in_specs=[xs_, hs_, hi_,
                      pl.BlockSpec((CC, cw), lambda i: (0, 0)),
                      pl.BlockSpec((1, cw), lambda i: (0, 0)),
                      pl.BlockSpec((1, cw), lambda i: (0, 0)),
                      pl.BlockSpec((1, cw), lambda i: (0, 0))],
            out_specs=pl.BlockSpec((tr, cw), lambda i: (i, 0)),
            out_shape=jax.ShapeDtypeStruct((T, cw), BF16),
            scratch_shapes=[pltpu.VMEM((hc_rows, cw), F32), pltpu.VMEM((tr, cw), F32)],
            compiler_params=_cp(("arbitrary",)),
            name="conv_c",
        )(z, z, hist_c, conv_c_w[l], conv_c_b[l][None], ln_c_w[l][None], ln_c_b[l][None])

        m = _merge(ya, yb, yc, woa[l], wob[l], woc[l], z, gs_col=col_gs, tm=tm, tn=tn)
        x1 = _oproj(m, wo[l], x, tm=tm, tn=tn)

        ri, rw, cnt = _router(x1, norm2_w[l][None], w_r[l], b_r[l], n_groups=G, per_group=E // G, tr=tr)
        counts = cnt[0, G:G + E].astype(jnp.int32)
        padded = ((counts + tg - 1) // tg) * tg
        ends = jnp.cumsum(padded)
        offs = ends - padded
        pos1 = offs[ri[:, 0]] + ri[:, 2]
        pos2 = offs[ri[:, 1]] + ri[:, 3]
        n_used = (ends[-1] // tg).astype(jnp.int32)
        tile_start = jnp.arange(n_tiles, dtype=jnp.int32) * tg
        tile_e = jnp.sum((tile_start[:, None] >= ends[None, :]).astype(jnp.int32), axis=1)
        last_e = jnp.sum((((n_used - 1) * tg) >= ends).astype(jnp.int32))
        tile_e = jnp.where(jnp.arange(n_tiles) < n_used, tile_e, last_e).astype(jnp.int32)

        xs = _dispatch(pos1, pos2, x1, norm2_w[l][None], jnp.zeros((P, D), F32), tr=tr)
        eo = _experts(tile_e, n_used.reshape(1), xs, w_gate_e[l], w_up_e[l], w_down_e[l], tg=tg)
        x = _combine(pos1, pos2, x1, rw, mask, final_norm_w[None], eo, tr=tr, final=(l == depth - 1))

        def prompt_tail(c0, c1, width):
            return jnp.stack([lax.slice(z, (Ts + (b + 1) * LP - (width - 1), c0), (Ts + (b + 1) * LP, c1))
                              for b in range(B)])

        def sample_hist(st, c0, c1):
            seq = _rows_to_sample(lax.slice(z, (0, c0), (Ts, c1)), L)
            return jnp.concatenate([st.astype(dt_), seq], axis=1)[:, L:]
        dp.append(s_p)
        ap.append(prompt_tail(0, W_QKV, CA))
        bpl.append(prompt_tail(col_u, col_u + W_B, CB))
        cpl.append(prompt_tail(col_gl, col_gl + W_C, CC))
        ds.append(s_s)
        as_.append(sample_hist(state_conv_a[l], 0, W_QKV))
        bs.append(sample_hist(state_conv_b[l], col_u, col_u + W_B))
        cs.append(sample_hist(state_conv_c[l], col_gl, col_gl + W_C))

    y = x
    y_sample = _rows_to_sample(y[:Ts], L)
    y_prompt = y[Ts:].reshape(B, LP, D)[:, CHUNK:]
    return (y_prompt, y_sample, jnp.stack(dp), jnp.stack(ap), jnp.stack(bpl), jnp.stack(cpl),
            jnp.stack(ds), jnp.stack(as_), jnp.stack(bs), jnp.stack(cs))


def _gcd(a, b):
    while b:
        a, b = b, a % b
    return a
```

```python
import functools

import jax
import jax.numpy as jnp
from jax import lax
from jax.experimental import pallas as pl
from jax.experimental.pallas import tpu as pltpu

F32 = jnp.float32
BF16 = jnp.bfloat16
HIGHEST = lax.Precision.HIGHEST

EPS = 1e-6
LN_EPS = 1e-5
N_META = 16
H_A = 8
DK = 128
DV = 128
CHUNK = 64
LANES = 128
DEC_SEQ = 8
SUBSET = CHUNK // DEC_SEQ
HALO = 32
TOP_K = 2
NEG = -1e30

VMEM_LIMIT = 52 * 1024 * 1024


def _cp(dims, vmem=VMEM_LIMIT):
    return pltpu.CompilerParams(dimension_semantics=dims, vmem_limit_bytes=vmem)


def _sigmoid(x):
    return 0.5 * jnp.tanh(0.5 * x) + 0.5


def _silu(x):
    return x * _sigmoid(x)


def _pick_tile(n, target, mult):
    best = None
    for t in range(mult, min(n, target) + 1, mult):
        if n % t == 0:
            best = t
    assert best is not None, (n, target, mult)
    return best


def _inproj_kernel(x_ref, nw_ref, wq_ref, wa_ref, wb_ref, wab_ref, z_ref, ab_ref, h_scr, *, bounds, row_chunk):
    j = pl.program_id(1)
    tm = x_ref.shape[0]

    @pl.when(j == 0)
    def _():
        def body(r, c):
            rs = pl.ds(pl.multiple_of(r * row_chunk, row_chunk), row_chunk)
            x = x_ref[rs, :]
            h = x * lax.rsqrt(jnp.mean(x * x, axis=-1, keepdims=True) + EPS) * nw_ref[...]
            h_scr[rs, :] = h.astype(BF16)
            return c
        lax.fori_loop(0, tm // row_chunk, body, 0)
        ab_ref[...] = jnp.dot(h_scr[...], wab_ref[...], preferred_element_type=F32)

    b_silu, b_id2, b_mul, b_glu, b_sig = bounds

    def proj(w_ref):
        return jnp.dot(h_scr[...], w_ref[...], preferred_element_type=F32)

    @pl.when(j < b_silu)
    def _():
        z_ref[...] = proj(wq_ref)

    @pl.when((j >= b_id2) & (j < b_mul))
    def _():
        z_ref[...] = proj(wa_ref)

    @pl.when((j >= b_silu) & (j < b_id2))
    def _():
        z_ref[...] = _silu(proj(wa_ref))

    @pl.when((j >= b_mul) & (j < b_glu))
    def _():
        z_ref[...] = proj(wa_ref) * proj(wb_ref)

    @pl.when((j >= b_glu) & (j < b_sig))
    def _():
        z_ref[...] = proj(wa_ref) * _sigmoid(proj(wb_ref))

    @pl.when(j >= b_sig)
    def _():
        z_ref[...] = _sigmoid(proj(wa_ref))


def _inproj(x, norm_w, w_q, w_rest, w_ab, l, *, w_v, w_b, w_c, tm, tn):
    T, D = x.shape
    nq, nv, nb, nc = w_q.shape[2] // tn, w_v // tn, w_b // tn, w_c // tn
    b_silu = nq
    b_id2 = nq + nv
    b_mul = b_id2 + nb
    b_glu = b_mul + nb
    b_sig = b_glu + nc
    n_out = nq + w_rest.shape[2] // tn - nb - nc
    park_lo = b_glu - nq
    park_hi = b_sig - nq + nb + nc - 1

    def wq_map(i, j):
        return (l, 0, jnp.minimum(j, nq - 1))

    def wa_map(i, j):
        return (l, 0, jnp.maximum(j - nq, 0) + jnp.where(j >= b_glu, nb, 0) + jnp.where(j >= b_sig, nc, 0))

    def wb_map(i, j):
        jb = jnp.where(j < b_mul, park_lo,
                       jnp.where(j < b_glu, j - nq + nb,
                                 jnp.where(j < b_sig, j - nq + nb + nc, park_hi)))
        return (l, 0, jb)

    kern = functools.partial(_inproj_kernel, bounds=(b_silu, b_id2, b_mul, b_glu, b_sig), row_chunk=16)
    return pl.pallas_call(
        kern,
        grid=(T // tm, n_out),
        in_specs=[
            pl.BlockSpec((tm, D), lambda i, j: (i, 0)),
            pl.BlockSpec((1, D), lambda i, j: (0, 0)),
            pl.BlockSpec((None, D, tn), wq_map),
            pl.BlockSpec((None, D, tn), wa_map),
            pl.BlockSpec((None, D, tn), wb_map),
            pl.BlockSpec((None, D, LANES), lambda i, j: (l, 0, 0)),
        ],
        out_specs=[
            pl.BlockSpec((tm, tn), lambda i, j: (i, j)),
            pl.BlockSpec((tm, LANES), lambda i, j: (i, 0)),
        ],
        out_shape=[jax.ShapeDtypeStruct((T, n_out * tn), F32), jax.ShapeDtypeStruct((T, LANES), F32)],
        scratch_shapes=[pltpu.VMEM((tm, D), BF16)],
        compiler_params=_cp(("arbitrary", "arbitrary")),
        name="inproj",
    )(x, norm_w, w_q, w_rest, w_rest, w_ab)


def _conv_taps(xp_scr, base, rows, step, w_ref, width):
    acc = None
    for j in range(width):
        off = base - (width - 1 - j) * step
        term = xp_scr[pl.ds(off, rows), :] * w_ref[j:j + 1, :]
        acc = term if acc is None else acc + term
    return acc


def _conv_tile(i, n_s_tiles, x_ref, halo_ref, hist_ref, w_ref, xp_scr, y_scr, width):
    tr = x_ref.shape[0]
    hsub = (width - 1) * SUBSET
    nsub = tr // CHUNK

    @pl.when(i < n_s_tiles)
    def _():
        for s in range(nsub):
            xp_scr[0:hsub, :] = hist_ref[s * hsub:(s + 1) * hsub, :]
            xp_scr[hsub:hsub + CHUNK, :] = x_ref[s * CHUNK:(s + 1) * CHUNK, :]
            y_scr[s * CHUNK:(s + 1) * CHUNK, :] = _conv_taps(xp_scr, hsub, CHUNK, SUBSET, w_ref, width)

    @pl.when(i >= n_s_tiles)
    def _():
        halo = halo_ref[...]
        xp_scr[0:HALO, :] = jnp.where(i == n_s_tiles, jnp.zeros_like(halo), halo)
        xp_scr[HALO:HALO + tr, :] = x_ref[...]
        y_scr[...] = _conv_taps(xp_scr, HALO, tr, 1, w_ref, width)


def _softplus(x):
    return jnp.maximum(x, 0.0) + jnp.log(1.0 + jnp.exp(-jnp.abs(x)))


def _conva_kernel(x_ref, halo_ref, hist_ref, w_ref, ab_ref, mask_ref, adt_ref, o_ref, gb_ref, xp_scr, y_scr,
                  *, n_s_tiles, width):
    i = pl.program_id(0)
    sec = pl.program_id(1)
    _conv_tile(i, n_s_tiles, x_ref, halo_ref, hist_ref, w_ref, xp_scr, y_scr, width)

    @pl.when(sec == 0)
    def _():
        ab = ab_ref[...]
        lane = lax.broadcasted_iota(jnp.int32, ab.shape, 1)
        g = -jnp.exp(adt_ref[0:1, :]) * _softplus(ab + adt_ref[1:2, :])
        gb_ref[...] = jnp.where(lane < H_A, g, _sigmoid(ab)) * mask_ref[...]

    scale = jnp.where(sec == 0, DK ** -0.5, 1.0).astype(F32)
    for h in range(H_A):
        hs = slice(h * DK, (h + 1) * DK)
        y = _silu(y_scr[:, hs])
        yn = y * (lax.rsqrt(jnp.sum(y * y, axis=-1, keepdims=True) + EPS) * scale)
        o_ref[:, hs] = jnp.where(sec == 2, y, yn)


def _convb_kernel(x_ref, halo_ref, hist_ref, w_ref, gate_ref, o_ref, xp_scr, y_scr, *, n_s_tiles, width):
    i = pl.program_id(0)
    _conv_tile(i, n_s_tiles, x_ref, halo_ref, hist_ref, w_ref, xp_scr, y_scr, width)
    o_ref[...] = (gate_ref[...] * y_scr[...]).astype(o_ref.dtype)


def _convc_kernel(x_ref, halo_ref, hist_ref, w_ref, cb_ref, lnw_ref, lnb_ref, o_ref, xp_scr, y_scr,
                  *, n_s_tiles, width):
    i = pl.program_id(0)
    _conv_tile(i, n_s_tiles, x_ref, halo_ref, hist_ref, w_ref, xp_scr, y_scr, width)
    y = y_scr[...] + cb_ref[...]
    mu = jnp.mean(y, axis=-1, keepdims=True)
    yc = y - mu
    var = jnp.mean(yc * yc, axis=-1, keepdims=True)
    yn = yc * lax.rsqrt(var + LN_EPS) * lnw_ref[...] + lnb_ref[...]
    o_ref[...] = _silu(yn).astype(o_ref.dtype)


def _conv_specs(tr, cw, width, col_of, n_s_tiles, sec_axis):
    hrows = (tr // CHUNK) * (width - 1) * SUBSET
    hb = tr // HALO
    last_hist = max(n_s_tiles - 1, 0)
    if sec_axis:
        x_map = lambda i, s: (i, col_of(s))
        halo_map = lambda i, s: (jnp.maximum(i * hb - 1, 0), col_of(s))
        hist_map = lambda i, s: (jnp.minimum(i, last_hist), s)
    else:
        x_map = lambda i: (i, col_of(0))
        halo_map = lambda i: (jnp.maximum(i * hb - 1, 0), col_of(0))
        hist_map = lambda i: (jnp.minimum(i, last_hist), 0)
    return [pl.BlockSpec((tr, cw), x_map), pl.BlockSpec((HALO, cw), halo_map), pl.BlockSpec((hrows, cw), hist_map)]


def _dot(a, b):
    return jnp.dot(a.astype(BF16), b.astype(BF16), preferred_element_type=F32)


def _dot_nt(a, b):
    return lax.dot_general(a.astype(BF16), b.astype(BF16), (((1,), (1,)), ((), ())), preferred_element_type=F32)


def _split(x):
    hi = x.astype(BF16)
    return hi, (x - hi.astype(F32)).astype(BF16)


def _mask_dot(mask_bf, x):
    hi, lo = _split(x)
    return (jnp.dot(mask_bf, hi, preferred_element_type=F32)
            + jnp.dot(mask_bf, lo, preferred_element_type=F32))


def _mask_dot_nt(xt, mask_bf):
    hi, lo = _split(xt)
    dn = (((1,), (1,)), ((), ()))
    return (lax.dot_general(hi, mask_bf, dn, preferred_element_type=F32)
            + lax.dot_general(lo, mask_bf, dn, preferred_element_type=F32))


def _bf_mask(m):
    return jnp.where(m, 1.0, 0.0).astype(BF16)


def _chunk_common(q_ref, k_ref, v_ref, gb, gc_all, gr_all, lmask, strict):
    heads = range(H_A)
    hs = [slice(h * DK, (h + 1) * DK) for h in heads]
    q = [q_ref[:, hs[h]] for h in heads]
    k = [k_ref[:, hs[h]] for h in heads]
    v = [v_ref[:, hs[h]] for h in heads]
    gc = [gc_all[:, h:h + 1] for h in heads]
    beta = [gb[:, H_A + h:H_A + h + 1] for h in heads]
    decay = [jnp.exp(jnp.where(lmask, gc[h] - gr_all[h:h + 1, :], NEG)) for h in heads]
    kk = [_dot_nt(k[h], k[h]) for h in heads]
    qk = [_dot_nt(q[h], k[h]) * decay[h] for h in heads]
    m = [jnp.where(strict, beta[h] * kk[h] * decay[h], 0.0) for h in heads]
    eg = [jnp.exp(gc[h]) for h in heads]
    rhs = [jnp.concatenate([beta[h] * v[h], (beta[h] * eg[h]) * k[h]], axis=1) for h in heads]
    return q, k, gc, qk, m, eg, rhs


def _delta_kernel(q_ref, k_ref, v_ref, gb_ref, sga_ref, onw_ref, s0s_ref, ya_ref, ss_ref, sp_ref, s_scr,
                  *, n_s_steps, n_chunks):
    s = pl.program_id(0)
    C = CHUNK
    row = lax.broadcasted_iota(jnp.int32, (C, C), 0)
    col = lax.broadcasted_iota(jnp.int32, (C, C), 1)
    eye = (row == col).astype(F32)
    gb = gb_ref[...]
    gbt = gb.T[0:H_A, :]

    def finish(o, h):
        hs = slice(h * DV, (h + 1) * DV)
        on = o * lax.rsqrt(jnp.mean(o * o, axis=-1, keepdims=True) + EPS) * onw_ref[...]
        ya_ref[:, hs] = (on * sga_ref[:, hs]).astype(ya_ref.dtype)

    heads = range(H_A)

    @pl.when(s < n_s_steps)
    def _():
        same = ((row - col) & (DEC_SEQ - 1)) == 0
        lmask = same & (row >= col)
        strict = same & (row > col)
        lm = _bf_mask(lmask)
        last = _bf_mask(col == (C - DEC_SEQ) + (row & (DEC_SEQ - 1)))
        gc_all = _mask_dot(lm, gb)
        gr_all = _mask_dot_nt(gbt, lm)
        gl_all = _mask_dot(last, gc_all)
        rsub = lax.broadcasted_iota(jnp.int32, (C, 1), 0) & (DEC_SEQ - 1)
        rsub2 = jnp.concatenate([rsub, rsub], axis=0)
        q, k, gc, qk, m, eg, rhs = _chunk_common(q_ref, k_ref, v_ref, gb, gc_all, gr_all, lmask, strict)
        m2 = [_dot(m[h], m[h]) for h in heads]
        b1 = [eye - m[h] for h in heads]
        b2 = [b1[h] + _dot(b1[h], m2[h]) for h in heads]
        m4 = [_dot(m2[h], m2[h]) for h in heads]
        tinv = [b2[h] + _dot(b2[h], m4[h]) for h in heads]
        x = [_dot(tinv[h], rhs[h]) for h in heads]
        lhs = [jnp.concatenate([x[h][:, DV:], q[h]], axis=0) for h in heads]
        sk = []
        for h in heads:
            acc = jnp.zeros((2 * C, DV), F32)
            for i in range(SUBSET):
                acc = acc + jnp.where(rsub2 == i, _dot(lhs[h], s0s_ref[i, h]), 0.0)
            sk.append(acc)
        u = [x[h][:, :DV] - sk[h][:C] for h in heads]
        o = [eg[h] * sk[h][C:] + _dot(qk[h], u[h]) for h in heads]
        kd = [k[h] * jnp.exp(gl_all[:, h:h + 1] - gc[h]) for h in heads]
        for h in heads:
            for i in range(SUBSET):
                kdi = jnp.where(rsub == i, kd[h], 0.0)
                gli = gc_all[C - DEC_SEQ + i:C - DEC_SEQ + i + 1, h:h + 1]
                ss_ref[i, h] = jnp.exp(gli) * s0s_ref[i, h] + _dot(kdi.T, u[h])
        for h in heads:
            finish(o[h], h)

    @pl.when(s >= n_s_steps)
    def _():
        c = (s - n_s_steps) % n_chunks

        @pl.when(c == 0)
        def _():
            s_scr[...] = jnp.zeros_like(s_scr)

        lmask = row >= col
        strict = row > col
        lm = _bf_mask(lmask)
        blk = (row >> 4) == (col >> 4)
        gc_all = _mask_dot(lm, gb)
        gr_all = _mask_dot_nt(gbt, lm)
        q, k, gc, qk, m, eg, rhs = _chunk_common(q_ref, k_ref, v_ref, gb, gc_all, gr_all, lmask, strict)
        nd = [jnp.where(blk, -m[h], 0.0) for h in heads]
        lo = [jnp.where(blk, 0.0, m[h]) for h in heads]
        p2 = [_dot(nd[h], nd[h]) for h in heads]
        a1 = [eye + nd[h] for h in heads]
        t1 = [a1[h] + _dot(a1[h], p2[h]) for h in heads]
        p4 = [_dot(p2[h], p2[h]) for h in heads]
        t2 = [t1[h] + _dot(t1[h], p4[h]) for h in heads]
        p8 = [_dot(p4[h], p4[h]) for h in heads]
        dinv = [t2[h] + _dot(t2[h], p8[h]) for h in heads]
        y = [_dot(dinv[h], rhs[h]) for h in heads]
        a = [_dot(dinv[h], lo[h]) for h in heads]
        w = [y[h] - _dot(a[h], y[h]) for h in heads]
        a2 = [_dot(a[h], a[h]) for h in heads]
        x = [w[h] + _dot(a2[h], w[h]) for h in heads]
        S = [s_scr[h] for h in heads]
        sk = [_dot(jnp.concatenate([x[h][:, DV:], q[h]], axis=0), S[h]) for h in heads]
        u = [x[h][:, :DV] - sk[h][:C] for h in heads]
        o = [eg[h] * sk[h][C:] + _dot(qk[h], u[h]) for h in heads]
        gl = [gc_all[C - 1:C, h:h + 1] for h in heads]
        kd = [k[h] * jnp.exp(gl[h] - gc[h]) for h in heads]
        for h in heads:
            s_scr[h] = jnp.exp(gl[h]) * S[h] + _dot(kd[h].T, u[h])
        for h in heads:
            finish(o[h], h)

        @pl.when(c == n_chunks - 1)
        def _():
            sp_ref[0] = s_scr[...]


def _delta(qkv, gb, z, onorm_w, state_all, l, *, n_s_steps, n_batch, n_chunks, sga_col):
    T = qkv.shape[0]
    W = H_A * DK
    n_steps = n_s_steps + n_batch * n_chunks
    last_s = max(n_s_steps - 1, 0)
    kern = functools.partial(_delta_kernel, n_s_steps=n_s_steps, n_chunks=n_chunks)
    return pl.pallas_call(
        kern,
        grid=(n_steps,),
        in_specs=[
            pl.BlockSpec((CHUNK, W), lambda s: (s, 0)),
            pl.BlockSpec((CHUNK, W), lambda s: (s, 1)),
            pl.BlockSpec((CHUNK, W), lambda s: (s, 2)),
            pl.BlockSpec((CHUNK, LANES), lambda s: (s, 0)),
            pl.BlockSpec((CHUNK, W), lambda s: (s, sga_col)),
            pl.BlockSpec((1, DV), lambda s: (0, 0)),
            pl.BlockSpec((None, SUBSET, H_A, DK, DV), lambda s: (l, jnp.minimum(s, last_s), 0, 0, 0)),
        ],
        out_specs=[
            pl.BlockSpec((CHUNK, W), lambda s: (s, 0)),
            pl.BlockSpec((SUBSET, H_A, DK, DV), lambda s: (jnp.minimum(s, last_s), 0, 0, 0)),
            pl.BlockSpec((1, H_A, DK, DV), lambda s: (jnp.maximum(s - n_s_steps, 0) // n_chunks, 0, 0, 0)),
        ],
        out_shape=[
            jax.ShapeDtypeStruct((T, W), BF16),
            jax.ShapeDtypeStruct(state_all.shape[1:], F32),
            jax.ShapeDtypeStruct((n_batch, H_A, DK, DV), F32),
        ],
        scratch_shapes=[pltpu.VMEM((H_A, DK, DV), F32)],
        compiler_params=_cp(("arbitrary",)),
        name="delta",
    )(qkv, qkv, qkv, gb, z, onorm_w, state_all)


def _merge_kernel(ya_ref, yb_ref, yc_ref, wa_ref, wb_ref, wc_ref, g0_ref, g1_ref, g2_ref, m_ref):
    m = g0_ref[...] * jnp.dot(ya_ref[...], wa_ref[...], preferred_element_type=F32)
    m = m + g1_ref[...] * jnp.dot(yb_ref[...], wb_ref[...], preferred_element_type=F32)
    m = m + g2_ref[...] * jnp.dot(yc_ref[...], wc_ref[...], preferred_element_type=F32)
    m_ref[...] = m.astype(m_ref.dtype)


def _merge(ya, yb, yc, wa, wb, wc, z, *, gs_col, tm, tn):
    T, K = ya.shape
    D = wa.shape[1]
    nb = D // tn
    g0 = gs_col // tn
    act = pl.BlockSpec((tm, K), lambda i, j: (i, 0))
    wsp = pl.BlockSpec((K, tn), lambda i, j: (0, j))
    return pl.pallas_call(
        _merge_kernel,
        grid=(T // tm, nb),
        in_specs=[act, act, act, wsp, wsp, wsp,
                  pl.BlockSpec((tm, tn), lambda i, j: (i, g0 + j)),
                  pl.BlockSpec((tm, tn), lambda i, j: (i, g0 + nb + j)),
                  pl.BlockSpec((tm, tn), lambda i, j: (i, g0 + 2 * nb + j))],
        out_specs=pl.BlockSpec((tm, tn), lambda i, j: (i, j)),
        out_shape=jax.ShapeDtypeStruct((T, D), BF16),
        compiler_params=_cp(("arbitrary", "arbitrary")),
        name="merge",
    )(ya, yb, yc, wa, wb, wc, z, z, z)


def _oproj_kernel(m_ref, w_ref, x_ref, o_ref):
    o_ref[...] = x_ref[...] + jnp.dot(m_ref[...], w_ref[...], preferred_element_type=F32)


def _oproj(m, w_o, x, *, tm, tn):
    T, D = x.shape
    return pl.pallas_call(
        _oproj_kernel,
        grid=(T // tm, D // tn),
        in_specs=[pl.BlockSpec((tm, D), lambda i, j: (i, 0)),
                  pl.BlockSpec((D, tn), lambda i, j: (0, j)),
                  pl.BlockSpec((tm, tn), lambda i, j: (i, j))],
        out_specs=pl.BlockSpec((tm, tn), lambda i, j: (i, j)),
        out_shape=jax.ShapeDtypeStruct((T, D), F32),
        compiler_params=_cp(("arbitrary", "arbitrary")),
        name="oproj",
    )(m, w_o, x)


def _dot_hi(a, b):
    return jnp.dot(a, b, preferred_element_type=F32, precision=HIGHEST)


def _router_kernel(x_ref, nw_ref, wr_ref, br_ref, ri_ref, rw_ref, cnt_ref, carry_scr, *, n_groups, per_group):
    i = pl.program_id(0)
    tr = x_ref.shape[0]

    @pl.when(i == 0)
    def _():
        carry_scr[...] = jnp.zeros_like(carry_scr)

    x = x_ref[...]
    h = x * lax.rsqrt(jnp.mean(x * x, axis=-1, keepdims=True) + EPS) * nw_ref[...]
    logits = _dot_hi(h, wr_ref[...]) + br_ref[...]
    lane = lax.broadcasted_iota(jnp.int32, logits.shape, 1)
    big = jnp.int32(1 << 20)

    def argmax_first(vals):
        m = jnp.max(vals, axis=-1, keepdims=True)
        idx = jnp.min(jnp.where(vals == m, lane, big), axis=-1, keepdims=True)
        return m, idx

    is_g = lane < n_groups
    mg, gsel = argmax_first(jnp.where(is_g, logits, NEG))
    pg = 1.0 / jnp.sum(jnp.where(is_g, jnp.exp(logits - mg), 0.0), axis=-1, keepdims=True)
    lo = n_groups + gsel * per_group
    in_group = (lane >= lo) & (lane < lo + per_group)
    le = jnp.where(in_group, logits, NEG)
    m1, i1 = argmax_first(le)
    m2, i2 = argmax_first(jnp.where(lane == i1, NEG, le))
    e21 = jnp.exp(m2 - m1)
    w1 = pg / (1.0 + e21)
    w2 = w1 * e21

    oh = ((lane == i1) | (lane == i2)).astype(F32)
    r_i = lax.broadcasted_iota(jnp.int32, (tr, tr), 0)
    c_i = lax.broadcasted_iota(jnp.int32, (tr, tr), 1)
    before = _dot((r_i > c_i).astype(F32), oh) + carry_scr[...]
    rank1 = jnp.sum(jnp.where(lane == i1, before, 0.0), axis=-1, keepdims=True)
    rank2 = jnp.sum(jnp.where(lane == i2, before, 0.0), axis=-1, keepdims=True)
    carry_scr[...] = carry_scr[...] + jnp.sum(oh, axis=0, keepdims=True)
    cnt_ref[...] = jnp.broadcast_to(carry_scr[...], cnt_ref.shape)

    ri = jnp.where(lane == 0, i1 - n_groups,
                   jnp.where(lane == 1, i2 - n_groups,
                             jnp.where(lane == 2, rank1.astype(jnp.int32),
                                       jnp.where(lane == 3, rank2.astype(jnp.int32), 0))))
    ri_ref[...] = ri
    rw_ref[...] = jnp.where(lane == 0, w1, jnp.where(lane == 1, w2, 0.0))


def _router(x, norm_w, w_r, b_r, *, n_groups, per_group, tr):
    T, D = x.shape
    kern = functools.partial(_router_kernel, n_groups=n_groups, per_group=per_group)
    return pl.pallas_call(
        kern,
        grid=(T // tr,),
        in_specs=[pl.BlockSpec((tr, D), lambda i: (i, 0)),
                  pl.BlockSpec((1, D), lambda i: (0, 0)),
                  pl.BlockSpec((D, LANES), lambda i: (0, 0)),
                  pl.BlockSpec((1, LANES), lambda i: (0, 0))],
        out_specs=[pl.BlockSpec((tr, LANES), lambda i: (i, 0)),
                   pl.BlockSpec((tr, LANES), lambda i: (i, 0)),
                   pl.BlockSpec((8, LANES), lambda i: (0, 0))],
        out_shape=[jax.ShapeDtypeStruct((T, LANES), jnp.int32),
                   jax.ShapeDtypeStruct((T, LANES), F32),
                   jax.ShapeDtypeStruct((8, LANES), F32)],
        scratch_shapes=[pltpu.VMEM((1, LANES), F32)],
        compiler_params=_cp(("arbitrary",)),
        name="router",
    )(x, norm_w, w_r, b_r)


def _row_copy(src, src_row, dst, dst_row, sem):
    return pltpu.make_async_copy(src.at[pl.ds(src_row, 1), :], dst.at[pl.ds(dst_row, 1), :], sem)


def _dispatch_kernel(e1_ref, r1_ref, e2_ref, r2_ref, off_ref, pad_ref, npad_ref, nu_ref, x_ref, nw_ref, xs_ref,
                     h_scr, z_scr, sem):
    i = pl.program_id(0)
    tr = x_ref.shape[0]
    n_experts = pad_ref.shape[0]
    tg = z_scr.shape[0]

    @pl.when(i == 0)
    def _():
        z_scr[...] = jnp.zeros_like(z_scr)
        n_all = xs_ref.shape[0] // tg

        def fill(start, rows):
            dst = xs_ref.at[pl.ds(pl.multiple_of(start, 8), rows), :]
            return pltpu.make_async_copy(z_scr.at[pl.ds(0, rows), :], dst, sem.at[2])

        def pad_fills(act):
            def per_expert(e, c):
                def piece(k, c2):
                    act(fill(pad_ref[e] + k * 8, 8))
                    return c2
                return lax.fori_loop(0, npad_ref[e], piece, c)
            lax.fori_loop(0, n_experts, per_expert, 0)

        def tail_fills(act):
            def tile(k, c):
                act(fill(k * tg, tg))
                return c
            lax.fori_loop(nu_ref[0], n_all, tile, 0)

        pad_fills(lambda cp: cp.start())
        tail_fills(lambda cp: cp.start())
        pad_fills(lambda cp: cp.wait())
        tail_fills(lambda cp: cp.wait())

    x = x_ref[...]
    h_scr[...] = x * lax.rsqrt(jnp.mean(x * x, axis=-1, keepdims=True) + EPS) * nw_ref[...]
    base = i * tr

    def issue(r, c):
        t = base + r
        _row_copy(h_scr, r, xs_ref, off_ref[e1_ref[t]] + r1_ref[t], sem.at[0]).start()
        _row_copy(h_scr, r, xs_ref, off_ref[e2_ref[t]] + r2_ref[t], sem.at[1]).start()
        return c
    lax.fori_loop(0, tr, issue, 0)
    pltpu.make_async_copy(h_scr, xs_ref.at[pl.ds(0, tr), :], sem.at[0]).wait()
    pltpu.make_async_copy(h_scr, xs_ref.at[pl.ds(0, tr), :], sem.at[1]).wait()


def _dispatch(route, pad_start, pad_groups, n_used, x, norm_w, *, n_rows, tr, tg):
    T, D = x.shape
    n_pre = len(route) + 3
    grid_spec = pltpu.PrefetchScalarGridSpec(
        num_scalar_prefetch=n_pre,
        grid=(T // tr,),
        in_specs=[pl.BlockSpec((tr, D), lambda i, *_: (i, 0)),
                  pl.BlockSpec((1, D), lambda i, *_: (0, 0))],
        out_specs=pl.BlockSpec(memory_space=pl.ANY),
        scratch_shapes=[pltpu.VMEM((tr, D), F32), pltpu.VMEM((tg, D), F32), pltpu.SemaphoreType.DMA((3,))],
    )
    return pl.pallas_call(
        _dispatch_kernel,
        grid_spec=grid_spec,
        out_shape=jax.ShapeDtypeStruct((n_rows, D), F32),
        compiler_params=_cp(("arbitrary",)),
        name="dispatch",
    )(*route, pad_start, pad_groups, n_used, x, norm_w)


def _expert_kernel(te_ref, nu_ref, xs_ref, wg_ref, wu_ref, wd_ref, o_ref, wg_scr, wu_scr, wd_scr):
    i = pl.program_id(0)
    prev = te_ref[jnp.maximum(i - 1, 0)]

    @pl.when((i == 0) | (te_ref[i] != prev))
    def _():
        wg_scr[...] = wg_ref[...].astype(BF16)
        wu_scr[...] = wu_ref[...].astype(BF16)
        wd_scr[...] = wd_ref[...].astype(BF16)

    @pl.when(i < nu_ref[0])
    def _():
        h = xs_ref[...].astype(BF16)
        g = jnp.dot(h, wg_scr[...], preferred_element_type=F32)
        u = jnp.dot(h, wu_scr[...], preferred_element_type=F32)
        a = (_silu(g) * u).astype(BF16)
        o_ref[...] = jnp.dot(a, wd_scr[...], preferred_element_type=F32)

    @pl.when(i >= nu_ref[0])
    def _():
        o_ref[...] = jnp.zeros_like(o_ref)


def _experts(tile_e, n_used, xs, w_gate, w_up, w_down, l, *, n_tiles, tg):
    D = xs.shape[1]
    F = w_gate.shape[-1]

    def xs_map(i, te, nu):
        return (jnp.maximum(jnp.minimum(i, nu[0] - 1), 0), 0)

    grid_spec = pltpu.PrefetchScalarGridSpec(
        num_scalar_prefetch=2,
        grid=(n_tiles,),
        in_specs=[pl.BlockSpec((tg, D), xs_map),
                  pl.BlockSpec((None, None, D, F), lambda i, te, nu: (l, te[i], 0, 0)),
                  pl.BlockSpec((None, None, D, F), lambda i, te, nu: (l, te[i], 0, 0)),
                  pl.BlockSpec((None, None, F, D), lambda i, te, nu: (l, te[i], 0, 0))],
        out_specs=pl.BlockSpec((tg, D), lambda i, te, nu: (i, 0)),
        scratch_shapes=[pltpu.VMEM((D, F), BF16), pltpu.VMEM((D, F), BF16), pltpu.VMEM((F, D), BF16)],
    )
    return pl.pallas_call(
        _expert_kernel,
        grid_spec=grid_spec,
        out_shape=jax.ShapeDtypeStruct((n_tiles * tg, D), F32),
        compiler_params=_cp(("arbitrary",)),
        name="experts",
    )(tile_e, n_used, xs, w_gate, w_up, w_down)


def _combine_kernel(e1_ref, r1_ref, e2_ref, r2_ref, off_ref, x_ref, rw_ref, mask_ref, fw_ref, eo_ref, out_ref,
                    b1_scr, b2_scr, sem, *, final):
    i = pl.program_id(0)
    tr = x_ref.shape[0]
    base = i * tr

    def issue(r, c):
        t = base + r
        _row_copy(eo_ref, off_ref[e1_ref[t]] + r1_ref[t], b1_scr, r, sem.at[0]).start()
        _row_copy(eo_ref, off_ref[e2_ref[t]] + r2_ref[t], b2_scr, r, sem.at[1]).start()
        return c
    lax.fori_loop(0, tr, issue, 0)
    pltpu.make_async_copy(eo_ref.at[pl.ds(0, tr), :], b1_scr, sem.at[0]).wait()
    pltpu.make_async_copy(eo_ref.at[pl.ds(0, tr), :], b2_scr, sem.at[1]).wait()
    rw = rw_ref[...]
    x2 = (x_ref[...] + rw[:, 0:1] * b1_scr[...] + rw[:, 1:2] * b2_scr[...]) * mask_ref[:, 0:1]
    if final:
        out_ref[...] = x2 * lax.rsqrt(jnp.mean(x2 * x2, axis=-1, keepdims=True) + EPS) * fw_ref[...]
    else:
        out_ref[...] = x2


def _combine(route, x, rw, mask, final_w, eo, *, tr, final):
    T, D = x.shape
    row = lambda w: pl.BlockSpec((tr, w), lambda i, *_: (i, 0))
    grid_spec = pltpu.PrefetchScalarGridSpec(
        num_scalar_prefetch=len(route),
        grid=(T // tr,),
        in_specs=[row(D), row(LANES), row(LANES),
                  pl.BlockSpec((1, D), lambda i, *_: (0, 0)),
                  pl.BlockSpec(memory_space=pl.ANY)],
        out_specs=row(D),
        scratch_shapes=[pltpu.VMEM((tr, D), F32), pltpu.VMEM((tr, D), F32), pltpu.SemaphoreType.DMA((2,))],
    )
    return pl.pallas_call(
        functools.partial(_combine_kernel, final=final),
        grid_spec=grid_spec,
        out_shape=jax.ShapeDtypeStruct((T, D), F32),
        compiler_params=_cp(("arbitrary",)),
        name="combine",
    )(*route, x, rw, mask, final_w, eo)


def _sample_to_rows(a):
    nb, L, C = a.shape
    return a.reshape(nb // SUBSET, SUBSET, L, C).transpose(0, 2, 1, 3).reshape(nb * L, C)


def _rows_to_sample(r, L):
    n, C = r.shape
    nb = n // L
    return r.reshape(nb // SUBSET, L, SUBSET, C).transpose(0, 2, 1, 3).reshape(nb, L, C)


def kernel(x_prompt, x_sample, state_delta, state_conv_a, state_conv_b, state_conv_c, meta_tokens, norm1_w, w_in, conv_a_w, a_log, dt_bias, onorm_a_w, w_out_a, conv_b_w, w_out_b, conv_c_w, conv_c_b, ln_c_w, ln_c_b, w_out_c, w_o, norm2_w, w_rg, b_rg, w_re, b_re, w_gate_e, w_up_e, w_down_e, final_norm_w):
    B, SEQ, D = x_prompt.shape
    NB, L, _ = x_sample.shape
    depth = w_in.shape[0]
    W_QKV = conv_a_w.shape[-1]
    W_V = H_A * DV
    W_B = conv_b_w.shape[-1]
    W_C = conv_c_w.shape[-1]
    CA, CB, CC = conv_a_w.shape[1], conv_b_w.shape[1], conv_c_w.shape[1]
    G = w_rg.shape[-1]
    E = w_re.shape[-1]
    assert L == DEC_SEQ and NB % SUBSET == 0 and SEQ % CHUNK == 0 and W_QKV == 3 * W_V
    assert W_V == W_B == W_C and max(CA, CB, CC) - 1 <= min(HALO, CHUNK - N_META)
    assert G + E <= LANES and D % LANES == 0

    LP = CHUNK + SEQ
    lead = CHUNK - N_META
    Ts, Tp = NB * L, B * LP
    T = Ts + Tp
    n_chunks = LP // CHUNK
    n_s_steps = Ts // CHUNK

    tr = _pick_tile(_gcd(Ts, Tp), 256, CHUNK)
    tm = _pick_tile(T, 1184, 16)
    tn = 512
    tg = 256
    n_s_tiles = Ts // tr

    dt_ = x_prompt.dtype
    lead_rows = jnp.concatenate([jnp.zeros((lead, D), dt_), meta_tokens.astype(dt_)], axis=0)
    xp = jnp.concatenate([jnp.broadcast_to(lead_rows[None], (B, CHUNK, D)), x_prompt], axis=1).reshape(Tp, D)
    x = jnp.concatenate([_sample_to_rows(x_sample), xp], axis=0)
    pos_in_seq = jnp.arange(Tp, dtype=jnp.int32) % LP
    real = jnp.concatenate([jnp.ones((Ts,), F32), (pos_in_seq >= lead).astype(F32)])
    mask = jnp.broadcast_to(real[:, None], (T, LANES))

    n_ab = 2 * H_A
    w_q = w_in[:, :, :W_QKV].astype(BF16)
    w_rest = w_in[:, :, W_QKV + n_ab:].astype(BF16)
    w_ab =jnp.pad(w_in[:, :, W_QKV:W_QKV + n_ab], ((0, 0), (0, 0), (0, LANES - n_ab))).astype(BF16)
    w_r = jnp.pad(jnp.concatenate([w_rg, w_re], axis=2), ((0, 0), (0, 0), (0, LANES - G - E)))
    b_r = jnp.pad(jnp.concatenate([b_rg, b_re], axis=1), ((0, 0), (0, LANES - G - E)))[:, None, :]
    adt = jnp.pad(jnp.stack([a_log, dt_bias], axis=1), ((0, 0), (0, 0), (0, LANES - H_A)))
    adt = jnp.pad(adt, ((0, 0), (0, 6), (0, 0)))
    woa, wob, woc, wo = (w.astype(BF16) for w in (w_out_a, w_out_b, w_out_c, w_o))

    col_sga, col_gb, col_u, col_gl, col_gs = W_QKV, W_QKV + W_V, W_QKV + W_V + W_B, W_QKV + W_V + 2 * W_B, W_QKV + W_V + 2 * W_B + W_C
    cw = W_V
    P = TOP_K * T + E * tg
    n_tiles = P // tg

    dp, ap, bpl, cpl, ds, as_, bs, cs = [], [], [], [], [], [], [], []
    for l in range(depth):
        z, ab = _inproj(x, norm1_w[l][None], w_q, w_rest, w_ab, l, w_v=W_V, w_b=W_B, w_c=W_C, tm=tm, tn=tn)

        def hist_rows(st):
            nb, hw, C = st.shape
            return st.reshape(nb // SUBSET, SUBSET, hw, C).transpose(0, 2, 1, 3).reshape(nb * hw, C)
        hist_a, hist_b, hist_c = hist_rows(state_conv_a[l]), hist_rows(state_conv_b[l]), hist_rows(state_conv_c[l])

        xs_, hs_, hi_ = _conv_specs(tr, cw, CA, lambda s: s, n_s_tiles, True)
        qkv, gb = pl.pallas_call(
            functools.partial(_conva_kernel, n_s_tiles=n_s_tiles, width=CA),
            grid=(T // tr, W_QKV // cw),
            in_specs=[xs_, hs_, hi_,
                      pl.BlockSpec((CA, cw), lambda i, s: (0, s)),
                      pl.BlockSpec((tr, LANES), lambda i, s: (i, 0)),
                      pl.BlockSpec((tr, LANES), lambda i, s: (i, 0)),
                      pl.BlockSpec((8, LANES), lambda i, s: (0, 0))],
            out_specs=[pl.BlockSpec((tr, cw), lambda i, s: (i, s)),
                       pl.BlockSpec((tr, LANES), lambda i, s: (i, 0))],
            out_shape=[jax.ShapeDtypeStruct((T, W_QKV), F32), jax.ShapeDtypeStruct((T, LANES), F32)],
            scratch_shapes=[pltpu.VMEM((HALO + tr, cw), F32), pltpu.VMEM((tr, cw), F32)],
            compiler_params=_cp(("arbitrary", "arbitrary")),
            name="conv_a",
        )(z, z, hist_a, conv_a_w[l], ab, mask, adt[l])

        ya, s_s, s_p = _delta(qkv, gb, z, onorm_a_w[l][None], state_delta, l,
                              n_s_steps=n_s_steps, n_batch=B, n_chunks=n_chunks, sga_col=col_sga // cw)

        xs_, hs_, hi_ = _conv_specs(tr, cw, CB, lambda s: col_u // cw, n_s_tiles, False)
        yb = pl.pallas_call(
            functools.partial(_convb_kernel, n_s_tiles=n_s_tiles, width=CB),
            grid=(T // tr,),
            in_specs=[xs_, hs_, hi_,
                      pl.BlockSpec((CB, cw), lambda i: (0, 0)),
                      pl.BlockSpec((tr, cw), lambda i: (i, col_gb // cw))],
            out_specs=pl.BlockSpec((tr, cw), lambda i: (i, 0)),
            out_shape=jax.ShapeDtypeStruct((T, cw), BF16),
            scratch_shapes=[pltpu.VMEM((HALO + tr, cw), F32), pltpu.VMEM((tr, cw), F32)],
            compiler_params=_cp(("arbitrary",)),
            name="conv_b",
        )(z, z, hist_b, conv_b_w[l], z)

        xs_, hs_, hi_ = _conv_specs(tr, cw, CC, lambda s: col_gl // cw, n_s_tiles, False)
        hc_rows = max(HALO + tr, (CC - 1) * SUBSET + CHUNK)
        yc = pl.pallas_call(
            functools.partial(_convc_kernel, n_s_tiles=n_s_tiles, width=CC),
            grid=(T // tr,),
            in_specs=[xs_, hs_, hi_,
                      pl.BlockSpec((CC, cw), lambda i: (0, 0)),
                      pl.BlockSpec((1, cw), lambda i: (0, 0)),
                      pl.BlockSpec((1, cw), lambda i: (0, 0)),
                      pl.BlockSpec((1, cw), lambda i: (0, 0))],
            out_specs=pl.BlockSpec((tr, cw), lambda i: (i, 0)),
            out_shape=jax.ShapeDtypeStruct((T, cw), BF16),
            scratch_shapes=[pltpu.VMEM((hc_rows, cw), F32), pltpu.VMEM((tr, cw), F32)],
            compiler_params=_cp(("arbitrary",)),
            name="conv_c",
        )(z, z, hist_c, conv_c_w[l], conv_c_b[l][None], ln_c_w[l][None], ln_c_b[l][None])

        m = _merge(ya, yb, yc, woa[l], wob[l], woc[l], z, gs_col=col_gs, tm=tm, tn=tn)
        x1 = _oproj(m, wo[l], x, tm=tm, tn=tn)

        ri, rw, cnt = _router(x1, norm2_w[l][None], w_r[l], b_r[l], n_groups=G, per_group=E // G, tr=tr)
        counts = cnt[0, G:G + E].astype(jnp.int32)
        padded = ((counts + tg - 1) // tg) * tg
        ends = jnp.cumsum(padded)
        offs = ends - padded
        route = (ri[:, 0], ri[:, 2], ri[:, 1], ri[:, 3], offs.astype(jnp.int32))
        n_used = (ends[-1] // tg).astype(jnp.int32)
        tile_start = jnp.arange(n_tiles, dtype=jnp.int32) * tg
        tile_e = jnp.sum((tile_start[:, None] >= ends[None, :]).astype(jnp.int32), axis=1)
        last_e = jnp.sum((((n_used - 1) * tg) >= ends).astype(jnp.int32))
        tile_e = jnp.where(jnp.arange(n_tiles) < n_used, tile_e, last_e).astype(jnp.int32)

        pad_start = (((offs + counts) // 8) * 8).astype(jnp.int32)
        n_used = n_used.reshape(1)
        pad_groups = ((ends - pad_start) // 8).astype(jnp.int32)
        xs = _dispatch(route, pad_start, pad_groups, n_used, x1, norm2_w[l][None], n_rows=P, tr=tr, tg=tg)
        eo = _experts(tile_e, n_used, xs, w_gate_e, w_up_e, w_down_e, l, n_tiles=n_tiles, tg=tg)
        x = _combine(route, x1, rw, mask, final_norm_w[None], eo, tr=tr, final=(l == depth - 1))

        def prompt_tail(c0, c1, width):
            return jnp.stack([lax.slice(z, (Ts + (b + 1) * LP - (width - 1), c0), (Ts + (b + 1) * LP, c1))
                              for b in range(B)])

        def sample_hist(st, c0, c1):
            seq = _rows_to_sample(lax.slice(z, (0, c0), (Ts, c1)), L)
            return jnp.concatenate([st.astype(dt_), seq], axis=1)[:, L:]
        dp.append(s_p)
        ap.append(prompt_tail(0, W_QKV, CA))
        bpl.append(prompt_tail(col_u, col_u + W_B, CB))
        cpl.append(prompt_tail(col_gl, col_gl + W_C, CC))
        ds.append(s_s)
        as_.append(sample_hist(state_conv_a[l], 0, W_QKV))
        bs.append(sample_hist(state_conv_b[l], col_u, col_u + W_B))
        cs.append(sample_hist(state_conv_c[l], col_gl, col_gl + W_C))

    y_sample = _rows_to_sample(lax.slice(x, (0, 0), (Ts, D)), L)
    y_prompt = jnp.stack([lax.slice(x, (Ts + b * LP + CHUNK, 0), (Ts + (b + 1) * LP, D)) for b in range(B)])
    return (y_prompt, y_sample, jnp.stack(dp), jnp.stack(ap), jnp.stack(bpl), jnp.stack(cpl),
            jnp.stack(ds), jnp.stack(as_), jnp.stack(bs), jnp.stack(cs))


def _gcd(a, b):
    while b:
        a, b = b, a % b
    return a
```

```python
import functools

import jax
import jax.numpy as jnp
from jax import lax
from jax.experimental import pallas as pl
from jax.experimental.pallas import tpu as pltpu

F32 = jnp.float32
BF16 = jnp.bfloat16
HIGHEST = lax.Precision.HIGHEST

EPS = 1e-6
LN_EPS = 1e-5
N_META = 16
H_A = 8
DK = 128
DV = 128
CHUNK = 64
LANES = 128
DEC_SEQ = 8
SUBSET = CHUNK // DEC_SEQ
HALO = 32
TOP_K = 2
NEG = -1e30
ISSUE_UNROLL = 8

VMEM_LIMIT = 52 * 1024 * 1024


def _cp(dims, vmem=VMEM_LIMIT):
    return pltpu.CompilerParams(dimension_semantics=dims, vmem_limit_bytes=vmem)


def _sigmoid(x):
    return 0.5 * jnp.tanh(0.5 * x) + 0.5


def _silu(x):
    return x * _sigmoid(x)


def _pick_tile(n, target, mult):
    best = None
    for t in range(mult, min(n, target) + 1, mult):
        if n % t == 0:
            best = t
    assert best is not None, (n, target, mult)
    return best


def _inproj_kernel(x_ref, nw_ref, wa_ref, wb_ref, wab_ref, z_ref, ab_ref, h_scr, *, bounds, row_chunk):
    j = pl.program_id(1)
    tm = x_ref.shape[0]

    def proj(w_ref):
        return lax.dot_general(h_scr[...], w_ref[0].astype(BF16), (((1,), (1,)), ((), ())),
                               preferred_element_type=F32)

    @pl.when(j == 0)
    def _():
        def body(r, c):
            rs = pl.ds(pl.multiple_of(r * row_chunk, row_chunk), row_chunk)
            x = x_ref[rs, :]
            h = x * lax.rsqrt(jnp.mean(x * x, axis=-1, keepdims=True) + EPS) * nw_ref[...]
            h_scr[rs, :] = h.astype(BF16)
            return c
        lax.fori_loop(0, tm // row_chunk, body, 0)
        ab_ref[...] = proj(wab_ref)

    b_silu, b_id2, b_mul, b_glu, b_sig = bounds

    @pl.when((j < b_silu) | ((j >= b_id2) & (j < b_mul)))
    def _():
        z_ref[...] = proj(wa_ref)

    @pl.when((j >= b_silu) & (j < b_id2))
    def _():
        z_ref[...] = _silu(proj(wa_ref))

    @pl.when((j >= b_mul) & (j < b_glu))
    def _():
        z_ref[...] = proj(wa_ref) * proj(wb_ref)

    @pl.when((j >= b_glu) & (j < b_sig))
    def _():
        z_ref[...] = proj(wa_ref) * _sigmoid(proj(wb_ref))

    @pl.when(j >= b_sig)
    def _():
        z_ref[...] = _sigmoid(proj(wa_ref))


def _inproj(x, norm_w, w_t, l, *, w_qkv, n_ab, w_v, w_b, w_c, tm, tn):
    T, D = x.shape
    nq, nv, nb, nc = w_qkv // tn, w_v // tn, w_b // tn, w_c // tn
    r0 = w_qkv + n_ab
    b_silu = nq
    b_id2 = nq + nv
    b_mul = b_id2 + nb
    b_glu = b_mul + nb
    b_sig = b_glu + nc
    n_out = nq + (w_t.shape[1] - r0) // tn - nb - nc
    park_lo = b_glu - nq
    park_hi = b_sig - nq + nb + nc - 1
    assert w_qkv % LANES == 0 and r0 % 8 == 0 and (w_t.shape[1] - r0) % tn == 0

    def wa_map(i, j):
        rest = j - nq + jnp.where(j >= b_glu, nb, 0) + jnp.where(j >= b_sig, nc, 0)
        return (l, pl.multiple_of(jnp.where(j < nq, j * tn, r0 + rest * tn), 8), 0)

    def wb_map(i, j):
        jb = jnp.where(j < b_mul, park_lo,
                       jnp.where(j < b_glu, j - nq + nb,
                                 jnp.where(j < b_sig, j - nq + nb + nc, park_hi)))
        return (l, pl.multiple_of(r0 + jb * tn, 8), 0)

    kern = functools.partial(_inproj_kernel, bounds=(b_silu, b_id2, b_mul, b_glu, b_sig), row_chunk=16)
    return pl.pallas_call(
        kern,
        grid=(T // tm, n_out),
        in_specs=[
            pl.BlockSpec((tm, D), lambda i, j: (i, 0), pipeline_mode=pl.Buffered(1)),
            pl.BlockSpec((1, D), lambda i, j: (0, 0)),
            pl.BlockSpec((pl.Element(1), pl.Element(tn), pl.Element(D)), wa_map),
            pl.BlockSpec((pl.Element(1), pl.Element(tn), pl.Element(D)), wb_map),
            pl.BlockSpec((1, LANES, D), lambda i, j: (l, w_qkv // LANES, 0)),
        ],
        out_specs=[
            pl.BlockSpec((tm, tn), lambda i, j: (i, j)),
            pl.BlockSpec((tm, LANES), lambda i, j: (i, 0)),
        ],
        out_shape=[jax.ShapeDtypeStruct((T, n_out * tn), F32), jax.ShapeDtypeStruct((T, LANES), F32)],
        scratch_shapes=[pltpu.VMEM((tm, D), BF16)],
        compiler_params=_cp(("arbitrary", "arbitrary")),
        name="inproj",
    )(x, norm_w, w_t, w_t, w_t)


def _conv_taps(xp_scr, base, rows, step, w_ref, width):
    acc = None
    for j in range(width):
        off = base - (width - 1 - j) * step
        term = xp_scr[pl.ds(off, rows), :] * w_ref[j:j + 1, :]
        acc = term if acc is None else acc + term
    return acc


def _conv_tile(i, n_s_tiles, x_ref, halo_ref, hist_ref, w_ref, xp_scr, y_scr, width):
    tr = x_ref.shape[0]
    hsub = (width - 1) * SUBSET
    nsub = tr // CHUNK

    @pl.when(i < n_s_tiles)
    def _():
        for s in range(nsub):
            xp_scr[0:hsub, :] = hist_ref[s * hsub:(s + 1) * hsub, :]
            xp_scr[hsub:hsub + CHUNK, :] = x_ref[s * CHUNK:(s + 1) * CHUNK, :]
            y_scr[s * CHUNK:(s + 1) * CHUNK, :] = _conv_taps(xp_scr, hsub, CHUNK, SUBSET, w_ref, width)

    @pl.when(i >= n_s_tiles)
    def _():
        halo = halo_ref[...]
        xp_scr[0:HALO, :] = jnp.where(i == n_s_tiles, jnp.zeros_like(halo), halo)
        xp_scr[HALO:HALO + tr, :] = x_ref[...]
        y_scr[...] = _conv_taps(xp_scr, HALO, tr, 1, w_ref, width)


def _softplus(x):
    return jnp.maximum(x, 0.0) + jnp.log(1.0 + jnp.exp(-jnp.abs(x)))


def _conva_kernel(x_ref, halo_ref, hist_ref, w_ref, ab_ref, mask_ref, adt_ref, o_ref, gb_ref, xp_scr, y_scr,
                  *, n_s_tiles, width):
    i = pl.program_id(0)
    sec = pl.program_id(1)
    _conv_tile(i, n_s_tiles, x_ref, halo_ref, hist_ref, w_ref, xp_scr, y_scr, width)

    @pl.when(sec == 0)
    def _():
        ab = ab_ref[...]
        lane = lax.broadcasted_iota(jnp.int32, ab.shape, 1)
        g = -jnp.exp(adt_ref[0:1, :]) * _softplus(ab + adt_ref[1:2, :])
        gb_ref[...] = jnp.where(lane < H_A, g, _sigmoid(ab)) * mask_ref[...]

    scale = jnp.where(sec == 0, DK ** -0.5, 1.0).astype(F32)
    for h in range(H_A):
        hs = slice(h * DK, (h + 1) * DK)
        y = _silu(y_scr[:, hs])
        yn = y * (lax.rsqrt(jnp.sum(y * y, axis=-1, keepdims=True) + EPS) * scale)
        o_ref[:, hs] = jnp.where(sec == 2, y, yn)


def _convb_kernel(x_ref, halo_ref, hist_ref, w_ref, gate_ref, o_ref, xp_scr, y_scr, *, n_s_tiles, width):
    i = pl.program_id(0)
    _conv_tile(i, n_s_tiles, x_ref, halo_ref, hist_ref, w_ref, xp_scr, y_scr, width)
    o_ref[...] = (gate_ref[...] * y_scr[...]).astype(o_ref.dtype)


def _convc_kernel(x_ref, halo_ref, hist_ref, w_ref, cb_ref, lnw_ref, lnb_ref, o_ref, xp_scr, y_scr,
                  *, n_s_tiles, width):
    i = pl.program_id(0)
    _conv_tile(i, n_s_tiles, x_ref, halo_ref, hist_ref, w_ref, xp_scr, y_scr, width)
    y = y_scr[...] + cb_ref[...]
    mu = jnp.mean(y, axis=-1, keepdims=True)
    yc = y - mu
    var = jnp.mean(yc * yc, axis=-1, keepdims=True)
    yn = yc * lax.rsqrt(var + LN_EPS) * lnw_ref[...] + lnb_ref[...]
    o_ref[...] = _silu(yn).astype(o_ref.dtype)


def _conv_specs(tr, cw, width, col_of, n_s_tiles, sec_axis):
    hrows = (tr // CHUNK) * (width - 1) * SUBSET
    hb = tr // HALO
    last_hist = max(n_s_tiles - 1, 0)
    if sec_axis:
        x_map = lambda i, s: (i, col_of(s))
        halo_map = lambda i, s: (jnp.maximum(i * hb - 1, 0), col_of(s))
        hist_map = lambda i, s: (jnp.minimum(i, last_hist), s)
    else:
        x_map = lambda i: (i, col_of(0))
        halo_map = lambda i: (jnp.maximum(i * hb - 1, 0), col_of(0))
        hist_map = lambda i: (jnp.minimum(i, last_hist), 0)
    return [pl.BlockSpec((tr, cw), x_map), pl.BlockSpec((HALO, cw), halo_map), pl.BlockSpec((hrows, cw), hist_map)]


def _dot(a, b):
    return jnp.dot(a.astype(BF16), b.astype(BF16), preferred_element_type=F32)


def _dot_nt(a, b):
    return lax.dot_general(a.astype(BF16), b.astype(BF16), (((1,), (1,)), ((), ())), preferred_element_type=F32)


def _split(x):
    hi = x.astype(BF16)
    return hi, (x - hi.astype(F32)).astype(BF16)


def _mask_dot(mask_bf, x):
    hi, lo = _split(x)
    return (jnp.dot(mask_bf, hi, preferred_element_type=F32)
            + jnp.dot(mask_bf, lo, preferred_element_type=F32))


def _mask_dot_nt(xt, mask_bf):
    hi, lo = _split(xt)
    dn = (((1,), (1,)), ((), ()))
    return (lax.dot_general(hi, mask_bf, dn, preferred_element_type=F32)
            + lax.dot_general(lo, mask_bf, dn, preferred_element_type=F32))


def _bf_mask(m):
    return jnp.where(m, 1.0, 0.0).astype(BF16)


def _chunk_common(q_ref, k_ref, v_ref, gb, gc_all, gr_all, lmask, strict):
    heads = range(H_A)
    hs = [slice(h * DK, (h + 1) * DK) for h in heads]
    q = [q_ref[:, hs[h]] for h in heads]
    k = [k_ref[:, hs[h]] for h in heads]
    v = [v_ref[:, hs[h]] for h in heads]
    gc = [gc_all[:, h:h + 1] for h in heads]
    beta = [gb[:, H_A + h:H_A + h + 1] for h in heads]
    decay = [jnp.exp(jnp.where(lmask, gc[h] - gr_all[h:h + 1, :], NEG)) for h in heads]
    kk = [_dot_nt(k[h], k[h]) for h in heads]
    qk = [_dot_nt(q[h], k[h]) * decay[h] for h in heads]
    m = [jnp.where(strict, beta[h] * kk[h] * decay[h], 0.0) for h in heads]
    eg = [jnp.exp(gc[h]) for h in heads]
    rhs = [jnp.concatenate([beta[h] * v[h], (beta[h] * eg[h]) * k[h]], axis=1) for h in heads]
    return q, k, gc, qk, m, eg, rhs


def _delta_kernel(q_ref, k_ref, v_ref, gb_ref, sga_ref, onw_ref, s0s_ref, ya_ref, ss_ref, sp_ref, s_scr,
                  *, n_s_steps, n_chunks):
    s = pl.program_id(0)
    C = CHUNK
    row = lax.broadcasted_iota(jnp.int32, (C, C), 0)
    col = lax.broadcasted_iota(jnp.int32, (C, C), 1)
    eye = (row == col).astype(F32)
    gb = gb_ref[...]
    gbt = gb.T[0:H_A, :]

    def finish(o, h):
        hs = slice(h * DV, (h + 1) * DV)
        on = o * lax.rsqrt(jnp.mean(o * o, axis=-1, keepdims=True) + EPS) * onw_ref[...]
        ya_ref[:, hs] = (on * sga_ref[:, hs]).astype(ya_ref.dtype)

    heads = range(H_A)

    @pl.when(s < n_s_steps)
    def _():
        same = ((row - col) & (DEC_SEQ - 1)) == 0
        lmask = same & (row >= col)
        strict = same & (row > col)
        lm = _bf_mask(lmask)
        last = _bf_mask(col == (C - DEC_SEQ) + (row & (DEC_SEQ - 1)))
        gc_all = _mask_dot(lm, gb)
        gr_all = _mask_dot_nt(gbt, lm)
        gl_all = _mask_dot(last, gc_all)
        rsub = lax.broadcasted_iota(jnp.int32, (C, 1), 0) & (DEC_SEQ - 1)
        rsub2 = jnp.concatenate([rsub, rsub], axis=0)
        q, k, gc, qk, m, eg, rhs = _chunk_common(q_ref, k_ref, v_ref, gb, gc_all, gr_all, lmask, strict)
        m2 = [_dot(m[h], m[h]) for h in heads]
        b1 = [eye - m[h] for h in heads]
        b2 = [b1[h] + _dot(b1[h], m2[h]) for h in heads]
        m4 = [_dot(m2[h], m2[h]) for h in heads]
        tinv = [b2[h] + _dot(b2[h], m4[h]) for h in heads]
        x = [_dot(tinv[h], rhs[h]) for h in heads]
        lhs = [jnp.concatenate([x[h][:, DV:], q[h]], axis=0) for h in heads]
        sk = []
        for h in heads:
            acc = jnp.zeros((2 * C, DV), F32)
            for i in range(SUBSET):
                acc = acc + jnp.where(rsub2 == i, _dot(lhs[h], s0s_ref[i, h]), 0.0)
            sk.append(acc)
        u = [x[h][:, :DV] - sk[h][:C] for h in heads]
        o = [eg[h] * sk[h][C:] + _dot(qk[h], u[h]) for h in heads]
        kd = [k[h] * jnp.exp(gl_all[:, h:h + 1] - gc[h]) for h in heads]
        for h in heads:
            for i in range(SUBSET):
                kdi = jnp.where(rsub == i, kd[h], 0.0)
                gli = gc_all[C - DEC_SEQ + i:C - DEC_SEQ + i + 1, h:h + 1]
                ss_ref[i, h] = jnp.exp(gli) * s0s_ref[i, h] + _dot(kdi.T, u[h])
        for h in heads:
            finish(o[h], h)

    @pl.when(s >= n_s_steps)
    def _():
        c = (s - n_s_steps) % n_chunks

        @pl.when(c == 0)
        def _():
            s_scr[...] = jnp.zeros_like(s_scr)

        lmask = row >= col
        strict = row > col
        lm = _bf_mask(lmask)
        blk = (row >> 4) == (col >> 4)
        gc_all = _mask_dot(lm, gb)
        gr_all = _mask_dot_nt(gbt, lm)
        q, k, gc, qk, m, eg, rhs = _chunk_common(q_ref, k_ref, v_ref, gb, gc_all, gr_all, lmask, strict)
        nd = [jnp.where(blk, -m[h], 0.0) for h in heads]
        lo = [jnp.where(blk, 0.0, m[h]) for h in heads]
        p2 = [_dot(nd[h], nd[h]) for h in heads]
        a1 = [eye + nd[h] for h in heads]
        t1 = [a1[h] + _dot(a1[h], p2[h]) for h in heads]
        p4 = [_dot(p2[h], p2[h]) for h in heads]
        t2 = [t1[h] + _dot(t1[h], p4[h]) for h in heads]
        p8 = [_dot(p4[h], p4[h]) for h in heads]
        dinv = [t2[h] + _dot(t2[h], p8[h]) for h in heads]
        y = [_dot(dinv[h], rhs[h]) for h in heads]
        a = [_dot(dinv[h], lo[h]) for h in heads]
        w = [y[h] - _dot(a[h], y[h]) for h in heads]
        a2 = [_dot(a[h], a[h]) for h in heads]
        x = [w[h] + _dot(a2[h], w[h]) for h in heads]
        S = [s_scr[h] for h in heads]
        sk = [_dot(jnp.concatenate([x[h][:, DV:], q[h]], axis=0), S[h]) for h in heads]
        u = [x[h][:, :DV] - sk[h][:C] for h in heads]
        o = [eg[h] * sk[h][C:] + _dot(qk[h], u[h]) for h in heads]
        gl = [gc_all[C - 1:C, h:h + 1] for h in heads]
        kd = [k[h] * jnp.exp(gl[h] - gc[h]) for h in heads]
        for h in heads:
            s_scr[h] = jnp.exp(gl[h]) * S[h] + _dot(kd[h].T, u[h])
        for h in heads:
            finish(o[h], h)

        @pl.when(c == n_chunks - 1)
        def _():
            sp_ref[0] = s_scr[...]


def _delta(qkv, gb, z, onorm_w, state_all, l, *, n_s_steps, n_batch, n_chunks, sga_col):
    T = qkv.shape[0]
    W = H_A * DK
    n_steps = n_s_steps + n_batch * n_chunks
    last_s = max(n_s_steps - 1, 0)
    kern = functools.partial(_delta_kernel, n_s_steps=n_s_steps, n_chunks=n_chunks)
    return pl.pallas_call(
        kern,
        grid=(n_steps,),
        in_specs=[
            pl.BlockSpec((CHUNK, W), lambda s: (s, 0)),
            pl.BlockSpec((CHUNK, W), lambda s: (s, 1)),
            pl.BlockSpec((CHUNK, W), lambda s: (s, 2)),
            pl.BlockSpec((CHUNK, LANES), lambda s: (s, 0)),
            pl.BlockSpec((CHUNK, W), lambda s: (s, sga_col)),
            pl.BlockSpec((1, DV), lambda s: (0, 0)),
            pl.BlockSpec((None, SUBSET, H_A, DK, DV), lambda s: (l, jnp.minimum(s, last_s), 0, 0, 0)),
        ],
        out_specs=[
            pl.BlockSpec((CHUNK, W), lambda s: (s, 0)),
            pl.BlockSpec((SUBSET, H_A, DK, DV), lambda s: (jnp.minimum(s, last_s), 0, 0, 0)),
            pl.BlockSpec((1, H_A, DK, DV), lambda s: (jnp.maximum(s - n_s_steps, 0) // n_chunks, 0, 0, 0)),
        ],
        out_shape=[
            jax.ShapeDtypeStruct((T, W), BF16),
            jax.ShapeDtypeStruct(state_all.shape[1:], F32),
            jax.ShapeDtypeStruct((n_batch, H_A, DK, DV), F32),
        ],
        scratch_shapes=[pltpu.VMEM((H_A, DK, DV), F32)],
        compiler_params=_cp(("arbitrary",)),
        name="delta",
    )(qkv, qkv, qkv, gb, z, onorm_w, state_all)


def _merge_kernel(ya_ref, yb_ref, yc_ref, wa_ref, wb_ref, wc_ref, g0_ref, g1_ref, g2_ref, m_ref):
    m = g0_ref[...] * jnp.dot(ya_ref[...], wa_ref[...], preferred_element_type=F32)
    m = m + g1_ref[...] * jnp.dot(yb_ref[...], wb_ref[...], preferred_element_type=F32)
    m = m + g2_ref[...] * jnp.dot(yc_ref[...], wc_ref[...], preferred_element_type=F32)
    m_ref[...] = m.astype(m_ref.dtype)


def _merge(ya, yb, yc, wa, wb, wc, z, *, gs_col, tm, tn):
    T, K = ya.shape
    D = wa.shape[1]
    nb = D // tn
    g0 = gs_col // tn
    act = pl.BlockSpec((tm, K), lambda i, j: (i, 0))
    wsp = pl.BlockSpec((K, tn), lambda i, j: (0, j))
    return pl.pallas_call(
        _merge_kernel,
        grid=(T // tm, nb),
        in_specs=[act, act, act, wsp, wsp, wsp,
                  pl.BlockSpec((tm, tn), lambda i, j: (i, g0 + j)),
                  pl.BlockSpec((tm, tn), lambda i, j: (i, g0 + nb + j)),
                  pl.BlockSpec((tm, tn), lambda i, j: (i, g0 + 2 * nb + j))],
        out_specs=pl.BlockSpec((tm, tn), lambda i, j: (i, j)),
        out_shape=jax.ShapeDtypeStruct((T, D), BF16),
        compiler_params=_cp(("arbitrary", "arbitrary")),
        name="merge",
    )(ya, yb, yc, wa, wb, wc, z, z, z)


def _oproj_kernel(m_ref, w_ref, x_ref, o_ref):
    o_ref[...] = x_ref[...] + jnp.dot(m_ref[...], w_ref[...], preferred_element_type=F32)


def _oproj(m, w_o, x, *, tm, tn):
    T, D = x.shape
    return pl.pallas_call(
        _oproj_kernel,
        grid=(T // tm, D // tn),
        in_specs=[pl.BlockSpec((tm, D), lambda i, j: (i, 0)),
                  pl.BlockSpec((D, tn), lambda i, j: (0, j)),
                  pl.BlockSpec((tm, tn), lambda i, j: (i, j))],
        out_specs=pl.BlockSpec((tm, tn), lambda i, j: (i, j)),
        out_shape=jax.ShapeDtypeStruct((T, D), F32),
        compiler_params=_cp(("arbitrary", "arbitrary")),
        name="oproj",
    )(m, w_o, x)


def _dot_hi(a, b):
    return jnp.dot(a, b, preferred_element_type=F32, precision=HIGHEST)


def _router_kernel(x_ref, nw_ref, wr_ref, br_ref, ri_ref, rw_ref, cnt_ref, carry_scr, *, n_groups, per_group):
    i = pl.program_id(0)
    tr = x_ref.shape[0]

    @pl.when(i == 0)
    def _():
        carry_scr[...] = jnp.zeros_like(carry_scr)

    x = x_ref[...]
    h = x * lax.rsqrt(jnp.mean(x * x, axis=-1, keepdims=True) + EPS) * nw_ref[...]
    logits = _dot_hi(h, wr_ref[...]) + br_ref[...]
    lane = lax.broadcasted_iota(jnp.int32, logits.shape, 1)
    big = jnp.int32(1 << 20)

    def argmax_first(vals):
        m = jnp.max(vals, axis=-1, keepdims=True)
        idx = jnp.min(jnp.where(vals == m, lane, big), axis=-1, keepdims=True)
        return m, idx

    is_g = lane < n_groups
    mg, gsel = argmax_first(jnp.where(is_g, logits, NEG))
    pg = 1.0 / jnp.sum(jnp.where(is_g, jnp.exp(logits - mg), 0.0), axis=-1, keepdims=True)
    lo = n_groups + gsel * per_group
    in_group = (lane >= lo) & (lane < lo + per_group)
    le = jnp.where(in_group, logits, NEG)
    m1, i1 = argmax_first(le)
    m2, i2 = argmax_first(jnp.where(lane == i1, NEG, le))
    e21 = jnp.exp(m2 - m1)
    w1 = pg / (1.0 + e21)
    w2 = w1 * e21

    oh = ((lane == i1) | (lane == i2)).astype(F32)
    r_i = lax.broadcasted_iota(jnp.int32, (tr, tr), 0)
    c_i = lax.broadcasted_iota(jnp.int32, (tr, tr), 1)
    before = _dot((r_i > c_i).astype(F32), oh) + carry_scr[...]
    rank1 = jnp.sum(jnp.where(lane == i1, before, 0.0), axis=-1, keepdims=True)
    rank2 = jnp.sum(jnp.where(lane == i2, before, 0.0), axis=-1, keepdims=True)
    carry_scr[...] = carry_scr[...] + jnp.sum(oh, axis=0, keepdims=True)
    cnt_ref[...] = jnp.broadcast_to(carry_scr[...], cnt_ref.shape)

    ri = jnp.where(lane == 0, i1 - n_groups,
                   jnp.where(lane == 1, i2 - n_groups,
                             jnp.where(lane == 2, rank1.astype(jnp.int32),
                                       jnp.where(lane == 3, rank2.astype(jnp.int32), 0))))
    ri_ref[...] = ri
    rw_ref[...] = jnp.where(lane == 0, w1, jnp.where(lane == 1, w2, 0.0))


def _router(x, norm_w, w_r, b_r, *, n_groups, per_group, tr):
    T, D = x.shape
    kern = functools.partial(_router_kernel, n_groups=n_groups, per_group=per_group)
    return pl.pallas_call(
        kern,
        grid=(T // tr,),
        in_specs=[pl.BlockSpec((tr, D), lambda i: (i, 0)),
                  pl.BlockSpec((1, D), lambda i: (0, 0)),
                  pl.BlockSpec((D, LANES), lambda i: (0, 0)),
                  pl.BlockSpec((1, LANES), lambda i: (0, 0))],
        out_specs=[pl.BlockSpec((tr, LANES), lambda i: (i, 0)),
                   pl.BlockSpec((tr, LANES), lambda i: (i, 0)),
                   pl.BlockSpec((8, LANES), lambda i: (0, 0))],
        out_shape=[jax.ShapeDtypeStruct((T, LANES), jnp.int32),
                   jax.ShapeDtypeStruct((T, LANES), F32),
                   jax.ShapeDtypeStruct((8, LANES), F32)],
        scratch_shapes=[pltpu.VMEM((1, LANES), F32)],
        compiler_params=_cp(("arbitrary",)),
        name="router",
    )(x, norm_w, w_r, b_r)


def _row_copy(src, src_row, dst, dst_row, sem):
    return pltpu.make_async_copy(src.at[pl.ds(src_row, 1), :], dst.at[pl.ds(dst_row, 1), :], sem)


def _dispatch_kernel(e1_ref, r1_ref, e2_ref, r2_ref, off_ref, pad_ref, npad_ref, nu_ref, x_ref, nw_ref, xs_ref,
                     h_scr, z_scr, sem):
    i = pl.program_id(0)
    tr = x_ref.shape[0]
    n_experts = pad_ref.shape[0]
    tg = z_scr.shape[0]

    @pl.when(i == 0)
    def _():
        z_scr[...] = jnp.zeros_like(z_scr)
        n_all = xs_ref.shape[0] // tg

        def fill(start, rows):
            dst = xs_ref.at[pl.ds(pl.multiple_of(start, 8), rows), :]
            return pltpu.make_async_copy(z_scr.at[pl.ds(0, rows), :], dst, sem.at[2])

        def pad_fills(act):
            def per_expert(e, c):
                def piece(k, c2):
                    act(fill(pad_ref[e] + k * 8, 8))
                    return c2
                return lax.fori_loop(0, npad_ref[e], piece, c)
            lax.fori_loop(0, n_experts, per_expert, 0)

        def tail_fills(act):
            def tile(k, c):
                act(fill(k * tg, tg))
                return c
            lax.fori_loop(nu_ref[0], n_all, tile, 0)

        pad_fills(lambda cp: cp.start())
        tail_fills(lambda cp: cp.start())
        pad_fills(lambda cp: cp.wait())
        tail_fills(lambda cp: cp.wait())

    x = x_ref[...]
    h_scr[...] = x * lax.rsqrt(jnp.mean(x * x, axis=-1, keepdims=True) + EPS) * nw_ref[...]
    base = i * tr

    def issue(r, c):
        t = base + r
        _row_copy(h_scr, r, xs_ref, off_ref[e1_ref[t]] + r1_ref[t], sem.at[0]).start()
        _row_copy(h_scr, r, xs_ref, off_ref[e2_ref[t]] + r2_ref[t], sem.at[1]).start()
        return c
    lax.fori_loop(0, tr, issue, 0, unroll=ISSUE_UNROLL)
    pltpu.make_async_copy(h_scr, xs_ref.at[pl.ds(0, tr), :], sem.at[0]).wait()
    pltpu.make_async_copy(h_scr, xs_ref.at[pl.ds(0, tr), :], sem.at[1]).wait()


def _dispatch(route, pad_start, pad_groups, n_used, x, norm_w, *, n_rows, tr, tg):
    T, D = x.shape
    n_pre = len(route) + 3
    grid_spec = pltpu.PrefetchScalarGridSpec(
        num_scalar_prefetch=n_pre,
        grid=(T // tr,),
        in_specs=[pl.BlockSpec((tr, D), lambda i, *_: (i, 0)),
                  pl.BlockSpec((1, D), lambda i, *_: (0, 0))],
        out_specs=pl.BlockSpec(memory_space=pl.ANY),
        scratch_shapes=[pltpu.VMEM((tr, D), F32), pltpu.VMEM((tg, D), F32), pltpu.SemaphoreType.DMA((3,))],
    )
    return pl.pallas_call(
        _dispatch_kernel,
        grid_spec=grid_spec,
        out_shape=jax.ShapeDtypeStruct((n_rows, D), F32),
        compiler_params=_cp(("arbitrary",)),
        name="dispatch",
    )(*route, pad_start, pad_groups, n_used, x, norm_w)


def _expert_kernel(te_ref, nu_ref, xs_ref, wg_ref, wu_ref, wd_ref, o_ref, wg_scr, wu_scr, wd_scr):
    i = pl.program_id(0)
    prev = te_ref[jnp.maximum(i - 1, 0)]

    @pl.when((i == 0) | (te_ref[i] != prev))
    def _():
        wg_scr[...] = wg_ref[...].astype(BF16)
        wu_scr[...] = wu_ref[...].astype(BF16)
        wd_scr[...] = wd_ref[...].astype(BF16)

    @pl.when(i < nu_ref[0])
    def _():
        h = xs_ref[...].astype(BF16)
        g = jnp.dot(h, wg_scr[...], preferred_element_type=F32)
        u = jnp.dot(h, wu_scr[...], preferred_element_type=F32)
        a = (_silu(g) * u).astype(BF16)
        o_ref[...] = jnp.dot(a, wd_scr[...], preferred_element_type=F32)

    @pl.when(i >= nu_ref[0])
    def _():
        o_ref[...] = jnp.zeros_like(o_ref)


def _experts(tile_e, n_used, xs, w_gate, w_up, w_down, l, *, n_tiles, tg):
    D = xs.shape[1]
    F = w_gate.shape[-1]

    def xs_map(i, te, nu):
        return (jnp.maximum(jnp.minimum(i, nu[0] - 1), 0), 0)

    grid_spec = pltpu.PrefetchScalarGridSpec(
        num_scalar_prefetch=2,
        grid=(n_tiles,),
        in_specs=[pl.BlockSpec((tg, D), xs_map),
                  pl.BlockSpec((None, None, D, F), lambda i, te, nu: (l, te[i], 0, 0)),
                  pl.BlockSpec((None, None, D, F), lambda i, te, nu: (l, te[i], 0, 0)),
                  pl.BlockSpec((None, None, F, D), lambda i, te, nu: (l, te[i], 0, 0))],
        out_specs=pl.BlockSpec((tg, D), lambda i, te, nu: (i, 0)),
        scratch_shapes=[pltpu.VMEM((D, F), BF16), pltpu.VMEM((D, F), BF16), pltpu.VMEM((F, D), BF16)],
    )
    return pl.pallas_call(
        _expert_kernel,
        grid_spec=grid_spec,
        out_shape=jax.ShapeDtypeStruct((n_tiles * tg, D), F32),
        compiler_params=_cp(("arbitrary",)),
        name="experts",
    )(tile_e, n_used, xs, w_gate, w_up, w_down)


def _combine_kernel(e1_ref, r1_ref, e2_ref, r2_ref, off_ref, x_ref, rw_ref, mask_ref, fw_ref, eo_ref, out_ref,
                    b1_scr, b2_scr, sem, *, final):
    i = pl.program_id(0)
    tr = x_ref.shape[0]
    base = i * tr

    def issue(r, c):
        t = base + r
        _row_copy(eo_ref, off_ref[e1_ref[t]] + r1_ref[t], b1_scr, r, sem.at[0]).start()
        _row_copy(eo_ref, off_ref[e2_ref[t]] + r2_ref[t], b2_scr, r, sem.at[1]).start()
        return c
    lax.fori_loop(0, tr, issue, 0, unroll=ISSUE_UNROLL)
    pltpu.make_async_copy(eo_ref.at[pl.ds(0, tr), :], b1_scr, sem.at[0]).wait()
    pltpu.make_async_copy(eo_ref.at[pl.ds(0, tr), :], b2_scr, sem.at[1]).wait()
    rw = rw_ref[...]
    x2 = (x_ref[...] + rw[:, 0:1] * b1_scr[...] + rw[:, 1:2] * b2_scr[...]) * mask_ref[:, 0:1]
    if final:
        out_ref[...] = x2 * lax.rsqrt(jnp.mean(x2 * x2, axis=-1, keepdims=True) + EPS) * fw_ref[...]
    else:
        out_ref[...] = x2


def _combine(route, x, rw, mask, final_w, eo, *, tr, final):
    T, D = x.shape
    row = lambda w: pl.BlockSpec((tr, w), lambda i, *_: (i, 0))
    grid_spec = pltpu.PrefetchScalarGridSpec(
        num_scalar_prefetch=len(route),
        grid=(T // tr,),
        in_specs=[row(D), row(LANES), row(LANES),
                  pl.BlockSpec((1, D), lambda i, *_: (0, 0)),
                  pl.BlockSpec(memory_space=pl.ANY)],
        out_specs=row(D),
        scratch_shapes=[pltpu.VMEM((tr, D), F32), pltpu.VMEM((tr, D), F32), pltpu.SemaphoreType.DMA((2,))],
    )
    return pl.pallas_call(
        functools.partial(_combine_kernel, final=final),
        grid_spec=grid_spec,
        out_shape=jax.ShapeDtypeStruct((T, D), F32),
        compiler_params=_cp(("arbitrary",)),
        name="combine",
    )(*route, x, rw, mask, final_w, eo)


def _sample_to_rows(a):
    nb, L, C = a.shape
    return a.reshape(nb // SUBSET, SUBSET, L, C).transpose(0, 2, 1, 3).reshape(nb * L, C)


def _rows_to_sample(r, L):
    n, C = r.shape
    nb = n // L
    return r.reshape(nb // SUBSET, L, SUBSET, C).transpose(0, 2, 1, 3).reshape(nb, L, C)


def kernel(x_prompt, x_sample, state_delta, state_conv_a, state_conv_b, state_conv_c, meta_tokens, norm1_w, w_in, conv_a_w, a_log, dt_bias, onorm_a_w, w_out_a, conv_b_w, w_out_b, conv_c_w, conv_c_b, ln_c_w, ln_c_b, w_out_c, w_o, norm2_w, w_rg, b_rg, w_re, b_re, w_gate_e, w_up_e, w_down_e, final_norm_w):
    B, SEQ, D = x_prompt.shape
    NB, L, _ = x_sample.shape
    depth = w_in.shape[0]
    W_QKV = conv_a_w.shape[-1]
    W_V = H_A * DV
    W_B = conv_b_w.shape[-1]
    W_C = conv_c_w.shape[-1]
    CA, CB, CC = conv_a_w.shape[1], conv_b_w.shape[1], conv_c_w.shape[1]
    G = w_rg.shape[-1]
    E = w_re.shape[-1]
    assert L == DEC_SEQ and NB % SUBSET == 0 and SEQ % CHUNK == 0 and W_QKV == 3 * W_V
    assert W_V == W_B == W_C and max(CA, CB, CC) - 1 <= min(HALO, CHUNK - N_META)
    assert G + E <= LANES and D % LANES == 0

    LP = CHUNK + SEQ
    lead = CHUNK - N_META
    Ts, Tp = NB * L, B * LP
    T = Ts + Tp
    n_chunks = LP // CHUNK
    n_s_steps = Ts // CHUNK

    tr = _pick_tile(_gcd(Ts, Tp), 256, CHUNK)
    tm = _pick_tile(T, 1184, 16)
    tn = 512
    tg = 256
    n_s_tiles = Ts // tr

    dt_ = x_prompt.dtype
    lead_rows = jnp.concatenate([jnp.zeros((lead, D), dt_), meta_tokens.astype(dt_)], axis=0)
    xp = jnp.concatenate([jnp.broadcast_to(lead_rows[None], (B, CHUNK, D)), x_prompt], axis=1).reshape(Tp, D)
    x = jnp.concatenate([_sample_to_rows(x_sample), xp], axis=0)
    pos_in_seq = jnp.arange(Tp, dtype=jnp.int32) % LP
    real = jnp.concatenate([jnp.ones((Ts,), F32), (pos_in_seq >= lead).astype(F32)])
    mask = jnp.broadcast_to(real[:, None], (T, LANES))

    n_ab = 2 * H_A
    w_t = jnp.swapaxes(w_in, 1, 2)
    w_r = jnp.pad(jnp.concatenate([w_rg, w_re], axis=2), ((0, 0), (0, 0), (0, LANES - G - E)))
    b_r = jnp.pad(jnp.concatenate([b_rg, b_re], axis=1), ((0, 0), (0, LANES - G - E)))[:, None, :]
    adt = jnp.pad(jnp.stack([a_log, dt_bias], axis=1), ((0, 0), (0, 0), (0, LANES - H_A)))
    adt = jnp.pad(adt, ((0, 0), (0, 6), (0, 0)))
    woa, wob, woc, wo = (w.astype(BF16) for w in (w_out_a, w_out_b, w_out_c, w_o))

    col_sga, col_gb, col_u, col_gl, col_gs = W_QKV, W_QKV + W_V, W_QKV + W_V + W_B, W_QKV + W_V + 2 * W_B, W_QKV + W_V + 2 * W_B + W_C
    cw = W_V
    P = TOP_K * T + E * tg
    n_tiles = P // tg

    dp, ap, bpl, cpl, ds, as_, bs, cs = [], [], [], [], [], [], [], []
    for l in range(depth):
        z, ab = _inproj(x, norm1_w[l][None], w_t, l,
                        w_qkv=W_QKV, n_ab=n_ab, w_v=W_V, w_b=W_B, w_c=W_C, tm=tm, tn=tn)

        def hist_rows(st):
            nb, hw, C = st.shape
            return st.reshape(nb // SUBSET, SUBSET, hw, C).transpose(0, 2, 1, 3).reshape(nb * hw, C)
        hist_a, hist_b, hist_c = hist_rows(state_conv_a[l]), hist_rows(state_conv_b[l]), hist_rows(state_conv_c[l])

        xs_, hs_, hi_ = _conv_specs(tr, cw, CA, lambda s: s, n_s_tiles, True)
        qkv, gb = pl.pallas_call(
            functools.partial(_conva_kernel, n_s_tiles=n_s_tiles, width=CA),
            grid=(T // tr, W_QKV // cw),
            in_specs=[xs_, hs_, hi_,
                      pl.BlockSpec((CA, cw), lambda i, s: (0, s)),
                      pl.BlockSpec((tr, LANES), lambda i, s: (i, 0)),
                      pl.BlockSpec((tr, LANES), lambda i, s: (i, 0)),
                      pl.BlockSpec((8, LANES), lambda i, s: (0, 0))],
            out_specs=[pl.BlockSpec((tr, cw), lambda i, s: (i, s)),
                       pl.BlockSpec((tr, LANES), lambda i, s: (i, 0))],
            out_shape=[jax.ShapeDtypeStruct((T, W_QKV), F32), jax.ShapeDtypeStruct((T, LANES), F32)],
            scratch_shapes=[pltpu.VMEM((HALO + tr, cw), F32), pltpu.VMEM((tr, cw), F32)],
            compiler_params=_cp(("arbitrary", "arbitrary")),
            name="conv_a",
        )(z, z, hist_a, conv_a_w[l], ab, mask, adt[l])

        ya, s_s, s_p = _delta(qkv, gb, z, onorm_a_w[l][None], state_delta, l,
                              n_s_steps=n_s_steps, n_batch=B, n_chunks=n_chunks, sga_col=col_sga // cw)

        xs_, hs_, hi_ = _conv_specs(tr, cw, CB, lambda s: col_u // cw, n_s_tiles, False)
        yb = pl.pallas_call(
            functools.partial(_convb_kernel, n_s_tiles=n_s_tiles, width=CB),
            grid=(T // tr,),
            in_specs=[xs_, hs_, hi_,
                      pl.BlockSpec((CB, cw), lambda i: (0, 0)),
                      pl.BlockSpec((tr, cw), lambda i: (i, col_gb // cw))],
            out_specs=pl.BlockSpec((tr, cw), lambda i: (i, 0)),
            out_shape=jax.ShapeDtypeStruct((T, cw), BF16),
            scratch_shapes=[pltpu.VMEM((HALO + tr, cw), F32), pltpu.VMEM((tr, cw), F32)],
            compiler_params=_cp(("arbitrary",)),
            name="conv_b",
        )(z, z, hist_b, conv_b_w[l], z)

        xs_, hs_, hi_ = _conv_specs(tr, cw, CC, lambda s: col_gl // cw, n_s_tiles, False)
        hc_rows = max(HALO + tr, (CC - 1) * SUBSET + CHUNK)
        yc = pl.pallas_call(
            functools.partial(_convc_kernel, n_s_tiles=n_s_tiles, width=CC),
            grid=(T // tr,),
            in_specs=[xs_, hs_, hi_,
                      pl.BlockSpec((CC, cw), lambda i: (0, 0)),
                      pl.BlockSpec((1, cw), lambda i: (0, 0)),
                      pl.BlockSpec((1, cw), lambda i: (0, 0)),
                      pl.BlockSpec((1, cw), lambda i: (0, 0))],
            out_specs=pl.BlockSpec((tr, cw), lambda i: (i, 0)),
            out_shape=jax.ShapeDtypeStruct((T, cw), BF16),
            scratch_shapes=[pltpu.VMEM((hc_rows, cw), F32), pltpu.VMEM((tr, cw), F32)],
            compiler_params=_cp(("arbitrary",)),
            name="conv_c",
        )(z, z, hist_c, conv_c_w[l], conv_c_b[l][None], ln_c_w[l][None], ln_c_b[l][None])

        m = _merge(ya, yb, yc, woa[l], wob[l], woc[l], z, gs_col=col_gs, tm=tm, tn=tn)
        x1 = _oproj(m, wo[l], x, tm=tm, tn=tn)

        ri, rw, cnt = _router(x1, norm2_w[l][None], w_r[l], b_r[l], n_groups=G, per_group=E // G, tr=tr)
        counts = cnt[0, G:G + E].astype(jnp.int32)
        padded = ((counts + tg - 1) // tg) * tg
        ends = jnp.cumsum(padded)
        offs = ends - padded
        route = (ri[:, 0], ri[:, 2], ri[:, 1], ri[:, 3], offs.astype(jnp.int32))
        n_used = (ends[-1] // tg).astype(jnp.int32)
        tile_start = jnp.arange(n_tiles, dtype=jnp.int32) * tg
        tile_e = jnp.sum((tile_start[:, None] >= ends[None, :]).astype(jnp.int32), axis=1)
        last_e = jnp.sum((((n_used - 1) * tg) >= ends).astype(jnp.int32))
        tile_e = jnp.where(jnp.arange(n_tiles) < n_used, tile_e, last_e).astype(jnp.int32)

        pad_start = (((offs + counts) // 8) * 8).astype(jnp.int32)
        n_used = n_used.reshape(1)
        pad_groups = ((ends - pad_start) // 8).astype(jnp.int32)
        xs = _dispatch(route, pad_start, pad_groups, n_used, x1, norm2_w[l][None], n_rows=P, tr=tr, tg=tg)
        eo = _experts(tile_e, n_used, xs, w_gate_e, w_up_e, w_down_e, l, n_tiles=n_tiles, tg=tg)
        x = _combine(route, x1, rw, mask, final_norm_w[None], eo, tr=tr, final=(l == depth - 1))

        def prompt_tail(c0, c1, width):
            return jnp.stack([lax.slice(z, (Ts + (b + 1) * LP - (width - 1), c0), (Ts + (b + 1) * LP, c1))
                              for b in range(B)])

        def sample_hist(st, c0, c1):
            seq = _rows_to_sample(lax.slice(z, (0, c0), (Ts, c1)), L)
            return jnp.concatenate([st.astype(dt_), seq], axis=1)[:, L:]
        dp.append(s_p)
        ap.append(prompt_tail(0, W_QKV, CA))
        bpl.append(prompt_tail(col_u, col_u + W_B, CB))
        cpl.append(prompt_tail(col_gl, col_gl + W_C, CC))
        ds.append(s_s)
        as_.append(sample_hist(state_conv_a[l], 0, W_QKV))
        bs.append(sample_hist(state_conv_b[l], col_u, col_u + W_B))
        cs.append(sample_hist(state_conv_c[l], col_gl, col_gl + W_C))

    y_sample = _rows_to_sample(lax.slice(x, (0, 0), (Ts, D)), L)
    y_prompt = jnp.stack([lax.slice(x, (Ts + b * LP + CHUNK, 0), (Ts + (b + 1) * LP, D)) for b in range(B)])
    return (y_prompt, y_sample, jnp.stack(dp), jnp.stack(ap), jnp.stack(bpl), jnp.stack(cpl),
            jnp.stack(ds), jnp.stack(as_), jnp.stack(bs), jnp.stack(cs))


def _gcd(a, b):
    while b:
        a, b = b, a % b
    return a
```

```python
import functools

import jax
import jax.numpy as jnp
from jax import lax
from jax.experimental import pallas as pl
from jax.experimental.pallas import tpu as pltpu

F32 = jnp.float32
BF16 = jnp.bfloat16
HIGHEST = lax.Precision.HIGHEST

EPS = 1e-6
LN_EPS = 1e-5
N_META = 16
H_A = 8
DK = 128
DV = 128
CHUNK = 64
LANES = 128
DEC_SEQ = 8
SUBSET = CHUNK // DEC_SEQ
HALO = 32
TOP_K = 2
NEG = -1e30
ISSUE_UNROLL = 8

VMEM_LIMIT = 52 * 1024 * 1024


def _cp(dims, vmem=VMEM_LIMIT):
    return pltpu.CompilerParams(dimension_semantics=dims, vmem_limit_bytes=vmem)


def _sigmoid(x):
    return 0.5 * jnp.tanh(0.5 * x) + 0.5


def _silu(x):
    return x * _sigmoid(x)


def _pick_tile(n, target, mult):
    best = None
    for t in range(mult, min(n, target) + 1, mult):
        if n % t == 0:
            best = t
    assert best is not None, (n, target, mult)
    return best


def _inproj_kernel(x_ref, nw_ref, wa_ref, wb_ref, wab_ref, z_ref, ab_ref, h_scr, *, bounds, row_chunk):
    j = pl.program_id(1)
    tm = x_ref.shape[0]

    def proj(w_ref):
        return lax.dot_general(h_scr[...], w_ref[0].astype(BF16), (((1,), (1,)), ((), ())),
                               preferred_element_type=F32)

    @pl.when(j == 0)
    def _():
        def body(r, c):
            rs = pl.ds(pl.multiple_of(r * row_chunk, row_chunk), row_chunk)
            x = x_ref[rs, :]
            h = x * lax.rsqrt(jnp.mean(x * x, axis=-1, keepdims=True) + EPS) * nw_ref[...]
            h_scr[rs, :] = h.astype(BF16)
            return c
        lax.fori_loop(0, tm // row_chunk, body, 0)
        ab_ref[...] = proj(wab_ref)

    b_silu, b_id2, b_mul, b_glu, b_sig = bounds

    def put(v):
        z_ref[...] = v.astype(z_ref.dtype)

    @pl.when((j < b_silu) | ((j >= b_id2) & (j < b_mul)))
    def _():
        put(proj(wa_ref))

    @pl.when((j >= b_silu) & (j < b_id2))
    def _():
        put(_silu(proj(wa_ref)))

    @pl.when((j >= b_mul) & (j < b_glu))
    def _():
        put(proj(wa_ref) * proj(wb_ref))

    @pl.when((j >= b_glu) & (j < b_sig))
    def _():
        put(proj(wa_ref) * _sigmoid(proj(wb_ref)))

    @pl.when(j >= b_sig)
    def _():
        put(_sigmoid(proj(wa_ref)))


def _inproj(x, norm_w, w_t, l, *, w_qkv, n_ab, w_v, w_b, w_c, tm, tn):
    T, D = x.shape
    nq, nv, nb, nc = w_qkv // tn, w_v // tn, w_b // tn, w_c // tn
    r0 = w_qkv + n_ab
    b_silu = nq
    b_id2 = nq + nv
    b_mul = b_id2 + nb
    b_glu = b_mul + nb
    b_sig = b_glu + nc
    n_out = nq + (w_t.shape[1] - r0) // tn - nb - nc
    park_lo = b_glu - nq
    park_hi = b_sig - nq + nb + nc - 1
    assert w_qkv % LANES == 0 and r0 % 8 == 0 and (w_t.shape[1] - r0) % tn == 0

    def wa_map(i, j):
        rest = j - nq + jnp.where(j >= b_glu, nb, 0) + jnp.where(j >= b_sig, nc, 0)
        return (l, pl.multiple_of(jnp.where(j < nq, j * tn, r0 + rest * tn), 8), 0)

    def wb_map(i, j):
        jb = jnp.where(j < b_mul, park_lo,
                       jnp.where(j < b_glu, j - nq + nb,
                                 jnp.where(j < b_sig, j - nq + nb + nc, park_hi)))
        return (l, pl.multiple_of(r0 + jb * tn, 8), 0)

    kern = functools.partial(_inproj_kernel, bounds=(b_silu, b_id2, b_mul, b_glu, b_sig), row_chunk=16)
    return pl.pallas_call(
        kern,
        grid=(T // tm, n_out),
        in_specs=[
            pl.BlockSpec((tm, D), lambda i, j: (i, 0), pipeline_mode=pl.Buffered(1)),
            pl.BlockSpec((1, D), lambda i, j: (0, 0)),
            pl.BlockSpec((pl.Element(1), pl.Element(tn), pl.Element(D)), wa_map),
            pl.BlockSpec((pl.Element(1), pl.Element(tn), pl.Element(D)), wb_map),
            pl.BlockSpec((1, LANES, D), lambda i, j: (l, w_qkv // LANES, 0)),
        ],
        out_specs=[
            pl.BlockSpec((tm, tn), lambda i, j: (i, j)),
            pl.BlockSpec((tm, LANES), lambda i, j: (i, 0)),
        ],
        out_shape=[jax.ShapeDtypeStruct((T, n_out * tn), BF16), jax.ShapeDtypeStruct((T, LANES), F32)],
        scratch_shapes=[pltpu.VMEM((tm, D), BF16)],
        compiler_params=_cp(("arbitrary", "arbitrary")),
        name="inproj",
    )(x, norm_w, w_t, w_t, w_t)


def _conv_taps(xp_scr, base, rows, step, w_ref, width):
    acc = None
    for j in range(width):
        off = base - (width - 1 - j) * step
        term = xp_scr[pl.ds(off, rows), :] * w_ref[j:j + 1, :]
        acc = term if acc is None else acc + term
    return acc


def _conv_taps_by_residue(xp_scr, sh_scr, base, rows, w_ref, width):
    lo = base - (width - 1)
    acc = None
    for rho in range(8):
        offs = [lo + j for j in range(width) if (lo + j) % 8 == rho]
        if not offs:
            continue
        first, last = min(offs) - rho, max(offs) - rho + rows
        if rho:
            sh_scr[first:last, :] = xp_scr[pl.ds(first + rho, last - first), :]
        src = sh_scr if rho else xp_scr
        for off in offs:
            term = src[off - rho:off - rho + rows, :] * w_ref[off - lo:off - lo + 1, :]
            acc = term if acc is None else acc + term
    return acc


def _conv_tile(i, n_s_tiles, x_ref, halo_ref, hist_ref, w_ref, xp_scr, y_scr, width, sh_scr=None):
    tr = x_ref.shape[0]
    hsub = (width - 1) * SUBSET
    nsub = tr // CHUNK

    @pl.when(i < n_s_tiles)
    def _():
        for s in range(nsub):
            xp_scr[0:hsub, :] = hist_ref[s * hsub:(s + 1) * hsub, :]
            xp_scr[hsub:hsub + CHUNK, :] = x_ref[s * CHUNK:(s + 1) * CHUNK, :].astype(F32)
            y_scr[s * CHUNK:(s + 1) * CHUNK, :] = _conv_taps(xp_scr, hsub, CHUNK, SUBSET, w_ref, width)

    @pl.when(i >= n_s_tiles)
    def _():
        halo = halo_ref[...].astype(F32)
        xp_scr[0:HALO, :] = jnp.where(i == n_s_tiles, jnp.zeros_like(halo), halo)
        xp_scr[HALO:HALO + tr, :] = x_ref[...].astype(F32)
        if sh_scr is None:
            y_scr[...] = _conv_taps(xp_scr, HALO, tr, 1, w_ref, width)
        else:
            y_scr[...] = _conv_taps_by_residue(xp_scr, sh_scr, HALO, tr, w_ref, width)


def _softplus(x):
    return jnp.maximum(x, 0.0) + jnp.log(1.0 + jnp.exp(-jnp.abs(x)))


def _conva_kernel(x_ref, halo_ref, hist_ref, w_ref, ab_ref, mask_ref, adt_ref, o_ref, gb_ref, xp_scr, y_scr,
                  *, n_s_tiles, width):
    i = pl.program_id(0)
    sec = pl.program_id(1)
    _conv_tile(i, n_s_tiles, x_ref, halo_ref, hist_ref, w_ref, xp_scr, y_scr, width)

    @pl.when(sec == 0)
    def _():
        ab = ab_ref[...]
        lane = lax.broadcasted_iota(jnp.int32, ab.shape, 1)
        g = -jnp.exp(adt_ref[0:1, :]) * _softplus(ab + adt_ref[1:2, :])
        gb_ref[...] = jnp.where(lane < H_A, g, _sigmoid(ab)) * mask_ref[...]

    scale = jnp.where(sec == 0, DK ** -0.5, 1.0).astype(F32)
    for h in range(H_A):
        hs = slice(h * DK, (h + 1) * DK)
        y = _silu(y_scr[:, hs])
        yn = y * (lax.rsqrt(jnp.sum(y * y, axis=-1, keepdims=True) + EPS) * scale)
        o_ref[:, hs] = jnp.where(sec == 2, y, yn)


def _convb_kernel(x_ref, halo_ref, hist_ref, w_ref, gate_ref, o_ref, xp_scr, y_scr, *, n_s_tiles, width):
    i = pl.program_id(0)
    _conv_tile(i, n_s_tiles, x_ref, halo_ref, hist_ref, w_ref, xp_scr, y_scr, width)
    o_ref[...] = (gate_ref[...].astype(F32) * y_scr[...]).astype(o_ref.dtype)


def _convc_kernel(x_ref, halo_ref, hist_ref, w_ref, cb_ref, lnw_ref, lnb_ref, o_ref, xp_scr, y_scr, sh_scr,
                  *, n_s_tiles, width):
    i = pl.program_id(0)
    _conv_tile(i, n_s_tiles, x_ref, halo_ref, hist_ref, w_ref, xp_scr, y_scr, width, sh_scr)
    y = y_scr[...] + cb_ref[...]
    mu = jnp.mean(y, axis=-1, keepdims=True)
    yc = y - mu
    var = jnp.mean(yc * yc, axis=-1, keepdims=True)
    yn = yc * lax.rsqrt(var + LN_EPS) * lnw_ref[...] + lnb_ref[...]
    o_ref[...] = _silu(yn).astype(o_ref.dtype)


def _conv_specs(tr, cw, width, col_of, n_s_tiles, sec_axis):
    hrows = (tr // CHUNK) * (width - 1) * SUBSET
    hb = tr // HALO
    last_hist = max(n_s_tiles - 1, 0)
    if sec_axis:
        x_map = lambda i, s: (i, col_of(s))
        halo_map = lambda i, s: (jnp.maximum(i * hb - 1, 0), col_of(s))
        hist_map = lambda i, s: (jnp.minimum(i, last_hist), s)
    else:
        x_map = lambda i: (i, col_of(0))
        halo_map = lambda i: (jnp.maximum(i * hb - 1, 0), col_of(0))
        hist_map = lambda i: (jnp.minimum(i, last_hist), 0)
    return [pl.BlockSpec((tr, cw), x_map), pl.BlockSpec((HALO, cw), halo_map), pl.BlockSpec((hrows, cw), hist_map)]


def _dot(a, b):
    return jnp.dot(a.astype(BF16), b.astype(BF16), preferred_element_type=F32)


def _dot_nt(a, b):
    return lax.dot_general(a.astype(BF16), b.astype(BF16), (((1,), (1,)), ((), ())), preferred_element_type=F32)


def _split(x):
    hi = x.astype(BF16)
    return hi, (x - hi.astype(F32)).astype(BF16)


def _mask_dot(mask_bf, x):
    hi, lo = _split(x)
    return (jnp.dot(mask_bf, hi, preferred_element_type=F32)
            + jnp.dot(mask_bf, lo, preferred_element_type=F32))


def _mask_dot_nt(xt, mask_bf):
    hi, lo = _split(xt)
    dn = (((1,), (1,)), ((), ()))
    return (lax.dot_general(hi, mask_bf, dn, preferred_element_type=F32)
            + lax.dot_general(lo, mask_bf, dn, preferred_element_type=F32))


def _bf_mask(m):
    return jnp.where(m, 1.0, 0.0).astype(BF16)


def _chunk_common(q_ref, k_ref, v_ref, gb, gc_all, gr_all, lmask, strict):
    heads = range(H_A)
    hs = [slice(h * DK, (h + 1) * DK) for h in heads]
    q = [q_ref[:, hs[h]] for h in heads]
    k = [k_ref[:, hs[h]] for h in heads]
    v = [v_ref[:, hs[h]] for h in heads]
    gc = [gc_all[:, h:h + 1] for h in heads]
    beta = [gb[:, H_A + h:H_A + h + 1] for h in heads]
    decay = [jnp.exp(jnp.where(lmask, gc[h] - gr_all[h:h + 1, :], NEG)) for h in heads]
    kk = [_dot_nt(k[h], k[h]) for h in heads]
    qk = [_dot_nt(q[h], k[h]) * decay[h] for h in heads]
    m = [jnp.where(strict, beta[h] * kk[h] * decay[h], 0.0) for h in heads]
    eg = [jnp.exp(gc[h]) for h in heads]
    rhs = [jnp.concatenate([beta[h] * v[h], (beta[h] * eg[h]) * k[h]], axis=1) for h in heads]
    return q, k, gc, qk, m, eg, rhs


def _delta_kernel(q_ref, k_ref, v_ref, gb_ref, sga_ref, onw_ref, s0s_ref, ya_ref, ss_ref, sp_ref, s_scr,
                  *, n_s_steps, n_chunks):
    s = pl.program_id(0)
    C = CHUNK
    row = lax.broadcasted_iota(jnp.int32, (C, C), 0)
    col = lax.broadcasted_iota(jnp.int32, (C, C), 1)
    eye = (row == col).astype(F32)
    gb = gb_ref[...]
    gbt = gb.T[0:H_A, :]

    def finish(o, h):
        hs = slice(h * DV, (h + 1) * DV)
        on = o * lax.rsqrt(jnp.mean(o * o, axis=-1, keepdims=True) + EPS) * onw_ref[...]
        ya_ref[:, hs] = (on * sga_ref[:, hs].astype(F32)).astype(ya_ref.dtype)

    heads = range(H_A)

    @pl.when(s < n_s_steps)
    def _():
        same = ((row - col) & (DEC_SEQ - 1)) == 0
        lmask = same & (row >= col)
        strict = same & (row > col)
        lm = _bf_mask(lmask)
        last = _bf_mask(col == (C - DEC_SEQ) + (row & (DEC_SEQ - 1)))
        gc_all = _mask_dot(lm, gb)
        gr_all = _mask_dot_nt(gbt, lm)
        gl_all = _mask_dot(last, gc_all)
        rsub = lax.broadcasted_iota(jnp.int32, (C, 1), 0) & (DEC_SEQ - 1)
        rsub2 = jnp.concatenate([rsub, rsub], axis=0)
        q, k, gc, qk, m, eg, rhs = _chunk_common(q_ref, k_ref, v_ref, gb, gc_all, gr_all, lmask, strict)
        m2 = [_dot(m[h], m[h]) for h in heads]
        b1 = [eye - m[h] for h in heads]
        b2 = [b1[h] + _dot(b1[h], m2[h]) for h in heads]
        m4 = [_dot(m2[h], m2[h]) for h in heads]
        tinv = [b2[h] + _dot(b2[h], m4[h]) for h in heads]
        x = [_dot(tinv[h], rhs[h]) for h in heads]
        lhs = [jnp.concatenate([x[h][:, DV:], q[h]], axis=0) for h in heads]
        sk = []
        for h in heads:
            acc = jnp.zeros((2 * C, DV), F32)
            for i in range(SUBSET):
                acc = acc + jnp.where(rsub2 == i, _dot(lhs[h], s0s_ref[i, h]), 0.0)
            sk.append(acc)
        u = [x[h][:, :DV] - sk[h][:C] for h in heads]
        o = [eg[h] * sk[h][C:] + _dot(qk[h], u[h]) for h in heads]
        kd = [k[h] * jnp.exp(gl_all[:, h:h + 1] - gc[h]) for h in heads]
        for h in heads:
            for i in range(SUBSET):
                kdi = jnp.where(rsub == i, kd[h], 0.0)
                gli = gc_all[C - DEC_SEQ + i:C - DEC_SEQ + i + 1, h:h + 1]
                ss_ref[i, h] = jnp.exp(gli) * s0s_ref[i, h] + _dot(kdi.T, u[h])
        for h in heads:
            finish(o[h], h)

    @pl.when(s >= n_s_steps)
    def _():
        c = (s - n_s_steps) % n_chunks

        @pl.when(c == 0)
        def _():
            s_scr[...] = jnp.zeros_like(s_scr)

        lmask = row >= col
        strict = row > col
        lm = _bf_mask(lmask)
        blk = (row >> 4) == (col >> 4)
        gc_all = _mask_dot(lm, gb)
        gr_all = _mask_dot_nt(gbt, lm)
        q, k, gc, qk, m, eg, rhs = _chunk_common(q_ref, k_ref, v_ref, gb, gc_all, gr_all, lmask, strict)
        nd = [jnp.where(blk, -m[h], 0.0) for h in heads]
        lo = [jnp.where(blk, 0.0, m[h]) for h in heads]
        p2 = [_dot(nd[h], nd[h]) for h in heads]
        a1 = [eye + nd[h] for h in heads]
        t1 = [a1[h] + _dot(a1[h], p2[h]) for h in heads]
        p4 = [_dot(p2[h], p2[h]) for h in heads]
        t2 = [t1[h] + _dot(t1[h], p4[h]) for h in heads]
        p8 = [_dot(p4[h], p4[h]) for h in heads]
        dinv = [t2[h] + _dot(t2[h], p8[h]) for h in heads]
        y = [_dot(dinv[h], rhs[h]) for h in heads]
        a = [_dot(dinv[h], lo[h]) for h in heads]
        w = [y[h] - _dot(a[h], y[h]) for h in heads]
        a2 = [_dot(a[h], a[h]) for h in heads]
        x = [w[h] + _dot(a2[h], w[h]) for h in heads]
        S = [s_scr[h] for h in heads]
        sk = [_dot(jnp.concatenate([x[h][:, DV:], q[h]], axis=0), S[h]) for h in heads]
        u = [x[h][:, :DV] - sk[h][:C] for h in heads]
        o = [eg[h] * sk[h][C:] + _dot(qk[h], u[h]) for h in heads]
        gl = [gc_all[C - 1:C, h:h + 1] for h in heads]
        kd = [k[h] * jnp.exp(gl[h] - gc[h]) for h in heads]
        for h in heads:
            s_scr[h] = jnp.exp(gl[h]) * S[h] + _dot(kd[h].T, u[h])
        for h in heads:
            finish(o[h], h)

        @pl.when(c == n_chunks - 1)
        def _():
            sp_ref[0] = s_scr[...]


def _delta(qkv, gb, z, onorm_w, state_all, l, *, n_s_steps, n_batch, n_chunks, sga_col):
    T = qkv.shape[0]
    W = H_A * DK
    n_steps = n_s_steps + n_batch * n_chunks
    last_s = max(n_s_steps - 1, 0)
    kern = functools.partial(_delta_kernel, n_s_steps=n_s_steps, n_chunks=n_chunks)
    return pl.pallas_call(
        kern,
        grid=(n_steps,),
        in_specs=[
            pl.BlockSpec((CHUNK, W), lambda s: (s, 0)),
            pl.BlockSpec((CHUNK, W), lambda s: (s, 1)),
            pl.BlockSpec((CHUNK, W), lambda s: (s, 2)),
            pl.BlockSpec((CHUNK, LANES), lambda s: (s, 0)),
            pl.BlockSpec((CHUNK, W), lambda s: (s, sga_col)),
            pl.BlockSpec((1, DV), lambda s: (0, 0)),
            pl.BlockSpec((None, SUBSET, H_A, DK, DV), lambda s: (l, jnp.minimum(s, last_s), 0, 0, 0)),
        ],
        out_specs=[
            pl.BlockSpec((CHUNK, W), lambda s: (s, 0)),
            pl.BlockSpec((SUBSET, H_A, DK, DV), lambda s: (jnp.minimum(s, last_s), 0, 0, 0)),
            pl.BlockSpec((1, H_A, DK, DV), lambda s: (jnp.maximum(s - n_s_steps, 0) // n_chunks, 0, 0, 0)),
        ],
        out_shape=[
            jax.ShapeDtypeStruct((T, W), BF16),
            jax.ShapeDtypeStruct(state_all.shape[1:], F32),
            jax.ShapeDtypeStruct((n_batch, H_A, DK, DV), F32),
        ],
        scratch_shapes=[pltpu.VMEM((H_A, DK, DV), F32)],
        compiler_params=_cp(("arbitrary",)),
        name="delta",
    )(qkv, qkv, qkv, gb, z, onorm_w, state_all)


def _merge_kernel(ya_ref, yb_ref, yc_ref, wa_ref, wb_ref, wc_ref, g0_ref, g1_ref, g2_ref, m_ref):
    m = g0_ref[...].astype(F32) * jnp.dot(ya_ref[...], wa_ref[...], preferred_element_type=F32)
    m = m + g1_ref[...].astype(F32) * jnp.dot(yb_ref[...], wb_ref[...], preferred_element_type=F32)
    m = m + g2_ref[...].astype(F32) * jnp.dot(yc_ref[...], wc_ref[...], preferred_element_type=F32)
    m_ref[...] = m.astype(m_ref.dtype)


def _merge(ya, yb, yc, wa, wb, wc, z, *, gs_col, tm, tn):
    T, K = ya.shape
    D = wa.shape[1]
    nb = D // tn
    g0 = gs_col // tn
    act = pl.BlockSpec((tm, K), lambda i, j: (i, 0))
    wsp = pl.BlockSpec((K, tn), lambda i, j: (0, j))
    return pl.pallas_call(
        _merge_kernel,
        grid=(T // tm, nb),
        in_specs=[act, act, act, wsp, wsp, wsp,
                  pl.BlockSpec((tm, tn), lambda i, j: (i, g0 + j)),
                  pl.BlockSpec((tm, tn), lambda i, j: (i, g0 + nb + j)),
                  pl.BlockSpec((tm, tn), lambda i, j: (i, g0 + 2 * nb + j))],
        out_specs=pl.BlockSpec((tm, tn), lambda i, j: (i, j)),
        out_shape=jax.ShapeDtypeStruct((T, D), BF16),
        compiler_params=_cp(("arbitrary", "arbitrary")),
        name="merge",
    )(ya, yb, yc, wa, wb, wc, z, z, z)


def _oproj_kernel(m_ref, w_ref, x_ref, o_ref):
    o_ref[...] = x_ref[...] + jnp.dot(m_ref[...], w_ref[...], preferred_element_type=F32)


def _oproj(m, w_o, x, *, tm, tn):
    T, D = x.shape
    return pl.pallas_call(
        _oproj_kernel,
        grid=(T // tm, D // tn),
        in_specs=[pl.BlockSpec((tm, D), lambda i, j: (i, 0)),
                  pl.BlockSpec((D, tn), lambda i, j: (0, j)),
                  pl.BlockSpec((tm, tn), lambda i, j: (i, j))],
        out_specs=pl.BlockSpec((tm, tn), lambda i, j: (i, j)),
        out_shape=jax.ShapeDtypeStruct((T, D), F32),
        compiler_params=_cp(("arbitrary", "arbitrary")),
        name="oproj",
    )(m, w_o, x)


def _dot_hi(a, b):
    return jnp.dot(a, b, preferred_element_type=F32, precision=HIGHEST)


def _router_kernel(x_ref, nw_ref, wr_ref, br_ref, ri_ref, rw_ref, cnt_ref, carry_scr, *, n_groups, per_group):
    i = pl.program_id(0)
    tr = x_ref.shape[0]

    @pl.when(i == 0)
    def _():
        carry_scr[...] = jnp.zeros_like(carry_scr)

    x = x_ref[...]
    h = x * lax.rsqrt(jnp.mean(x * x, axis=-1, keepdims=True) + EPS) * nw_ref[...]
    logits = _dot_hi(h, wr_ref[...]) + br_ref[...]
    lane = lax.broadcasted_iota(jnp.int32, logits.shape, 1)
    big = jnp.int32(1 << 20)

    def argmax_first(vals):
        m = jnp.max(vals, axis=-1, keepdims=True)
        idx = jnp.min(jnp.where(vals == m, lane, big), axis=-1, keepdims=True)
        return m, idx

    is_g = lane < n_groups
    mg, gsel = argmax_first(jnp.where(is_g, logits, NEG))
    pg = 1.0 / jnp.sum(jnp.where(is_g, jnp.exp(logits - mg), 0.0), axis=-1, keepdims=True)
    lo = n_groups + gsel * per_group
    in_group = (lane >= lo) & (lane < lo + per_group)
    le = jnp.where(in_group, logits, NEG)
    m1, i1 = argmax_first(le)
    m2, i2 = argmax_first(jnp.where(lane == i1, NEG, le))
    e21 = jnp.exp(m2 - m1)
    w1 = pg / (1.0 + e21)
    w2 = w1 * e21

    oh = ((lane == i1) | (lane == i2)).astype(F32)
    r_i = lax.broadcasted_iota(jnp.int32, (tr, tr), 0)
    c_i = lax.broadcasted_iota(jnp.int32, (tr, tr), 1)
    before = _dot((r_i > c_i).astype(F32), oh) + carry_scr[...]
    rank1 = jnp.sum(jnp.where(lane == i1, before, 0.0), axis=-1, keepdims=True)
    rank2 = jnp.sum(jnp.where(lane == i2, before, 0.0), axis=-1, keepdims=True)
    carry_scr[...] = carry_scr[...] + jnp.sum(oh, axis=0, keepdims=True)
    cnt_ref[...] = jnp.broadcast_to(carry_scr[...], cnt_ref.shape)

    ri = jnp.where(lane == 0, i1 - n_groups,
                   jnp.where(lane == 1, i2 - n_groups,
                             jnp.where(lane == 2, rank1.astype(jnp.int32),
                                       jnp.where(lane == 3, rank2.astype(jnp.int32), 0))))
    ri_ref[...] = ri
    rw_ref[...] = jnp.where(lane == 0, w1, jnp.where(lane == 1, w2, 0.0))


def _router(x, norm_w, w_r, b_r, *, n_groups, per_group, tr):
    T, D = x.shape
    kern = functools.partial(_router_kernel, n_groups=n_groups, per_group=per_group)
    return pl.pallas_call(
        kern,
        grid=(T // tr,),
        in_specs=[pl.BlockSpec((tr, D), lambda i: (i, 0)),
                  pl.BlockSpec((1, D), lambda i: (0, 0)),
                  pl.BlockSpec((D, LANES), lambda i: (0, 0)),
                  pl.BlockSpec((1, LANES), lambda i: (0, 0))],
        out_specs=[pl.BlockSpec((tr, LANES), lambda i: (i, 0)),
                   pl.BlockSpec((tr, LANES), lambda i: (i, 0)),
                   pl.BlockSpec((8, LANES), lambda i: (0, 0))],
        out_shape=[jax.ShapeDtypeStruct((T, LANES), jnp.int32),
                   jax.ShapeDtypeStruct((T, LANES), F32),
                   jax.ShapeDtypeStruct((8, LANES), F32)],
        scratch_shapes=[pltpu.VMEM((1, LANES), F32)],
        compiler_params=_cp(("arbitrary",)),
        name="router",
    )(x, norm_w, w_r, b_r)


def _row_copy(src, src_row, dst, dst_row, sem):
    return pltpu.make_async_copy(src.at[pl.ds(src_row, 1), :], dst.at[pl.ds(dst_row, 1), :], sem)


def _dispatch_kernel(e1_ref, r1_ref, e2_ref, r2_ref, off_ref, pad_ref, npad_ref, nu_ref, x_ref, nw_ref, xs_ref,
                     h_scr, z_scr, sem):
    i = pl.program_id(0)
    tr = x_ref.shape[0]
    n_experts = pad_ref.shape[0]
    tg = z_scr.shape[0]

    @pl.when(i == 0)
    def _():
        z_scr[...] = jnp.zeros_like(z_scr)
        n_all = xs_ref.shape[0] // tg

        def fill(start, rows):
            dst = xs_ref.at[pl.ds(pl.multiple_of(start, 8), rows), :]
            return pltpu.make_async_copy(z_scr.at[pl.ds(0, rows), :], dst, sem.at[2])

        def pad_fills(act):
            def per_expert(e, c):
                def piece(k, c2):
                    act(fill(pad_ref[e] + k * 8, 8))
                    return c2
                return lax.fori_loop(0, npad_ref[e], piece, c)
            lax.fori_loop(0, n_experts, per_expert, 0)

        def tail_fills(act):
            def tile(k, c):
                act(fill(k * tg, tg))
                return c
            lax.fori_loop(nu_ref[0], n_all, tile, 0)

        pad_fills(lambda cp: cp.start())
        tail_fills(lambda cp: cp.start())
        pad_fills(lambda cp: cp.wait())
        tail_fills(lambda cp: cp.wait())

    x = x_ref[...]
    h_scr[...] = x * lax.rsqrt(jnp.mean(x * x, axis=-1, keepdims=True) + EPS) * nw_ref[...]
    base = i * tr

    def issue(r, c):
        t = base + r
        _row_copy(h_scr, r, xs_ref, off_ref[e1_ref[t]] + r1_ref[t], sem.at[0]).start()
        _row_copy(h_scr, r, xs_ref, off_ref[e2_ref[t]] + r2_ref[t], sem.at[1]).start()
        return c
    lax.fori_loop(0, tr, issue, 0, unroll=ISSUE_UNROLL)
    pltpu.make_async_copy(h_scr, xs_ref.at[pl.ds(0, tr), :], sem.at[0]).wait()
    pltpu.make_async_copy(h_scr, xs_ref.at[pl.ds(0, tr), :], sem.at[1]).wait()


def _dispatch(route, pad_start, pad_groups, n_used, x, norm_w, *, n_rows, tr, tg):
    T, D = x.shape
    n_pre = len(route) + 3
    grid_spec = pltpu.PrefetchScalarGridSpec(
        num_scalar_prefetch=n_pre,
        grid=(T // tr,),
        in_specs=[pl.BlockSpec((tr, D), lambda i, *_: (i, 0)),
                  pl.BlockSpec((1, D), lambda i, *_: (0, 0))],
        out_specs=pl.BlockSpec(memory_space=pl.ANY),
        scratch_shapes=[pltpu.VMEM((tr, D), F32), pltpu.VMEM((tg, D), F32), pltpu.SemaphoreType.DMA((3,))],
    )
    return pl.pallas_call(
        _dispatch_kernel,
        grid_spec=grid_spec,
        out_shape=jax.ShapeDtypeStruct((n_rows, D), F32),
        compiler_params=_cp(("arbitrary",)),
        name="dispatch",
    )(*route, pad_start, pad_groups, n_used, x, norm_w)


def _expert_kernel(te_ref, nu_ref, xs_ref, wg_ref, wu_ref, wd_ref, o_ref, wg_scr, wu_scr, wd_scr):
    i = pl.program_id(0)
    prev = te_ref[jnp.maximum(i - 1, 0)]

    @pl.when((i == 0) | (te_ref[i] != prev))
    def _():
        wg_scr[...] = wg_ref[...].astype(BF16)
        wu_scr[...] = wu_ref[...].astype(BF16)
        wd_scr[...] = wd_ref[...].astype(BF16)

    @pl.when(i < nu_ref[0])
    def _():
        h = xs_ref[...].astype(BF16)
        g = jnp.dot(h, wg_scr[...], preferred_element_type=F32)
        u = jnp.dot(h, wu_scr[...], preferred_element_type=F32)
        a = (_silu(g) * u).astype(BF16)
        o_ref[...] = jnp.dot(a, wd_scr[...], preferred_element_type=F32)

    @pl.when(i >= nu_ref[0])
    def _():
        o_ref[...] = jnp.zeros_like(o_ref)


def _experts(tile_e, n_used, xs, w_gate, w_up, w_down, l, *, n_tiles, tg):
    D = xs.shape[1]
    F = w_gate.shape[-1]

    def xs_map(i, te, nu):
        return (jnp.maximum(jnp.minimum(i, nu[0] - 1), 0), 0)

    grid_spec = pltpu.PrefetchScalarGridSpec(
        num_scalar_prefetch=2,
        grid=(n_tiles,),
        in_specs=[pl.BlockSpec((tg, D), xs_map),
                  pl.BlockSpec((None, None, D, F), lambda i, te, nu: (l, te[i], 0, 0)),
                  pl.BlockSpec((None, None, D, F), lambda i, te, nu: (l, te[i], 0, 0)),
                  pl.BlockSpec((None, None, F, D), lambda i, te, nu: (l, te[i], 0, 0))],
        out_specs=pl.BlockSpec((tg, D), lambda i, te, nu: (i, 0)),
        scratch_shapes=[pltpu.VMEM((D, F), BF16), pltpu.VMEM((D, F), BF16), pltpu.VMEM((F, D), BF16)],
    )
    return pl.pallas_call(
        _expert_kernel,
        grid_spec=grid_spec,
        out_shape=jax.ShapeDtypeStruct((n_tiles * tg, D), F32),
        compiler_params=_cp(("arbitrary",)),
        name="experts",
    )(tile_e, n_used, xs, w_gate, w_up, w_down)


def _combine_kernel(e1_ref, r1_ref, e2_ref, r2_ref, off_ref, x_ref, rw_ref, mask_ref, fw_ref, eo_ref, out_ref,
                    b1_scr, b2_scr, sem, *, final):
    i = pl.program_id(0)
    tr = x_ref.shape[0]
    base = i * tr

    def issue(r, c):
        t = base + r
        _row_copy(eo_ref, off_ref[e1_ref[t]] + r1_ref[t], b1_scr, r, sem.at[0]).start()
        _row_copy(eo_ref, off_ref[e2_ref[t]] + r2_ref[t], b2_scr, r, sem.at[1]).start()
        return c
    lax.fori_loop(0, tr, issue, 0, unroll=ISSUE_UNROLL)
    pltpu.make_async_copy(eo_ref.at[pl.ds(0, tr), :], b1_scr, sem.at[0]).wait()
    pltpu.make_async_copy(eo_ref.at[pl.ds(0, tr), :], b2_scr, sem.at[1]).wait()
    rw = rw_ref[...]
    x2 = (x_ref[...] + rw[:, 0:1] * b1_scr[...] + rw[:, 1:2] * b2_scr[...]) * mask_ref[:, 0:1]
    if final:
        out_ref[...] = x2 * lax.rsqrt(jnp.mean(x2 * x2, axis=-1, keepdims=True) + EPS) * fw_ref[...]
    else:
        out_ref[...] = x2


def _combine(route, x, rw, mask, final_w, eo, *, tr, final):
    T, D = x.shape
    row = lambda w: pl.BlockSpec((tr, w), lambda i, *_: (i, 0))
    grid_spec = pltpu.PrefetchScalarGridSpec(
        num_scalar_prefetch=len(route),
        grid=(T // tr,),
        in_specs=[row(D), row(LANES), row(LANES),
                  pl.BlockSpec((1, D), lambda i, *_: (0, 0)),
                  pl.BlockSpec(memory_space=pl.ANY)],
        out_specs=row(D),
        scratch_shapes=[pltpu.VMEM((tr, D), F32), pltpu.VMEM((tr, D), F32), pltpu.SemaphoreType.DMA((2,))],
    )
    return pl.pallas_call(
        functools.partial(_combine_kernel, final=final),
        grid_spec=grid_spec,
        out_shape=jax.ShapeDtypeStruct((T, D), F32),
        compiler_params=_cp(("arbitrary",)),
        name="combine",
    )(*route, x, rw, mask, final_w, eo)


def _sample_to_rows(a):
    nb, L, C = a.shape
    return a.reshape(nb // SUBSET, SUBSET, L, C).transpose(0, 2, 1, 3).reshape(nb * L, C)


def _rows_to_sample(r, L):
    n, C = r.shape
    nb = n // L
    return r.reshape(nb // SUBSET, L, SUBSET, C).transpose(0, 2, 1, 3).reshape(nb, L, C)


def kernel(x_prompt, x_sample, state_delta, state_conv_a, state_conv_b, state_conv_c, meta_tokens, norm1_w, w_in, conv_a_w, a_log, dt_bias, onorm_a_w, w_out_a, conv_b_w, w_out_b, conv_c_w, conv_c_b, ln_c_w, ln_c_b, w_out_c, w_o, norm2_w, w_rg, b_rg, w_re, b_re, w_gate_e, w_up_e, w_down_e, final_norm_w):
    B, SEQ, D = x_prompt.shape
    NB, L, _ = x_sample.shape
    depth = w_in.shape[0]
    W_QKV = conv_a_w.shape[-1]
    W_V = H_A * DV
    W_B = conv_b_w.shape[-1]
    W_C = conv_c_w.shape[-1]
    CA, CB, CC = conv_a_w.shape[1], conv_b_w.shape[1], conv_c_w.shape[1]
    G = w_rg.shape[-1]
    E = w_re.shape[-1]
    assert L == DEC_SEQ and NB % SUBSET == 0 and SEQ % CHUNK == 0 and W_QKV == 3 * W_V
    assert W_V == W_B == W_C and max(CA, CB, CC) - 1 <= min(HALO, CHUNK - N_META)
    assert G + E <= LANES and D % LANES == 0

    LP = CHUNK + SEQ
    lead = CHUNK - N_META
    Ts, Tp = NB * L, B * LP
    T = Ts + Tp
    n_chunks = LP // CHUNK
    n_s_steps = Ts // CHUNK

    tr = _pick_tile(_gcd(Ts, Tp), 256, CHUNK)
    tm = _pick_tile(T, 1184, 16)
    tn = 512
    tg = 256
    n_s_tiles = Ts // tr

    dt_ = x_prompt.dtype
    lead_rows = jnp.concatenate([jnp.zeros((lead, D), dt_), meta_tokens.astype(dt_)], axis=0)
    xp = jnp.concatenate([jnp.broadcast_to(lead_rows[None], (B, CHUNK, D)), x_prompt], axis=1).reshape(Tp, D)
    x = jnp.concatenate([_sample_to_rows(x_sample), xp], axis=0)
    pos_in_seq = jnp.arange(Tp, dtype=jnp.int32) % LP
    real = jnp.concatenate([jnp.ones((Ts,), F32), (pos_in_seq >= lead).astype(F32)])
    mask = jnp.broadcast_to(real[:, None], (T, LANES))

    n_ab = 2 * H_A
    w_t = jnp.swapaxes(w_in, 1, 2)
    w_r = jnp.pad(jnp.concatenate([w_rg, w_re], axis=2), ((0, 0), (0, 0), (0, LANES - G - E)))
    b_r = jnp.pad(jnp.concatenate([b_rg, b_re], axis=1), ((0, 0), (0, LANES - G - E)))[:, None, :]
    adt = jnp.pad(jnp.stack([a_log, dt_bias], axis=1), ((0, 0), (0, 0), (0, LANES - H_A)))
    adt = jnp.pad(adt, ((0, 0), (0, 6), (0, 0)))
    woa, wob, woc, wo = (w.astype(BF16) for w in (w_out_a, w_out_b, w_out_c, w_o))

    col_sga, col_gb, col_u, col_gl, col_gs = W_QKV, W_QKV + W_V, W_QKV + W_V + W_B, W_QKV + W_V + 2 * W_B, W_QKV + W_V + 2 * W_B + W_C
    cw = W_V
    P = TOP_K * T + E * tg
    n_tiles = P // tg

    dp, ap, bpl, cpl, ds, as_, bs, cs = [], [], [], [], [], [], [], []
    for l in range(depth):
        z, ab = _inproj(x, norm1_w[l][None], w_t, l,
                        w_qkv=W_QKV, n_ab=n_ab, w_v=W_V, w_b=W_B, w_c=W_C, tm=tm, tn=tn)

        def hist_rows(st):
            nb, hw, C = st.shape
            return st.reshape(nb // SUBSET, SUBSET, hw, C).transpose(0, 2, 1, 3).reshape(nb * hw, C)
        hist_a, hist_b, hist_c = hist_rows(state_conv_a[l]), hist_rows(state_conv_b[l]), hist_rows(state_conv_c[l])

        xs_, hs_, hi_ = _conv_specs(tr, cw, CA, lambda s: s, n_s_tiles, True)
        qkv, gb = pl.pallas_call(
            functools.partial(_conva_kernel, n_s_tiles=n_s_tiles, width=CA),
            grid=(T // tr, W_QKV // cw),
            in_specs=[xs_, hs_, hi_,
                      pl.BlockSpec((CA, cw), lambda i, s: (0, s)),
                      pl.BlockSpec((tr, LANES), lambda i, s: (i, 0)),
                      pl.BlockSpec((tr, LANES), lambda i, s: (i, 0)),
                      pl.BlockSpec((8, LANES), lambda i, s: (0, 0))],
            out_specs=[pl.BlockSpec((tr, cw), lambda i, s: (i, s)),
                       pl.BlockSpec((tr, LANES), lambda i, s: (i, 0))],
            out_shape=[jax.ShapeDtypeStruct((T, W_QKV), F32), jax.ShapeDtypeStruct((T, LANES), F32)],
            scratch_shapes=[pltpu.VMEM((HALO + tr, cw), F32), pltpu.VMEM((tr, cw), F32)],
            compiler_params=_cp(("arbitrary", "arbitrary")),
            name="conv_a",
        )(z, z, hist_a, conv_a_w[l], ab, mask, adt[l])

        ya, s_s, s_p = _delta(qkv, gb, z, onorm_a_w[l][None], state_delta, l,
                              n_s_steps=n_s_steps, n_batch=B, n_chunks=n_chunks, sga_col=col_sga // cw)

        xs_, hs_, hi_ = _conv_specs(tr, cw, CB, lambda s: col_u // cw, n_s_tiles, False)
        yb = pl.pallas_call(
            functools.partial(_convb_kernel, n_s_tiles=n_s_tiles, width=CB),
            grid=(T // tr,),
            in_specs=[xs_, hs_, hi_,
                      pl.BlockSpec((CB, cw), lambda i: (0, 0)),
                      pl.BlockSpec((tr, cw), lambda i: (i, col_gb // cw))],
            out_specs=pl.BlockSpec((tr, cw), lambda i: (i, 0)),
            out_shape=jax.ShapeDtypeStruct((T, cw), BF16),
            scratch_shapes=[pltpu.VMEM((HALO + tr, cw), F32), pltpu.VMEM((tr, cw), F32)],
            compiler_params=_cp(("arbitrary",)),
            name="conv_b",
        )(z, z, hist_b, conv_b_w[l], z)

        xs_, hs_, hi_ = _conv_specs(tr, cw, CC, lambda s: col_gl // cw, n_s_tiles, False)
        hc_rows = max(HALO + tr, (CC - 1) * SUBSET + CHUNK)
        yc = pl.pallas_call(
            functools.partial(_convc_kernel, n_s_tiles=n_s_tiles, width=CC),
            grid=(T // tr,),
            in_specs=[xs_, hs_, hi_,
                      pl.BlockSpec((CC, cw), lambda i: (0, 0)),
                      pl.BlockSpec((1, cw), lambda i: (0, 0)),
                      pl.BlockSpec((1, cw), lambda i: (0, 0)),
                      pl.BlockSpec((1, cw), lambda i: (0, 0))],
            out_specs=pl.BlockSpec((tr, cw), lambda i: (i, 0)),
            out_shape=jax.ShapeDtypeStruct((T, cw), BF16),
            scratch_shapes=[pltpu.VMEM((hc_rows, cw), F32), pltpu.VMEM((tr, cw), F32),
                            pltpu.VMEM((HALO + tr, cw), F32)],
            compiler_params=_cp(("arbitrary",)),
            name="conv_c",
        )(z, z, hist_c, conv_c_w[l], conv_c_b[l][None], ln_c_w[l][None], ln_c_b[l][None])

        m = _merge(ya, yb, yc, woa[l], wob[l], woc[l], z, gs_col=col_gs, tm=tm, tn=tn)
        x1 = _oproj(m, wo[l], x, tm=tm, tn=tn)

        ri, rw, cnt = _router(x1, norm2_w[l][None], w_r[l], b_r[l], n_groups=G, per_group=E // G, tr=tr)
        counts = cnt[0, G:G + E].astype(jnp.int32)
        padded = ((counts + tg - 1) // tg) * tg
        ends = jnp.cumsum(padded)
        offs = ends - padded
        route = (ri[:, 0], ri[:, 2], ri[:, 1], ri[:, 3], offs.astype(jnp.int32))
        n_used = (ends[-1] // tg).astype(jnp.int32)
        tile_start = jnp.arange(n_tiles, dtype=jnp.int32) * tg
        tile_e = jnp.sum((tile_start[:, None] >= ends[None, :]).astype(jnp.int32), axis=1)
        last_e = jnp.sum((((n_used - 1) * tg) >= ends).astype(jnp.int32))
        tile_e = jnp.where(jnp.arange(n_tiles) < n_used, tile_e, last_e).astype(jnp.int32)

        pad_start = (((offs + counts) // 8) * 8).astype(jnp.int32)
        n_used = n_used.reshape(1)
        pad_groups = ((ends - pad_start) // 8).astype(jnp.int32)
        xs = _dispatch(route, pad_start, pad_groups, n_used, x1, norm2_w[l][None], n_rows=P, tr=tr, tg=tg)
        eo = _experts(tile_e, n_used, xs, w_gate_e, w_up_e, w_down_e, l, n_tiles=n_tiles, tg=tg)
        x = _combine(route, x1, rw, mask, final_norm_w[None], eo, tr=tr, final=(l == depth - 1))

        def prompt_tail(c0, c1, width):
            return jnp.stack([lax.slice(z, (Ts + (b + 1) * LP - (width - 1), c0), (Ts + (b + 1) * LP, c1))
                              for b in range(B)]).astype(dt_)

        def sample_hist(st, c0, c1):
            seq = _rows_to_sample(lax.slice(z, (0, c0), (Ts, c1)), L).astype(dt_)
            return jnp.concatenate([st.astype(dt_), seq], axis=1)[:, L:]
        dp.append(s_p)
        ap.append(prompt_tail(0, W_QKV, CA))
        bpl.append(prompt_tail(col_u, col_u + W_B, CB))
        cpl.append(prompt_tail(col_gl, col_gl + W_C, CC))
        ds.append(s_s)
        as_.append(sample_hist(state_conv_a[l], 0, W_QKV))
        bs.append(sample_hist(state_conv_b[l], col_u, col_u + W_B))
        cs.append(sample_hist(state_conv_c[l], col_gl, col_gl + W_C))

    y_sample = _rows_to_sample(lax.slice(x, (0, 0), (Ts, D)), L)
    y_prompt = jnp.stack([lax.slice(x, (Ts + b * LP + CHUNK, 0), (Ts + (b + 1) * LP, D)) for b in range(B)])
    return (y_prompt, y_sample, jnp.stack(dp), jnp.stack(ap), jnp.stack(bpl), jnp.stack(cpl),
            jnp.stack(ds), jnp.stack(as_), jnp.stack(bs), jnp.stack(cs))


def _gcd(a, b):
    while b:
        a, b = b, a % b
    return a
```

```python
import functools

import jax
import jax.numpy as jnp
from jax import lax
from jax.experimental import pallas as pl
from jax.experimental.pallas import tpu as pltpu

F32 = jnp.float32
BF16 = jnp.bfloat16

EPS = 1e-6
LN_EPS = 1e-5
N_META = 16
H_A = 8
DK = 128
DV = 128
CHUNK = 64
LANES = 128
DEC_SEQ = 8
SUBSET = CHUNK // DEC_SEQ
HALO = 32
TOP_K = 2
NEG = -1e30
ISSUE_UNROLL = 8

VMEM_LIMIT = 52 * 1024 * 1024


def _cp(dims, vmem=VMEM_LIMIT):
    return pltpu.CompilerParams(dimension_semantics=dims, vmem_limit_bytes=vmem)


def _sigmoid(x):
    return 0.5 * jnp.tanh(0.5 * x) + 0.5


def _silu(x):
    return x * _sigmoid(x)


def _pick_tile(n, target, mult):
    best = None
    for t in range(mult, min(n, target) + 1, mult):
        if n % t == 0:
            best = t
    assert best is not None, (n, target, mult)
    return best


def _inproj_kernel(x_ref, nw_ref, wa_ref, wb_ref, wab_ref, z_ref, ab_ref, h_scr, *, bounds, row_chunk):
    j = pl.program_id(1)
    tm = x_ref.shape[0]

    def proj(w_ref):
        return lax.dot_general(h_scr[...], w_ref[0].astype(BF16), (((1,), (1,)), ((), ())),
                               preferred_element_type=F32)

    @pl.when(j == 0)
    def _():
        def body(r, c):
            rs = pl.ds(pl.multiple_of(r * row_chunk, row_chunk), row_chunk)
            x = x_ref[rs, :]
            h = x * lax.rsqrt(jnp.mean(x * x, axis=-1, keepdims=True) + EPS) * nw_ref[...]
            h_scr[rs, :] = h.astype(BF16)
            return c
        lax.fori_loop(0, tm // row_chunk, body, 0)
        ab_ref[...] = proj(wab_ref)

    b_silu, b_id2, b_mul, b_glu, b_sig = bounds

    def put(v):
        z_ref[...] = v.astype(z_ref.dtype)

    @pl.when((j < b_silu) | ((j >= b_id2) & (j < b_mul)))
    def _():
        put(proj(wa_ref))

    @pl.when((j >= b_silu) & (j < b_id2))
    def _():
        put(_silu(proj(wa_ref)))

    @pl.when((j >= b_mul) & (j < b_glu))
    def _():
        put(proj(wa_ref) * proj(wb_ref))

    @pl.when((j >= b_glu) & (j < b_sig))
    def _():
        put(proj(wa_ref) * _sigmoid(proj(wb_ref)))

    @pl.when(j >= b_sig)
    def _():
        put(_sigmoid(proj(wa_ref)))


def _inproj(x, norm_w, w_t, l, *, w_qkv, n_ab, w_v, w_b, w_c, tm, tn):
    T, D = x.shape
    nq, nv, nb, nc = w_qkv // tn, w_v // tn, w_b // tn, w_c // tn
    r0 = w_qkv + n_ab
    b_silu = nq
    b_id2 = nq + nv
    b_mul = b_id2 + nb
    b_glu = b_mul + nb
    b_sig = b_glu + nc
    n_out = nq + (w_t.shape[1] - r0) // tn - nb - nc
    park_lo = b_glu - nq
    park_hi = b_sig - nq + nb + nc - 1
    assert w_qkv % LANES == 0 and r0 % 8 == 0 and (w_t.shape[1] - r0) % tn == 0

    def wa_map(i, j):
        rest = j - nq + jnp.where(j >= b_glu, nb, 0) + jnp.where(j >= b_sig, nc, 0)
        return (l, pl.multiple_of(jnp.where(j < nq, j * tn, r0 + rest * tn), 8), 0)

    def wb_map(i, j):
        jb = jnp.where(j < b_mul, park_lo,
                       jnp.where(j < b_glu, j - nq + nb,
                                 jnp.where(j < b_sig, j - nq + nb + nc, park_hi)))
        return (l, pl.multiple_of(r0 + jb * tn, 8), 0)

    kern = functools.partial(_inproj_kernel, bounds=(b_silu, b_id2, b_mul, b_glu, b_sig), row_chunk=16)
    return pl.pallas_call(
        kern,
        grid=(T // tm, n_out),
        in_specs=[
            pl.BlockSpec((tm, D), lambda i, j: (i, 0), pipeline_mode=pl.Buffered(1)),
            pl.BlockSpec((1, D), lambda i, j: (0, 0)),
            pl.BlockSpec((pl.Element(1), pl.Element(tn), pl.Element(D)), wa_map),
            pl.BlockSpec((pl.Element(1), pl.Element(tn), pl.Element(D)), wb_map),
            pl.BlockSpec((1, LANES, D), lambda i, j: (l, w_qkv // LANES, 0)),
        ],
        out_specs=[
            pl.BlockSpec((tm, tn), lambda i, j: (i, j)),
            pl.BlockSpec((tm, LANES), lambda i, j: (i, 0)),
        ],
        out_shape=[jax.ShapeDtypeStruct((T, n_out * tn), BF16), jax.ShapeDtypeStruct((T, LANES), F32)],
        scratch_shapes=[pltpu.VMEM((tm, D), BF16)],
        compiler_params=_cp(("arbitrary", "arbitrary")),
        name="inproj",
    )(x, norm_w, w_t, w_t, w_t)


def _conv_taps(xp_scr, base, rows, step, w_ref, width):
    acc = None
    for j in range(width):
        off = base - (width - 1 - j) * step
        term = xp_scr[pl.ds(off, rows), :] * w_ref[j:j + 1, :]
        acc = term if acc is None else acc + term
    return acc


def _conv_taps_by_residue(xp_scr, sh_scr, base, rows, w_ref, width):
    lo = base - (width - 1)
    acc = None
    for rho in range(8):
        offs = [lo + j for j in range(width) if (lo + j) % 8 == rho]
        if not offs:
            continue
        first, last = min(offs) - rho, max(offs) - rho + rows
        if rho:
            sh_scr[first:last, :] = xp_scr[pl.ds(first + rho, last - first), :]
        src = sh_scr if rho else xp_scr
        for off in offs:
            term = src[off - rho:off - rho + rows, :] * w_ref[off - lo:off - lo + 1, :]
            acc = term if acc is None else acc + term
    return acc


def _conv_tile(i, n_s_tiles, x_ref, halo_ref, hist_ref, w_ref, xp_scr, y_scr, width, sh_scr=None):
    tr = x_ref.shape[0]
    hsub = (width - 1) * SUBSET
    nsub = tr // CHUNK

    @pl.when(i < n_s_tiles)
    def _():
        for s in range(nsub):
            xp_scr[0:hsub, :] = hist_ref[s * hsub:(s + 1) * hsub, :]
            xp_scr[hsub:hsub + CHUNK, :] = x_ref[s * CHUNK:(s + 1) * CHUNK, :].astype(F32)
            y_scr[s * CHUNK:(s + 1) * CHUNK, :] = _conv_taps(xp_scr, hsub, CHUNK, SUBSET, w_ref, width)

    @pl.when(i >= n_s_tiles)
    def _():
        halo = halo_ref[...].astype(F32)
        xp_scr[0:HALO, :] = jnp.where(i == n_s_tiles, jnp.zeros_like(halo), halo)
        xp_scr[HALO:HALO + tr, :] = x_ref[...].astype(F32)
        if sh_scr is None:
            y_scr[...] = _conv_taps(xp_scr, HALO, tr, 1, w_ref, width)
        else:
            y_scr[...] = _conv_taps_by_residue(xp_scr, sh_scr, HALO, tr, w_ref, width)


def _softplus(x):
    return jnp.maximum(x, 0.0) + jnp.log(1.0 + jnp.exp(-jnp.abs(x)))


def _conva_kernel(x_ref, halo_ref, hist_ref, w_ref, ab_ref, mask_ref, adt_ref, o_ref, gb_ref, xp_scr, y_scr,
                  *, n_s_tiles, width):
    i = pl.program_id(0)
    sec = pl.program_id(1)
    _conv_tile(i, n_s_tiles, x_ref, halo_ref, hist_ref, w_ref, xp_scr, y_scr, width)

    @pl.when(sec == 0)
    def _():
        ab = ab_ref[...]
        lane = lax.broadcasted_iota(jnp.int32, ab.shape, 1)
        g = -jnp.exp(adt_ref[0:1, :]) * _softplus(ab + adt_ref[1:2, :])
        gb_ref[...] = jnp.where(lane < H_A, g, _sigmoid(ab)) * mask_ref[...]

    scale = jnp.where(sec == 0, DK ** -0.5, 1.0).astype(F32)
    for h in range(H_A):
        hs = slice(h * DK, (h + 1) * DK)
        y = _silu(y_scr[:, hs])
        yn = y * (lax.rsqrt(jnp.sum(y * y, axis=-1, keepdims=True) + EPS) * scale)
        o_ref[:, hs] = jnp.where(sec == 2, y, yn)


def _convb_kernel(x_ref, halo_ref, hist_ref, w_ref, gate_ref, o_ref, xp_scr, y_scr, *, n_s_tiles, width):
    i = pl.program_id(0)
    _conv_tile(i, n_s_tiles, x_ref, halo_ref, hist_ref, w_ref, xp_scr, y_scr, width)
    o_ref[...] = (gate_ref[...].astype(F32) * y_scr[...]).astype(o_ref.dtype)


def _convc_kernel(x_ref, halo_ref, hist_ref, w_ref, cb_ref, lnw_ref, lnb_ref, o_ref, xp_scr, y_scr, sh_scr,
                  *, n_s_tiles, width):
    i = pl.program_id(0)
    _conv_tile(i, n_s_tiles, x_ref, halo_ref, hist_ref, w_ref, xp_scr, y_scr, width, sh_scr)
    y = y_scr[...] + cb_ref[...]
    mu = jnp.mean(y, axis=-1, keepdims=True)
    yc = y - mu
    var = jnp.mean(yc * yc, axis=-1, keepdims=True)
    yn = yc * lax.rsqrt(var + LN_EPS) * lnw_ref[...] + lnb_ref[...]
    o_ref[...] = _silu(yn).astype(o_ref.dtype)


def _conv_specs(tr, cw, width, col_of, n_s_tiles, sec_axis):
    hrows = (tr // CHUNK) * (width - 1) * SUBSET
    hb = tr // HALO
    last_hist = max(n_s_tiles - 1, 0)
    if sec_axis:
        x_map = lambda i, s: (i, col_of(s))
        halo_map = lambda i, s: (jnp.maximum(i * hb - 1, 0), col_of(s))
        hist_map = lambda i, s: (jnp.minimum(i, last_hist), s)
    else:
        x_map = lambda i: (i, col_of(0))
        halo_map = lambda i: (jnp.maximum(i * hb - 1, 0), col_of(0))
        hist_map = lambda i: (jnp.minimum(i, last_hist), 0)
    return [pl.BlockSpec((tr, cw), x_map), pl.BlockSpec((HALO, cw), halo_map), pl.BlockSpec((hrows, cw), hist_map)]


def _dot(a, b):
    return jnp.dot(a.astype(BF16), b.astype(BF16), preferred_element_type=F32)


def _dot_nt(a, b):
    return lax.dot_general(a.astype(BF16), b.astype(BF16), (((1,), (1,)), ((), ())), preferred_element_type=F32)


def _split(x):
    hi = x.astype(BF16)
    return hi, (x - hi.astype(F32)).astype(BF16)


def _mask_dot(mask_bf, x):
    hi, lo = _split(x)
    return (jnp.dot(mask_bf, hi, preferred_element_type=F32)
            + jnp.dot(mask_bf, lo, preferred_element_type=F32))


def _mask_dot_nt(xt, mask_bf):
    hi, lo = _split(xt)
    dn = (((1,), (1,)), ((), ()))
    return (lax.dot_general(hi, mask_bf, dn, preferred_element_type=F32)
            + lax.dot_general(lo, mask_bf, dn, preferred_element_type=F32))


def _bf_mask(m):
    return jnp.where(m, 1.0, 0.0).astype(BF16)


def _chunk_common(q_ref, k_ref, v_ref, gb, gc_all, gr_all, lmask, strict):
    heads = range(H_A)
    hs = [slice(h * DK, (h + 1) * DK) for h in heads]
    q = [q_ref[:, hs[h]] for h in heads]
    k = [k_ref[:, hs[h]] for h in heads]
    v = [v_ref[:, hs[h]] for h in heads]
    gc = [gc_all[:, h:h + 1] for h in heads]
    beta = [gb[:, H_A + h:H_A + h + 1] for h in heads]
    decay = [jnp.exp(jnp.where(lmask, gc[h] - gr_all[h:h + 1, :], NEG)) for h in heads]
    C = q[0].shape[0]
    qkk = [_dot_nt(jnp.concatenate([q[h], k[h]], axis=0), k[h]) for h in heads]
    qk = [qkk[h][:C] * decay[h] for h in heads]
    m = [jnp.where(strict, beta[h] * qkk[h][C:] * decay[h], 0.0) for h in heads]
    eg = [jnp.exp(gc[h]) for h in heads]
    rhs = [jnp.concatenate([beta[h] * v[h], (beta[h] * eg[h]) * k[h]], axis=1) for h in heads]
    return q, k, gc, qk, m, eg, rhs


def _delta_kernel(q_ref, k_ref, v_ref, gb_ref, sga_ref, onw_ref, s0s_ref, ya_ref, ss_ref, sp_ref, s_scr,
                  *, n_s_steps, n_chunks):
    s = pl.program_id(0)
    C = CHUNK
    row = lax.broadcasted_iota(jnp.int32, (C, C), 0)
    col = lax.broadcasted_iota(jnp.int32, (C, C), 1)
    eye = (row == col).astype(F32)
    gb = gb_ref[...]
    gbt = gb.T[0:H_A, :]

    def finish(o, h):
        hs = slice(h * DV, (h + 1) * DV)
        on = o * lax.rsqrt(jnp.mean(o * o, axis=-1, keepdims=True) + EPS) * onw_ref[...]
        ya_ref[:, hs] = (on * sga_ref[:, hs].astype(F32)).astype(ya_ref.dtype)

    heads = range(H_A)

    @pl.when(s < n_s_steps)
    def _():
        same = ((row - col) & (DEC_SEQ - 1)) == 0
        lmask = same & (row >= col)
        strict = same & (row > col)
        lm = _bf_mask(lmask)
        last = _bf_mask(col == (C - DEC_SEQ) + (row & (DEC_SEQ - 1)))
        gc_all = _mask_dot(lm, gb)
        gr_all = _mask_dot_nt(gbt, lm)
        gl_all = _mask_dot(last, gc_all)
        rsub = lax.broadcasted_iota(jnp.int32, (C, 1), 0) & (DEC_SEQ - 1)
        rsub2 = jnp.concatenate([rsub, rsub], axis=0)
        q, k, gc, qk, m, eg, rhs = _chunk_common(q_ref, k_ref, v_ref, gb, gc_all, gr_all, lmask, strict)
        seqs = range(SUBSET)
        m2 = [_dot(m[h], m[h]) for h in heads]
        b1 = [eye - m[h] for h in heads]
        r1 = [_dot(jnp.concatenate([b1[h], m2[h]], axis=0), m2[h]) for h in heads]
        b2 = [b1[h] + r1[h][:C] for h in heads]
        tinv = [b2[h] + _dot(b2[h], r1[h][C:]) for h in heads]
        x = [_dot(tinv[h], rhs[h]) for h in heads]
        lhs = [jnp.concatenate([x[h][:, DV:], q[h]], axis=0) for h in heads]
        s_cat = [jnp.concatenate([s0s_ref[i, h] for i in seqs], axis=1) for h in heads]
        sk_all = [_dot(lhs[h], s_cat[h]) for h in heads]
        sk = []
        for h in heads:
            acc = jnp.zeros((2 * C, DV), F32)
            for i in seqs:
                acc = acc + jnp.where(rsub2 == i, sk_all[h][:, i * DV:(i + 1) * DV], 0.0)
            sk.append(acc)
        u = [x[h][:, :DV] - sk[h][:C] for h in heads]
        kd = [k[h] * jnp.exp(gl_all[:, h:h + 1] - gc[h]) for h in heads]
        kdt = [jnp.concatenate([jnp.where(rsub == i, kd[h], 0.0).T for i in seqs], axis=0) for h in heads]
        r2 = [_dot(jnp.concatenate([qk[h], kdt[h]], axis=0), u[h]) for h in heads]
        for h in heads:
            for i in seqs:
                gli = gc_all[C - DEC_SEQ + i:C - DEC_SEQ + i + 1, h:h + 1]
                ss_ref[i, h] = jnp.exp(gli) * s0s_ref[i, h] + r2[h][C + i * DK:C + (i + 1) * DK]
        for h in heads:
            finish(eg[h] * sk[h][C:] + r2[h][:C], h)

    @pl.when(s >= n_s_steps)
    def _():
        c = (s - n_s_steps) % n_chunks

        @pl.when(c == 0)
        def _():
            s_scr[...] = jnp.zeros_like(s_scr)

        lmask = row >= col
        strict = row > col
        lm = _bf_mask(lmask)
        blk = (row >> 4) == (col >> 4)
        gc_all = _mask_dot(lm, gb)
        gr_all = _mask_dot_nt(gbt, lm)
        q, k, gc, qk, m, eg, rhs = _chunk_common(q_ref, k_ref, v_ref, gb, gc_all, gr_all, lmask, strict)
        nd = [jnp.where(blk, -m[h], 0.0) for h in heads]
        lo = [jnp.where(blk, 0.0, m[h]) for h in heads]
        W2 = 2 * DV
        p2 = [_dot(nd[h], nd[h]) for h in heads]
        a1 = [eye + nd[h] for h in heads]
        r1 = [_dot(jnp.concatenate([a1[h], p2[h]], axis=0), p2[h]) for h in heads]
        t1 = [a1[h] + r1[h][:C] for h in heads]
        r2 = [_dot(jnp.concatenate([t1[h], r1[h][C:]], axis=0), r1[h][C:]) for h in heads]
        t2 = [t1[h] + r2[h][:C] for h in heads]
        dinv = [t2[h] + _dot(t2[h], r2[h][C:]) for h in heads]
        ya = [_dot(dinv[h], jnp.concatenate([rhs[h], lo[h]], axis=1)) for h in heads]
        y = [ya[h][:, :W2] for h in heads]
        a = [ya[h][:, W2:] for h in heads]
        r3 = [_dot(a[h], ya[h]) for h in heads]
        w = [y[h] - r3[h][:, :W2] for h in heads]
        x = [w[h] + _dot(r3[h][:, W2:], w[h]) for h in heads]
        S = [s_scr[h] for h in heads]
        sk = [_dot(jnp.concatenate([x[h][:, DV:], q[h]], axis=0), S[h]) for h in heads]
        u = [x[h][:, :DV] - sk[h][:C] for h in heads]
        gl = [gc_all[C - 1:C, h:h + 1] for h in heads]
        kd = [k[h] * jnp.exp(gl[h] - gc[h]) for h in heads]
        r4 = [_dot(jnp.concatenate([qk[h], kd[h].T], axis=0), u[h]) for h in heads]
        for h in heads:
            s_scr[h] = jnp.exp(gl[h]) * S[h] + r4[h][C:]
        for h in heads:
            finish(eg[h] * sk[h][C:] + r4[h][:C], h)

        @pl.when(c == n_chunks - 1)
        def _():
            sp_ref[0] = s_scr[...]


def _delta(qkv, gb, z, onorm_w, state_all, l, *, n_s_steps, n_batch, n_chunks, sga_col):
    T = qkv.shape[0]
    W = H_A * DK
    n_steps = n_s_steps + n_batch * n_chunks
    last_s = max(n_s_steps - 1, 0)
    kern = functools.partial(_delta_kernel, n_s_steps=n_s_steps, n_chunks=n_chunks)
    return pl.pallas_call(
        kern,
        grid=(n_steps,),
        in_specs=[
            pl.BlockSpec((CHUNK, W), lambda s: (s, 0)),
            pl.BlockSpec((CHUNK, W), lambda s: (s, 1)),
            pl.BlockSpec((CHUNK, W), lambda s: (s, 2)),
            pl.BlockSpec((CHUNK, LANES), lambda s: (s, 0)),
            pl.BlockSpec((CHUNK, W), lambda s: (s, sga_col)),
            pl.BlockSpec((1, DV), lambda s: (0, 0)),
            pl.BlockSpec((None, SUBSET, H_A, DK, DV), lambda s: (l, jnp.minimum(s, last_s), 0, 0, 0)),
        ],
        out_specs=[
            pl.BlockSpec((CHUNK, W), lambda s: (s, 0)),
            pl.BlockSpec((SUBSET, H_A, DK, DV), lambda s: (jnp.minimum(s, last_s), 0, 0, 0)),
            pl.BlockSpec((1, H_A, DK, DV), lambda s: (jnp.maximum(s - n_s_steps, 0) // n_chunks, 0, 0, 0)),
        ],
        out_shape=[
            jax.ShapeDtypeStruct((T, W), BF16),
            jax.ShapeDtypeStruct(state_all.shape[1:], F32),
            jax.ShapeDtypeStruct((n_batch, H_A, DK, DV), F32),
        ],
        scratch_shapes=[pltpu.VMEM((H_A, DK, DV), F32)],
        compiler_params=_cp(("arbitrary",)),
        name="delta",
    )(qkv, qkv, qkv, gb, z, onorm_w, state_all)


def _merge_kernel(ya_ref, yb_ref, yc_ref, wa_ref, wb_ref, wc_ref, g0_ref, g1_ref, g2_ref, m_ref):
    m = g0_ref[...].astype(F32) * jnp.dot(ya_ref[...], wa_ref[...], preferred_element_type=F32)
    m = m + g1_ref[...].astype(F32) * jnp.dot(yb_ref[...], wb_ref[...], preferred_element_type=F32)
    m = m + g2_ref[...].astype(F32) * jnp.dot(yc_ref[...], wc_ref[...], preferred_element_type=F32)
    m_ref[...] = m.astype(m_ref.dtype)


def _merge(ya, yb, yc, wa, wb, wc, z, *, gs_col, tm, tn):
    T, K = ya.shape
    D = wa.shape[1]
    nb = D // tn
    g0 = gs_col // tn
    act = pl.BlockSpec((tm, K), lambda i, j: (i, 0))
    wsp = pl.BlockSpec((K, tn), lambda i, j: (0, j))
    return pl.pallas_call(
        _merge_kernel,
        grid=(T // tm, nb),
        in_specs=[act, act, act, wsp, wsp, wsp,
                  pl.BlockSpec((tm, tn), lambda i, j: (i, g0 + j)),
                  pl.BlockSpec((tm, tn), lambda i, j: (i, g0 + nb + j)),
                  pl.BlockSpec((tm, tn), lambda i, j: (i, g0 + 2 * nb + j))],
        out_specs=pl.BlockSpec((tm, tn), lambda i, j: (i, j)),
        out_shape=jax.ShapeDtypeStruct((T, D), BF16),
        compiler_params=_cp(("arbitrary", "arbitrary")),
        name="merge",
    )(ya, yb, yc, wa, wb, wc, z, z, z)


def _oproj_kernel(m_ref, w_ref, x_ref, o_ref):
    o_ref[...] = x_ref[...] + jnp.dot(m_ref[...], w_ref[...], preferred_element_type=F32)


def _oproj(m, w_o, x, *, tm, tn):
    T, D = x.shape
    return pl.pallas_call(
        _oproj_kernel,
        grid=(T // tm, D // tn),
        in_specs=[pl.BlockSpec((tm, D), lambda i, j: (i, 0)),
                  pl.BlockSpec((D, tn), lambda i, j: (0, j)),
                  pl.BlockSpec((tm, tn), lambda i, j: (i, j))],
        out_specs=pl.BlockSpec((tm, tn), lambda i, j: (i, j)),
        out_shape=jax.ShapeDtypeStruct((T, D), F32),
        compiler_params=_cp(("arbitrary", "arbitrary")),
        name="oproj",
    )(m, w_o, x)


def _dot_x3(a, b):
    ah, al = _split(a)
    bh, bl = _split(b)
    d = lambda u, v: jnp.dot(u, v, preferred_element_type=F32)
    return d(ah, bh) + (d(al, bh) + d(ah, bl))


def _router_kernel(x_ref, nw_ref, wr_ref, br_ref, ri_ref, rw_ref, cnt_ref, carry_scr, *, n_groups, per_group):
    i = pl.program_id(0)
    tr = x_ref.shape[0]

    @pl.when(i == 0)
    def _():
        carry_scr[...] = jnp.zeros_like(carry_scr)

    x = x_ref[...]
    h = x * lax.rsqrt(jnp.mean(x * x, axis=-1, keepdims=True) + EPS) * nw_ref[...]
    logits = _dot_x3(h, wr_ref[...]) + br_ref[...]
    lane = lax.broadcasted_iota(jnp.int32, logits.shape, 1)
    big = jnp.int32(1 << 20)

    def argmax_first(vals):
        m = jnp.max(vals, axis=-1, keepdims=True)
        idx = jnp.min(jnp.where(vals == m, lane, big), axis=-1, keepdims=True)
        return m, idx

    is_g = lane < n_groups
    mg, gsel = argmax_first(jnp.where(is_g, logits, NEG))
    pg = 1.0 / jnp.sum(jnp.where(is_g, jnp.exp(logits - mg), 0.0), axis=-1, keepdims=True)
    lo = n_groups + gsel * per_group
    in_group = (lane >= lo) & (lane < lo + per_group)
    le = jnp.where(in_group, logits, NEG)
    m1, i1 = argmax_first(le)
    m2, i2 = argmax_first(jnp.where(lane == i1, NEG, le))
    e21 = jnp.exp(m2 - m1)
    w1 = pg / (1.0 + e21)
    w2 = w1 * e21

    oh = ((lane == i1) | (lane == i2)).astype(F32)
    r_i = lax.broadcasted_iota(jnp.int32, (tr, tr), 0)
    c_i = lax.broadcasted_iota(jnp.int32, (tr, tr), 1)
    before = _dot((r_i > c_i).astype(F32), oh) + carry_scr[...]
    rank1 = jnp.sum(jnp.where(lane == i1, before, 0.0), axis=-1, keepdims=True)
    rank2 = jnp.sum(jnp.where(lane == i2, before, 0.0), axis=-1, keepdims=True)
    carry_scr[...] = carry_scr[...] + jnp.sum(oh, axis=0, keepdims=True)
    cnt_ref[...] = jnp.broadcast_to(carry_scr[...], cnt_ref.shape)

    ri = jnp.where(lane == 0, i1 - n_groups,
                   jnp.where(lane == 1, i2 - n_groups,
                             jnp.where(lane == 2, rank1.astype(jnp.int32),
                                       jnp.where(lane == 3, rank2.astype(jnp.int32), 0))))
    ri_ref[...] = ri
    rw_ref[...] = jnp.where(lane == 0, w1, jnp.where(lane == 1, w2, 0.0))


def _router(x, norm_w, w_r, b_r, *, n_groups, per_group, tr):
    T, D = x.shape
    kern = functools.partial(_router_kernel, n_groups=n_groups, per_group=per_group)
    return pl.pallas_call(
        kern,
        grid=(T // tr,),
        in_specs=[pl.BlockSpec((tr, D), lambda i: (i, 0)),
                  pl.BlockSpec((1, D), lambda i: (0, 0)),
                  pl.BlockSpec((D, LANES), lambda i: (0, 0)),
                  pl.BlockSpec((1, LANES), lambda i: (0, 0))],
        out_specs=[pl.BlockSpec((tr, LANES), lambda i: (i, 0)),
                   pl.BlockSpec((tr, LANES), lambda i: (i, 0)),
                   pl.BlockSpec((8, LANES), lambda i: (0, 0))],
        out_shape=[jax.ShapeDtypeStruct((T, LANES), jnp.int32),
                   jax.ShapeDtypeStruct((T, LANES), F32),
                   jax.ShapeDtypeStruct((8, LANES), F32)],
        scratch_shapes=[pltpu.VMEM((1, LANES), F32)],
        compiler_params=_cp(("arbitrary",)),
        name="router",
    )(x, norm_w, w_r, b_r)


def _row_copy(src, src_row, dst, dst_row, sem):
    return pltpu.make_async_copy(src.at[pl.ds(src_row, 1), :], dst.at[pl.ds(dst_row, 1), :], sem)


def _dispatch_kernel(e1_ref, r1_ref, e2_ref, r2_ref, off_ref, pad_ref, npad_ref, nu_ref, x_ref, nw_ref, xs_ref,
                     h_scr, z_scr, sem):
    i = pl.program_id(0)
    tr = x_ref.shape[0]
    n_experts = pad_ref.shape[0]
    tg = z_scr.shape[0]

    @pl.when(i == 0)
    def _():
        z_scr[...] = jnp.zeros_like(z_scr)
        n_all = xs_ref.shape[0] // tg

        def fill(start, rows):
            dst = xs_ref.at[pl.ds(pl.multiple_of(start, 8), rows), :]
            return pltpu.make_async_copy(z_scr.at[pl.ds(0, rows), :], dst, sem.at[2])

        def pad_fills(act):
            def per_expert(e, c):
                def piece(k, c2):
                    act(fill(pad_ref[e] + k * 8, 8))
                    return c2
                return lax.fori_loop(0, npad_ref[e], piece, c)
            lax.fori_loop(0, n_experts, per_expert, 0)

        def tail_fills(act):
            def tile(k, c):
                act(fill(k * tg, tg))
                return c
            lax.fori_loop(nu_ref[0], n_all, tile, 0)

        pad_fills(lambda cp: cp.start())
        tail_fills(lambda cp: cp.start())
        pad_fills(lambda cp: cp.wait())
        tail_fills(lambda cp: cp.wait())

    x = x_ref[...]
    h_scr[...] = x * lax.rsqrt(jnp.mean(x * x, axis=-1, keepdims=True) + EPS) * nw_ref[...]
    base = i * tr

    def issue(r, c):
        t = base + r
        _row_copy(h_scr, r, xs_ref, off_ref[e1_ref[t]] + r1_ref[t], sem.at[0]).start()
        _row_copy(h_scr, r, xs_ref, off_ref[e2_ref[t]] + r2_ref[t], sem.at[1]).start()
        return c
    lax.fori_loop(0, tr, issue, 0, unroll=ISSUE_UNROLL)
    pltpu.make_async_copy(h_scr, xs_ref.at[pl.ds(0, tr), :], sem.at[0]).wait()
    pltpu.make_async_copy(h_scr, xs_ref.at[pl.ds(0, tr), :], sem.at[1]).wait()


def _dispatch(route, pad_start, pad_groups, n_used, x, norm_w, *, n_rows, tr, tg):
    T, D = x.shape
    n_pre = len(route) + 3
    grid_spec = pltpu.PrefetchScalarGridSpec(
        num_scalar_prefetch=n_pre,
        grid=(T // tr,),
        in_specs=[pl.BlockSpec((tr, D), lambda i, *_: (i, 0)),
                  pl.BlockSpec((1, D), lambda i, *_: (0, 0))],
        out_specs=pl.BlockSpec(memory_space=pl.ANY),
        scratch_shapes=[pltpu.VMEM((tr, D), F32), pltpu.VMEM((tg, D), F32), pltpu.SemaphoreType.DMA((3,))],
    )
    return pl.pallas_call(
        _dispatch_kernel,
        grid_spec=grid_spec,
        out_shape=jax.ShapeDtypeStruct((n_rows, D), F32),
        compiler_params=_cp(("arbitrary",)),
        name="dispatch",
    )(*route, pad_start, pad_groups, n_used, x, norm_w)


def _expert_kernel(te_ref, nu_ref, xs_ref, wg_ref, wu_ref, wd_ref, o_ref, wg_scr, wu_scr, wd_scr):
    i = pl.program_id(0)
    prev = te_ref[jnp.maximum(i - 1, 0)]

    @pl.when((i == 0) | (te_ref[i] != prev))
    def _():
        wg_scr[...] = wg_ref[...].astype(BF16)
        wu_scr[...] = wu_ref[...].astype(BF16)
        wd_scr[...] = wd_ref[...].astype(BF16)

    @pl.when(i < nu_ref[0])
    def _():
        h = xs_ref[...].astype(BF16)
        g = jnp.dot(h, wg_scr[...], preferred_element_type=F32)
        u = jnp.dot(h, wu_scr[...], preferred_element_type=F32)
        a = (_silu(g) * u).astype(BF16)
        o_ref[...] = jnp.dot(a, wd_scr[...], preferred_element_type=F32)

    @pl.when(i >= nu_ref[0])
    def _():
        o_ref[...] = jnp.zeros_like(o_ref)


def _experts(tile_e, n_used, xs, w_gate, w_up, w_down, l, *, n_tiles, tg):
    D = xs.shape[1]
    F = w_gate.shape[-1]

    def xs_map(i, te, nu):
        return (jnp.maximum(jnp.minimum(i, nu[0] - 1), 0), 0)

    grid_spec = pltpu.PrefetchScalarGridSpec(
        num_scalar_prefetch=2,
        grid=(n_tiles,),
        in_specs=[pl.BlockSpec((tg, D), xs_map),
                  pl.BlockSpec((None, None, D, F), lambda i, te, nu: (l, te[i], 0, 0)),
                  pl.BlockSpec((None, None, D, F), lambda i, te, nu: (l, te[i], 0, 0)),
                  pl.BlockSpec((None, None, F, D), lambda i, te, nu: (l, te[i], 0, 0))],
        out_specs=pl.BlockSpec((tg, D), lambda i, te, nu: (i, 0)),
        scratch_shapes=[pltpu.VMEM((D, F), BF16), pltpu.VMEM((D, F), BF16), pltpu.VMEM((F, D), BF16)],
    )
    return pl.pallas_call(
        _expert_kernel,
        grid_spec=grid_spec,
        out_shape=jax.ShapeDtypeStruct((n_tiles * tg, D), F32),
        compiler_params=_cp(("arbitrary",)),
        name="experts",
    )(tile_e, n_used, xs, w_gate, w_up, w_down)


def _combine_kernel(e1_ref, r1_ref, e2_ref, r2_ref, off_ref, x_ref, rw_ref, mask_ref, fw_ref, eo_ref, out_ref,
                    b1_scr, b2_scr, sem, *, final):
    i = pl.program_id(0)
    tr = x_ref.shape[0]
    base = i * tr

    def issue(r, c):
        t = base + r
        _row_copy(eo_ref, off_ref[e1_ref[t]] + r1_ref[t], b1_scr, r, sem.at[0]).start()
        _row_copy(eo_ref, off_ref[e2_ref[t]] + r2_ref[t], b2_scr, r, sem.at[1]).start()
        return c
    lax.fori_loop(0, tr, issue, 0, unroll=ISSUE_UNROLL)
    pltpu.make_async_copy(eo_ref.at[pl.ds(0, tr), :], b1_scr, sem.at[0]).wait()
    pltpu.make_async_copy(eo_ref.at[pl.ds(0, tr), :], b2_scr, sem.at[1]).wait()
    rw = rw_ref[...]
    x2 = (x_ref[...] + rw[:, 0:1] * b1_scr[...] + rw[:, 1:2] * b2_scr[...]) * mask_ref[:, 0:1]
    if final:
        out_ref[...] = x2 * lax.rsqrt(jnp.mean(x2 * x2, axis=-1, keepdims=True) + EPS) * fw_ref[...]
    else:
        out_ref[...] = x2


def _combine(route, x, rw, mask, final_w, eo, *, tr, final):
    T, D = x.shape
    row = lambda w: pl.BlockSpec((tr, w), lambda i, *_: (i, 0))
    grid_spec = pltpu.PrefetchScalarGridSpec(
        num_scalar_prefetch=len(route),
        grid=(T // tr,),
        in_specs=[row(D), row(LANES), row(LANES),
                  pl.BlockSpec((1, D), lambda i, *_: (0, 0)),
                  pl.BlockSpec(memory_space=pl.ANY)],
        out_specs=row(D),
        scratch_shapes=[pltpu.VMEM((tr, D), F32), pltpu.VMEM((tr, D), F32), pltpu.SemaphoreType.DMA((2,))],
    )
    return pl.pallas_call(
        functools.partial(_combine_kernel, final=final),
        grid_spec=grid_spec,
        out_shape=jax.ShapeDtypeStruct((T, D), F32),
        compiler_params=_cp(("arbitrary",)),
        name="combine",
    )(*route, x, rw, mask, final_w, eo)


def _sample_to_rows(a):
    nb, L, C = a.shape
    return a.reshape(nb // SUBSET, SUBSET, L, C).transpose(0, 2, 1, 3).reshape(nb * L, C)


def _rows_to_sample(r, L):
    n, C = r.shape
    nb = n // L
    return r.reshape(nb // SUBSET, L, SUBSET, C).transpose(0, 2, 1, 3).reshape(nb, L, C)


def kernel(x_prompt, x_sample, state_delta, state_conv_a, state_conv_b, state_conv_c, meta_tokens, norm1_w, w_in, conv_a_w, a_log, dt_bias, onorm_a_w, w_out_a, conv_b_w, w_out_b, conv_c_w, conv_c_b, ln_c_w, ln_c_b, w_out_c, w_o, norm2_w, w_rg, b_rg, w_re, b_re, w_gate_e, w_up_e, w_down_e, final_norm_w):
    B, SEQ, D = x_prompt.shape
    NB, L, _ = x_sample.shape
    depth = w_in.shape[0]
    W_QKV = conv_a_w.shape[-1]
    W_V = H_A * DV
    W_B = conv_b_w.shape[-1]
    W_C = conv_c_w.shape[-1]
    CA, CB, CC = conv_a_w.shape[1], conv_b_w.shape[1], conv_c_w.shape[1]
    G = w_rg.shape[-1]
    E = w_re.shape[-1]
    assert L == DEC_SEQ and NB % SUBSET == 0 and SEQ % CHUNK == 0 and W_QKV == 3 * W_V
    assert W_V == W_B == W_C and max(CA, CB, CC) - 1 <= min(HALO, CHUNK - N_META)
    assert G + E <= LANES and D % LANES == 0

    LP = CHUNK + SEQ
    lead = CHUNK - N_META
    Ts, Tp = NB * L, B * LP
    T = Ts + Tp
    n_chunks = LP // CHUNK
    n_s_steps = Ts // CHUNK

    tr = _pick_tile(_gcd(Ts, Tp), 256, CHUNK)
    tm = _pick_tile(T, 1184, 16)
    tn = 512
    tg = 256
    n_s_tiles = Ts // tr

    dt_ = x_prompt.dtype
    lead_rows = jnp.concatenate([jnp.zeros((lead, D), dt_), meta_tokens.astype(dt_)], axis=0)
    xp = jnp.concatenate([jnp.broadcast_to(lead_rows[None], (B, CHUNK, D)), x_prompt], axis=1).reshape(Tp, D)
    x = jnp.concatenate([_sample_to_rows(x_sample), xp], axis=0)
    pos_in_seq = jnp.arange(Tp, dtype=jnp.int32) % LP
    real = jnp.concatenate([jnp.ones((Ts,), F32), (pos_in_seq >= lead).astype(F32)])
    mask = jnp.broadcast_to(real[:, None], (T, LANES))

    n_ab = 2 * H_A
    w_t = jnp.swapaxes(w_in, 1, 2)
    w_r = jnp.pad(jnp.concatenate([w_rg, w_re], axis=2), ((0, 0), (0, 0), (0, LANES - G - E)))
    b_r = jnp.pad(jnp.concatenate([b_rg, b_re], axis=1), ((0, 0), (0, LANES - G - E)))[:, None, :]
    adt = jnp.pad(jnp.stack([a_log, dt_bias], axis=1), ((0, 0), (0, 0), (0, LANES - H_A)))
    adt = jnp.pad(adt, ((0, 0), (0, 6), (0, 0)))
    woa, wob, woc, wo = (w.astype(BF16) for w in (w_out_a, w_out_b, w_out_c, w_o))

    col_sga, col_gb, col_u, col_gl, col_gs = W_QKV, W_QKV + W_V, W_QKV + W_V + W_B, W_QKV + W_V + 2 * W_B, W_QKV + W_V + 2 * W_B + W_C
    cw = W_V
    P = TOP_K * T + E * tg
    n_tiles = P // tg

    dp, ap, bpl, cpl, ds, as_, bs, cs = [], [], [], [], [], [], [], []
    for l in range(depth):
        z, ab = _inproj(x, norm1_w[l][None], w_t, l,
                        w_qkv=W_QKV, n_ab=n_ab, w_v=W_V, w_b=W_B, w_c=W_C, tm=tm, tn=tn)

        def hist_rows(st):
            nb, hw, C = st.shape
            return st.reshape(nb // SUBSET, SUBSET, hw, C).transpose(0, 2, 1, 3).reshape(nb * hw, C)
        hist_a, hist_b, hist_c = hist_rows(state_conv_a[l]), hist_rows(state_conv_b[l]), hist_rows(state_conv_c[l])

        xs_, hs_, hi_ = _conv_specs(tr, cw, CA, lambda s: s, n_s_tiles, True)
        qkv, gb = pl.pallas_call(
            functools.partial(_conva_kernel, n_s_tiles=n_s_tiles, width=CA),
            grid=(T // tr, W_QKV // cw),
            in_specs=[xs_, hs_, hi_,
                      pl.BlockSpec((CA, cw), lambda i, s: (0, s)),
                      pl.BlockSpec((tr, LANES), lambda i, s: (i, 0)),
                      pl.BlockSpec((tr, LANES), lambda i, s: (i, 0)),
                      pl.BlockSpec((8, LANES), lambda i, s: (0, 0))],
            out_specs=[pl.BlockSpec((tr, cw), lambda i, s: (i, s)),
                       pl.BlockSpec((tr, LANES), lambda i, s: (i, 0))],
            out_shape=[jax.ShapeDtypeStruct((T, W_QKV), F32), jax.ShapeDtypeStruct((T, LANES), F32)],
            scratch_shapes=[pltpu.VMEM((HALO + tr, cw), F32), pltpu.VMEM((tr, cw), F32)],
            compiler_params=_cp(("arbitrary", "arbitrary")),
            name="conv_a",
        )(z, z, hist_a, conv_a_w[l], ab, mask, adt[l])

        ya, s_s, s_p = _delta(qkv, gb, z, onorm_a_w[l][None], state_delta, l,
                              n_s_steps=n_s_steps, n_batch=B, n_chunks=n_chunks, sga_col=col_sga // cw)

        xs_, hs_, hi_ = _conv_specs(tr, cw, CB, lambda s: col_u // cw, n_s_tiles, False)
        yb = pl.pallas_call(
            functools.partial(_convb_kernel, n_s_tiles=n_s_tiles, width=CB),
            grid=(T // tr,),
            in_specs=[xs_, hs_, hi_,
                      pl.BlockSpec((CB, cw), lambda i: (0, 0)),
                      pl.BlockSpec((tr, cw), lambda i: (i, col_gb // cw))],
            out_specs=pl.BlockSpec((tr, cw), lambda i: (i, 0)),
            out_shape=jax.ShapeDtypeStruct((T, cw), BF16),
            scratch_shapes=[pltpu.VMEM((HALO + tr, cw), F32), pltpu.VMEM((tr, cw), F32)],
            compiler_params=_cp(("arbitrary",)),
            name="conv_b",
        )(z, z, hist_b, conv_b_w[l], z)

        xs_, hs_, hi_ = _conv_specs(tr, cw, CC, lambda s: col_gl // cw, n_s_tiles, False)
        hc_rows = max(HALO + tr, (CC - 1) * SUBSET + CHUNK)
        yc = pl.pallas_call(
            functools.partial(_convc_kernel, n_s_tiles=n_s_tiles, width=CC),
            grid=(T // tr,),
            in_specs=[xs_, hs_, hi_,
                      pl.BlockSpec((CC, cw), lambda i: (0, 0)),
                      pl.BlockSpec((1, cw), lambda i: (0, 0)),
                      pl.BlockSpec((1, cw), lambda i: (0, 0)),
                      pl.BlockSpec((1, cw), lambda i: (0, 0))],
            out_specs=pl.BlockSpec((tr, cw), lambda i: (i, 0)),
            out_shape=jax.ShapeDtypeStruct((T, cw), BF16),
            scratch_shapes=[pltpu.VMEM((hc_rows, cw), F32), pltpu.VMEM((tr, cw), F32),
                            pltpu.VMEM((HALO + tr, cw), F32)],
            compiler_params=_cp(("arbitrary",)),
            name="conv_c",
        )(z, z, hist_c, conv_c_w[l], conv_c_b[l][None], ln_c_w[l][None], ln_c_b[l][None])

        m = _merge(ya, yb, yc, woa[l], wob[l], woc[l], z, gs_col=col_gs, tm=tm, tn=tn)
        x1 = _oproj(m, wo[l], x, tm=tm, tn=tn)

        ri, rw, cnt = _router(x1, norm2_w[l][None], w_r[l], b_r[l], n_groups=G, per_group=E // G, tr=tr)
        counts = cnt[0, G:G + E].astype(jnp.int32)
        padded = ((counts + tg - 1) // tg) * tg
        ends = jnp.cumsum(padded)
        offs = ends - padded
        route = (ri[:, 0], ri[:, 2], ri[:, 1], ri[:, 3], offs.astype(jnp.int32))
        n_used = (ends[-1] // tg).astype(jnp.int32)
        tile_start = jnp.arange(n_tiles, dtype=jnp.int32) * tg
        tile_e = jnp.sum((tile_start[:, None] >= ends[None, :]).astype(jnp.int32), axis=1)
        last_e = jnp.sum((((n_used - 1) * tg) >= ends).astype(jnp.int32))
        tile_e = jnp.where(jnp.arange(n_tiles) < n_used, tile_e, last_e).astype(jnp.int32)

        pad_start = (((offs + counts) // 8) * 8).astype(jnp.int32)
        n_used = n_used.reshape(1)
        pad_groups = ((ends - pad_start) // 8).astype(jnp.int32)
        xs = _dispatch(route, pad_start, pad_groups, n_used, x1, norm2_w[l][None], n_rows=P, tr=tr, tg=tg)
        eo = _experts(tile_e, n_used, xs, w_gate_e, w_up_e, w_down_e, l, n_tiles=n_tiles, tg=tg)
        x = _combine(route, x1, rw, mask, final_norm_w[None], eo, tr=tr, final=(l == depth - 1))

        def prompt_tail(c0, c1, width):
            return jnp.stack([lax.slice(z, (Ts + (b + 1) * LP - (width - 1), c0), (Ts + (b + 1) * LP, c1))
                              for b in range(B)]).astype(dt_)

        def sample_hist(st, c0, c1):
            seq = _rows_to_sample(lax.slice(z, (0, c0), (Ts, c1)), L).astype(dt_)
            return jnp.concatenate([st.astype(dt_), seq], axis=1)[:, L:]
        dp.append(s_p)
        ap.append(prompt_tail(0, W_QKV, CA))
        bpl.append(prompt_tail(col_u, col_u + W_B, CB))
        cpl.append(prompt_tail(col_gl, col_gl + W_C, CC))
        ds.append(s_s)
        as_.append(sample_hist(state_conv_a[l], 0, W_QKV))
        bs.append(sample_hist(state_conv_b[l], col_u, col_u + W_B))
        cs.append(sample_hist(state_conv_c[l], col_gl, col_gl + W_C))

    y_sample = _rows_to_sample(lax.slice(x, (0, 0), (Ts, D)), L)
    y_prompt = jnp.stack([lax.slice(x, (Ts + b * LP + CHUNK, 0), (Ts + (b + 1) * LP, D)) for b in range(B)])
    return (y_prompt, y_sample, jnp.stack(dp), jnp.stack(ap), jnp.stack(bpl), jnp.stack(cpl),
            jnp.stack(ds), jnp.stack(as_), jnp.stack(bs), jnp.stack(cs))


def _gcd(a, b):
    while b:
        a, b = b, a % b
    return a
```

```python
import functools

import jax
import jax.numpy as jnp
from jax import lax
from jax.experimental import pallas as pl
from jax.experimental.pallas import tpu as pltpu

F32 = jnp.float32
BF16 = jnp.bfloat16

EPS = 1e-6
LN_EPS = 1e-5
N_META = 16
H_A = 8
DK = 128
DV = 128
CHUNK = 64
LANES = 128
DEC_SEQ = 8
SUBSET = CHUNK // DEC_SEQ
HALO = 32
TOP_K = 2
NEG = -1e30
ISSUE_UNROLL = 8

VMEM_LIMIT = 52 * 1024 * 1024


def _cp(dims, vmem=VMEM_LIMIT):
    return pltpu.CompilerParams(dimension_semantics=dims, vmem_limit_bytes=vmem)


def _sigmoid(x):
    return 0.5 * jnp.tanh(0.5 * x) + 0.5


def _silu(x):
    return x * _sigmoid(x)


def _pick_tile(n, target, mult):
    best = None
    for t in range(mult, min(n, target) + 1, mult):
        if n % t == 0:
            best = t
    assert best is not None, (n, target, mult)
    return best


def _inproj_kernel(x_ref, nw_ref, wa_ref, wb_ref, wab_ref, z_ref, ab_ref, h_scr, *, bounds, row_chunk):
    j = pl.program_id(1)
    tm = x_ref.shape[0]

    def proj(w_ref):
        return lax.dot_general(h_scr[...], w_ref[0].astype(BF16), (((1,), (1,)), ((), ())),
                               preferred_element_type=F32)

    @pl.when(j == 0)
    def _():
        def body(r, c):
            rs = pl.ds(pl.multiple_of(r * row_chunk, row_chunk), row_chunk)
            x = x_ref[rs, :]
            h = x * lax.rsqrt(jnp.mean(x * x, axis=-1, keepdims=True) + EPS) * nw_ref[...]
            h_scr[rs, :] = h.astype(BF16)
            return c
        lax.fori_loop(0, tm // row_chunk, body, 0)
        ab_ref[...] = proj(wab_ref)

    b_silu, b_id2, b_mul, b_glu, b_sig = bounds

    def put(v):
        z_ref[...] = v.astype(z_ref.dtype)

    @pl.when((j < b_silu) | ((j >= b_id2) & (j < b_mul)))
    def _():
        put(proj(wa_ref))

    @pl.when((j >= b_silu) & (j < b_id2))
    def _():
        put(_silu(proj(wa_ref)))

    @pl.when((j >= b_mul) & (j < b_glu))
    def _():
        put(proj(wa_ref) * proj(wb_ref))

    @pl.when((j >= b_glu) & (j < b_sig))
    def _():
        put(proj(wa_ref) * _sigmoid(proj(wb_ref)))

    @pl.when(j >= b_sig)
    def _():
        put(_sigmoid(proj(wa_ref)))


def _inproj(x, norm_w, w_t, l, *, w_qkv, n_ab, w_v, w_b, w_c, tm, tn):
    T, D = x.shape
    nq, nv, nb, nc = w_qkv // tn, w_v // tn, w_b // tn, w_c // tn
    r0 = w_qkv + n_ab
    b_silu = nq
    b_id2 = nq + nv
    b_mul = b_id2 + nb
    b_glu = b_mul + nb
    b_sig = b_glu + nc
    n_out = nq + (w_t.shape[1] - r0) // tn - nb - nc
    park_lo = b_glu - nq
    park_hi = b_sig - nq + nb + nc - 1
    assert w_qkv % LANES == 0 and r0 % 8 == 0 and (w_t.shape[1] - r0) % tn == 0

    def wa_map(i, j):
        rest = j - nq + jnp.where(j >= b_glu, nb, 0) + jnp.where(j >= b_sig, nc, 0)
        return (l, pl.multiple_of(jnp.where(j < nq, j * tn, r0 + rest * tn), 8), 0)

    def wb_map(i, j):
        jb = jnp.where(j < b_mul, park_lo,
                       jnp.where(j < b_glu, j - nq + nb,
                                 jnp.where(j < b_sig, j - nq + nb + nc, park_hi)))
        return (l, pl.multiple_of(r0 + jb * tn, 8), 0)

    kern = functools.partial(_inproj_kernel, bounds=(b_silu, b_id2, b_mul, b_glu, b_sig), row_chunk=16)
    return pl.pallas_call(
        kern,
        grid=(T // tm, n_out),
        in_specs=[
            pl.BlockSpec((tm, D), lambda i, j: (i, 0), pipeline_mode=pl.Buffered(1)),
            pl.BlockSpec((1, D), lambda i, j: (0, 0)),
            pl.BlockSpec((pl.Element(1), pl.Element(tn), pl.Element(D)), wa_map),
            pl.BlockSpec((pl.Element(1), pl.Element(tn), pl.Element(D)), wb_map),
            pl.BlockSpec((1, LANES, D), lambda i, j: (l, w_qkv // LANES, 0)),
        ],
        out_specs=[
            pl.BlockSpec((tm, tn), lambda i, j: (i, j)),
            pl.BlockSpec((tm, LANES), lambda i, j: (i, 0)),
        ],
        out_shape=[jax.ShapeDtypeStruct((T, n_out * tn), BF16), jax.ShapeDtypeStruct((T, LANES), F32)],
        scratch_shapes=[pltpu.VMEM((tm, D), BF16)],
        compiler_params=_cp(("arbitrary", "arbitrary")),
        name="inproj",
    )(x, norm_w, w_t, w_t, w_t)


def _conv_taps(xp_scr, base, rows, step, w_ref, width):
    acc = None
    for j in range(width):
        off = base - (width - 1 - j) * step
        term = xp_scr[pl.ds(off, rows), :] * w_ref[j:j + 1, :]
        acc = term if acc is None else acc + term
    return acc


def _conv_taps_by_residue(xp_scr, sh_scr, base, rows, w_ref, width):
    lo = base - (width - 1)
    acc = None
    for rho in range(8):
        offs = [lo + j for j in range(width) if (lo + j) % 8 == rho]
        if not offs:
            continue
        first, last = min(offs) - rho, max(offs) - rho + rows
        if rho:
            sh_scr[first:last, :] = xp_scr[pl.ds(first + rho, last - first), :]
        src = sh_scr if rho else xp_scr
        for off in offs:
            term = src[off - rho:off - rho + rows, :] * w_ref[off - lo:off - lo + 1, :]
            acc = term if acc is None else acc + term
    return acc


def _conv_tile(i, n_s_tiles, x_ref, halo_ref, hist_ref, w_ref, xp_scr, y_scr, width, sh_scr=None):
    tr = x_ref.shape[0]
    hsub = (width - 1) * SUBSET
    nsub = tr // CHUNK

    @pl.when(i < n_s_tiles)
    def _():
        for s in range(nsub):
            xp_scr[0:hsub, :] = hist_ref[s * hsub:(s + 1) * hsub, :]
            xp_scr[hsub:hsub + CHUNK, :] = x_ref[s * CHUNK:(s + 1) * CHUNK, :].astype(F32)
            y_scr[s * CHUNK:(s + 1) * CHUNK, :] = _conv_taps(xp_scr, hsub, CHUNK, SUBSET, w_ref, width)

    @pl.when(i >= n_s_tiles)
    def _():
        halo = halo_ref[...].astype(F32)
        xp_scr[0:HALO, :] = jnp.where(i == n_s_tiles, jnp.zeros_like(halo), halo)
        xp_scr[HALO:HALO + tr, :] = x_ref[...].astype(F32)
        if sh_scr is None:
            y_scr[...] = _conv_taps(xp_scr, HALO, tr, 1, w_ref, width)
        else:
            y_scr[...] = _conv_taps_by_residue(xp_scr, sh_scr, HALO, tr, w_ref, width)


def _softplus(x):
    return jnp.maximum(x, 0.0) + jnp.log(1.0 + jnp.exp(-jnp.abs(x)))


def _conva_kernel(x_ref, halo_ref, hist_ref, w_ref, ab_ref, mask_ref, adt_ref, o_ref, gb_ref, xp_scr, y_scr,
                  *, n_s_tiles, width):
    i = pl.program_id(0)
    sec = pl.program_id(1)
    _conv_tile(i, n_s_tiles, x_ref, halo_ref, hist_ref, w_ref, xp_scr, y_scr, width)

    @pl.when(sec == 0)
    def _():
        ab = ab_ref[...]
        lane = lax.broadcasted_iota(jnp.int32, ab.shape, 1)
        g = -jnp.exp(adt_ref[0:1, :]) * _softplus(ab + adt_ref[1:2, :])
        gb_ref[...] = jnp.where(lane < H_A, g, _sigmoid(ab)) * mask_ref[...]

    scale = jnp.where(sec == 0, DK ** -0.5, 1.0).astype(F32)
    for h in range(H_A):
        hs = slice(h * DK, (h + 1) * DK)
        y = _silu(y_scr[:, hs])
        yn = y * (lax.rsqrt(jnp.sum(y * y, axis=-1, keepdims=True) + EPS) * scale)
        o_ref[:, hs] = jnp.where(sec == 2, y, yn)


def _convb_kernel(x_ref, halo_ref, hist_ref, w_ref, gate_ref, o_ref, xp_scr, y_scr, *, n_s_tiles, width):
    i = pl.program_id(0)
    _conv_tile(i, n_s_tiles, x_ref, halo_ref, hist_ref, w_ref, xp_scr, y_scr, width)
    o_ref[...] = (gate_ref[...].astype(F32) * y_scr[...]).astype(o_ref.dtype)


def _convc_kernel(x_ref, halo_ref, hist_ref, w_ref, cb_ref, lnw_ref, lnb_ref, o_ref, xp_scr, y_scr, sh_scr,
                  *, n_s_tiles, width):
    i = pl.program_id(0)
    _conv_tile(i, n_s_tiles, x_ref, halo_ref, hist_ref, w_ref, xp_scr, y_scr, width, sh_scr)
    y = y_scr[...] + cb_ref[...]
    mu = jnp.mean(y, axis=-1, keepdims=True)
    yc = y - mu
    var = jnp.mean(yc * yc, axis=-1, keepdims=True)
    yn = yc * lax.rsqrt(var + LN_EPS) * lnw_ref[...] + lnb_ref[...]
    o_ref[...] = _silu(yn).astype(o_ref.dtype)


def _conv_specs(tr, cw, width, col_of, n_s_tiles, sec_axis):
    hrows = (tr // CHUNK) * (width - 1) * SUBSET
    hb = tr // HALO
    last_hist = max(n_s_tiles - 1, 0)
    if sec_axis:
        x_map = lambda i, s: (i, col_of(s))
        halo_map = lambda i, s: (jnp.maximum(i * hb - 1, 0), col_of(s))
        hist_map = lambda i, s: (jnp.minimum(i, last_hist), s)
    else:
        x_map = lambda i: (i, col_of(0))
        halo_map = lambda i: (jnp.maximum(i * hb - 1, 0), col_of(0))
        hist_map = lambda i: (jnp.minimum(i, last_hist), 0)
    return [pl.BlockSpec((tr, cw), x_map), pl.BlockSpec((HALO, cw), halo_map), pl.BlockSpec((hrows, cw), hist_map)]


def _dot(a, b):
    return jnp.dot(a.astype(BF16), b.astype(BF16), preferred_element_type=F32)


def _dot_nt(a, b):
    return lax.dot_general(a.astype(BF16), b.astype(BF16), (((1,), (1,)), ((), ())), preferred_element_type=F32)


def _split(x):
    hi = x.astype(BF16)
    return hi, (x - hi.astype(F32)).astype(BF16)


def _mask_dot(mask_bf, x):
    hi, lo = _split(x)
    return (jnp.dot(mask_bf, hi, preferred_element_type=F32)
            + jnp.dot(mask_bf, lo, preferred_element_type=F32))


def _mask_dot_nt(xt, mask_bf):
    hi, lo = _split(xt)
    dn = (((1,), (1,)), ((), ()))
    return (lax.dot_general(hi, mask_bf, dn, preferred_element_type=F32)
            + lax.dot_general(lo, mask_bf, dn, preferred_element_type=F32))


def _bf_mask(m):
    return jnp.where(m, 1.0, 0.0).astype(BF16)


def _chunk_common(q_ref, k_ref, v_ref, gb, gc_all, gr_all, lmask, strict):
    heads = range(H_A)
    hs = [slice(h * DK, (h + 1) * DK) for h in heads]
    q = [q_ref[:, hs[h]] for h in heads]
    k = [k_ref[:, hs[h]] for h in heads]
    v = [v_ref[:, hs[h]] for h in heads]
    gc = [gc_all[:, h:h + 1] for h in heads]
    beta = [gb[:, H_A + h:H_A + h + 1] for h in heads]
    decay = [jnp.exp(jnp.where(lmask, gc[h] - gr_all[h:h + 1, :], NEG)) for h in heads]
    C = q[0].shape[0]
    qkk = [_dot_nt(jnp.concatenate([q[h], k[h]], axis=0), k[h]) for h in heads]
    qk = [qkk[h][:C] * decay[h] for h in heads]
    m = [jnp.where(strict, beta[h] * qkk[h][C:] * decay[h], 0.0) for h in heads]
    eg = [jnp.exp(gc[h]) for h in heads]
    rhs = [jnp.concatenate([beta[h] * v[h], (beta[h] * eg[h]) * k[h]], axis=1) for h in heads]
    return q, k, gc, qk, m, eg, rhs


def _delta_kernel(q_ref, k_ref, v_ref, gb_ref, sga_ref, onw_ref, s0s_ref, ya_ref, ss_ref, sp_ref, s_scr,
                  *, n_s_steps, n_chunks):
    s = pl.program_id(0)
    C = CHUNK
    row = lax.broadcasted_iota(jnp.int32, (C, C), 0)
    col = lax.broadcasted_iota(jnp.int32, (C, C), 1)
    eye = (row == col).astype(F32)
    gb = gb_ref[...]
    gbt = gb.T[0:H_A, :]

    def finish(o, h):
        hs = slice(h * DV, (h + 1) * DV)
        on = o * lax.rsqrt(jnp.mean(o * o, axis=-1, keepdims=True) + EPS) * onw_ref[...]
        ya_ref[:, hs] = (on * sga_ref[:, hs].astype(F32)).astype(ya_ref.dtype)

    heads = range(H_A)

    @pl.when(s < n_s_steps)
    def _():
        same = ((row - col) & (DEC_SEQ - 1)) == 0
        lmask = same & (row >= col)
        strict = same & (row > col)
        lm = _bf_mask(lmask)
        last = _bf_mask(col == (C - DEC_SEQ) + (row & (DEC_SEQ - 1)))
        gc_all = _mask_dot(lm, gb)
        gr_all = _mask_dot_nt(gbt, lm)
        gl_all = _mask_dot(last, gc_all)
        rsub = lax.broadcasted_iota(jnp.int32, (C, 1), 0) & (DEC_SEQ - 1)
        rsub2 = jnp.concatenate([rsub, rsub], axis=0)
        q, k, gc, qk, m, eg, rhs = _chunk_common(q_ref, k_ref, v_ref, gb, gc_all, gr_all, lmask, strict)
        seqs = range(SUBSET)
        m2 = [_dot(m[h], m[h]) for h in heads]
        b1 = [eye - m[h] for h in heads]
        r1 = [_dot(jnp.concatenate([b1[h], m2[h]], axis=0), m2[h]) for h in heads]
        b2 = [b1[h] + r1[h][:C] for h in heads]
        tinv = [b2[h] + _dot(b2[h], r1[h][C:]) for h in heads]
        x = [_dot(tinv[h], rhs[h]) for h in heads]
        lhs = [jnp.concatenate([x[h][:, DV:], q[h]], axis=0) for h in heads]
        s_cat = [jnp.concatenate([s0s_ref[i, h] for i in seqs], axis=1) for h in heads]
        sk_all = [_dot(lhs[h], s_cat[h]) for h in heads]
        sk = []
        for h in heads:
            acc = jnp.zeros((2 * C, DV), F32)
            for i in seqs:
                acc = acc + jnp.where(rsub2 == i, sk_all[h][:, i * DV:(i + 1) * DV], 0.0)
            sk.append(acc)
        u = [x[h][:, :DV] - sk[h][:C] for h in heads]
        kd = [k[h] * jnp.exp(gl_all[:, h:h + 1] - gc[h]) for h in heads]
        kdt = [jnp.concatenate([jnp.where(rsub == i, kd[h], 0.0).T for i in seqs], axis=0) for h in heads]
        r2 = [_dot(jnp.concatenate([qk[h], kdt[h]], axis=0), u[h]) for h in heads]
        for h in heads:
            for i in seqs:
                gli = gc_all[C - DEC_SEQ + i:C - DEC_SEQ + i + 1, h:h + 1]
                ss_ref[i, h] = jnp.exp(gli) * s0s_ref[i, h] + r2[h][C + i * DK:C + (i + 1) * DK]
        for h in heads:
            finish(eg[h] * sk[h][C:] + r2[h][:C], h)

    @pl.when(s >= n_s_steps)
    def _():
        c = (s - n_s_steps) % n_chunks

        @pl.when(c == 0)
        def _():
            s_scr[...] = jnp.zeros_like(s_scr)

        lmask = row >= col
        strict = row > col
        lm = _bf_mask(lmask)
        blk = (row >> 4) == (col >> 4)
        gc_all = _mask_dot(lm, gb)
        gr_all = _mask_dot_nt(gbt, lm)
        q, k, gc, qk, m, eg, rhs = _chunk_common(q_ref, k_ref, v_ref, gb, gc_all, gr_all, lmask, strict)
        nd = [jnp.where(blk, -m[h], 0.0) for h in heads]
        lo = [jnp.where(blk, 0.0, m[h]) for h in heads]
        W2 = 2 * DV
        p2 = [_dot(nd[h], nd[h]) for h in heads]
        a1 = [eye + nd[h] for h in heads]
        r1 = [_dot(jnp.concatenate([a1[h], p2[h]], axis=0), p2[h]) for h in heads]
        t1 = [a1[h] + r1[h][:C] for h in heads]
        r2 = [_dot(jnp.concatenate([t1[h], r1[h][C:]], axis=0), r1[h][C:]) for h in heads]
        t2 = [t1[h] + r2[h][:C] for h in heads]
        dinv = [t2[h] + _dot(t2[h], r2[h][C:]) for h in heads]
        ya = [_dot(dinv[h], jnp.concatenate([rhs[h], lo[h]], axis=1)) for h in heads]
        y = [ya[h][:, :W2] for h in heads]
        a = [ya[h][:, W2:] for h in heads]
        r3 = [_dot(a[h], ya[h]) for h in heads]
        w = [y[h] - r3[h][:, :W2] for h in heads]
        x = [w[h] + _dot(r3[h][:, W2:], w[h]) for h in heads]
        S = [s_scr[h] for h in heads]
        sk = [_dot(jnp.concatenate([x[h][:, DV:], q[h]], axis=0), S[h]) for h in heads]
        u = [x[h][:, :DV] - sk[h][:C] for h in heads]
        gl = [gc_all[C - 1:C, h:h + 1] for h in heads]
        kd = [k[h] * jnp.exp(gl[h] - gc[h]) for h in heads]
        r4 = [_dot(jnp.concatenate([qk[h], kd[h].T], axis=0), u[h]) for h in heads]
        for h in heads:
            s_scr[h] = jnp.exp(gl[h]) * S[h] + r4[h][C:]
        for h in heads:
            finish(eg[h] * sk[h][C:] + r4[h][:C], h)

        @pl.when(c == n_chunks - 1)
        def _():
            sp_ref[0] = s_scr[...]


def _delta(qkv, gb, z, onorm_w, state_all, l, *, n_s_steps, n_batch, n_chunks, sga_col):
    T = qkv.shape[0]
    W = H_A * DK
    n_steps = n_s_steps + n_batch * n_chunks
    last_s = max(n_s_steps - 1, 0)
    kern = functools.partial(_delta_kernel, n_s_steps=n_s_steps, n_chunks=n_chunks)
    return pl.pallas_call(
        kern,
        grid=(n_steps,),
        in_specs=[
            pl.BlockSpec((CHUNK, W), lambda s: (s, 0)),
            pl.BlockSpec((CHUNK, W), lambda s: (s, 1)),
            pl.BlockSpec((CHUNK, W), lambda s: (s, 2)),
            pl.BlockSpec((CHUNK, LANES), lambda s: (s, 0)),
            pl.BlockSpec((CHUNK, W), lambda s: (s, sga_col)),
            pl.BlockSpec((1, DV), lambda s: (0, 0)),
            pl.BlockSpec((None, SUBSET, H_A, DK, DV), lambda s: (l, jnp.minimum(s, last_s), 0, 0, 0)),
        ],
        out_specs=[
            pl.BlockSpec((CHUNK, W), lambda s: (s, 0)),
            pl.BlockSpec((SUBSET, H_A, DK, DV), lambda s: (jnp.minimum(s, last_s), 0, 0, 0)),
            pl.BlockSpec((1, H_A, DK, DV), lambda s: (jnp.maximum(s - n_s_steps, 0) // n_chunks, 0, 0, 0)),
        ],
        out_shape=[
            jax.ShapeDtypeStruct((T, W), BF16),
            jax.ShapeDtypeStruct(state_all.shape[1:], F32),
            jax.ShapeDtypeStruct((n_batch, H_A, DK, DV), F32),
        ],
        scratch_shapes=[pltpu.VMEM((H_A, DK, DV), F32)],
        compiler_params=_cp(("arbitrary",)),
        name="delta",
    )(qkv, qkv, qkv, gb, z, onorm_w, state_all)


def _merge_kernel(ya_ref, yb_ref, yc_ref, wa_ref, wb_ref, wc_ref, g0_ref, g1_ref, g2_ref, m_ref):
    m = g0_ref[...].astype(F32) * jnp.dot(ya_ref[...], wa_ref[...], preferred_element_type=F32)
    m = m + g1_ref[...].astype(F32) * jnp.dot(yb_ref[...], wb_ref[...], preferred_element_type=F32)
    m = m + g2_ref[...].astype(F32) * jnp.dot(yc_ref[...], wc_ref[...], preferred_element_type=F32)
    m_ref[...] = m.astype(m_ref.dtype)


def _merge(ya, yb, yc, wa, wb, wc, z, *, gs_col, tm, tn):
    T, K = ya.shape
    D = wa.shape[1]
    nb = D // tn
    g0 = gs_col // tn
    act = pl.BlockSpec((tm, K), lambda i, j: (i, 0))
    wsp = pl.BlockSpec((K, tn), lambda i, j: (0, j))
    return pl.pallas_call(
        _merge_kernel,
        grid=(T // tm, nb),
        in_specs=[act, act, act, wsp, wsp, wsp,
                  pl.BlockSpec((tm, tn), lambda i, j: (i, g0 + j)),
                  pl.BlockSpec((tm, tn), lambda i, j: (i, g0 + nb + j)),
                  pl.BlockSpec((tm, tn), lambda i, j: (i, g0 + 2 * nb + j))],
        out_specs=pl.BlockSpec((tm, tn), lambda i, j: (i, j)),
        out_shape=jax.ShapeDtypeStruct((T, D), BF16),
        compiler_params=_cp(("arbitrary", "arbitrary")),
        name="merge",
    )(ya, yb, yc, wa, wb, wc, z, z, z)


def _oproj_kernel(m_ref, w_ref, x_ref, o_ref):
    o_ref[...] = x_ref[...] + jnp.dot(m_ref[...], w_ref[...], preferred_element_type=F32)


def _oproj(m, w_o, x, *, tm, tn):
    T, D = x.shape
    return pl.pallas_call(
        _oproj_kernel,
        grid=(T // tm, D // tn),
        in_specs=[pl.BlockSpec((tm, D), lambda i, j: (i, 0)),
                  pl.BlockSpec((D, tn), lambda i, j: (0, j)),
                  pl.BlockSpec((tm, tn), lambda i, j: (i, j))],
        out_specs=pl.BlockSpec((tm, tn), lambda i, j: (i, j)),
        out_shape=jax.ShapeDtypeStruct((T, D), F32),
        compiler_params=_cp(("arbitrary", "arbitrary")),
        name="oproj",
    )(m, w_o, x)


def _router_kernel(x_ref, nw_ref, wr_ref, br_ref, ri_ref, rw_ref, cnt_ref, carry_scr, *, n_groups, per_group):
    i = pl.program_id(0)
    tr = x_ref.shape[0]

    @pl.when(i == 0)
    def _():
        carry_scr[...] = jnp.zeros_like(carry_scr)

    x = x_ref[...]
    h = x * lax.rsqrt(jnp.mean(x * x, axis=-1, keepdims=True) + EPS) * nw_ref[...]
    logits = _dot(h, wr_ref[...]) + br_ref[...]
    lane = lax.broadcasted_iota(jnp.int32, logits.shape, 1)
    big = jnp.int32(1 << 20)

    def argmax_first(vals):
        m = jnp.max(vals, axis=-1, keepdims=True)
        idx = jnp.min(jnp.where(vals == m, lane, big), axis=-1, keepdims=True)
        return m, idx

    is_g = lane < n_groups
    mg, gsel = argmax_first(jnp.where(is_g, logits, NEG))
    pg = 1.0 / jnp.sum(jnp.where(is_g, jnp.exp(logits - mg), 0.0), axis=-1, keepdims=True)
    lo = n_groups + gsel * per_group
    in_group = (lane >= lo) & (lane < lo + per_group)
    le = jnp.where(in_group, logits, NEG)
    m1, i1 = argmax_first(le)
    m2, i2 = argmax_first(jnp.where(lane == i1, NEG, le))
    e21 = jnp.exp(m2 - m1)
    w1 = pg / (1.0 + e21)
    w2 = w1 * e21

    oh = ((lane == i1) | (lane == i2)).astype(F32)
    r_i = lax.broadcasted_iota(jnp.int32, (tr, tr), 0)
    c_i = lax.broadcasted_iota(jnp.int32, (tr, tr), 1)
    before = _dot((r_i > c_i).astype(F32), oh) + carry_scr[...]
    rank1 = jnp.sum(jnp.where(lane == i1, before, 0.0), axis=-1, keepdims=True)
    rank2 = jnp.sum(jnp.where(lane == i2, before, 0.0), axis=-1, keepdims=True)
    carry_scr[...] = carry_scr[...] + jnp.sum(oh, axis=0, keepdims=True)
    cnt_ref[...] = jnp.broadcast_to(carry_scr[...], cnt_ref.shape)

    ri = jnp.where(lane == 0, i1 - n_groups,
                   jnp.where(lane == 1, i2 - n_groups,
                             jnp.where(lane == 2, rank1.astype(jnp.int32),
                                       jnp.where(lane == 3, rank2.astype(jnp.int32), 0))))
    ri_ref[...] = ri
    rw_ref[...] = jnp.where(lane == 0, w1, jnp.where(lane == 1, w2, 0.0))


def _router(x, norm_w, w_r, b_r, *, n_groups, per_group, tr):
    T, D = x.shape
    kern = functools.partial(_router_kernel, n_groups=n_groups, per_group=per_group)
    return pl.pallas_call(
        kern,
        grid=(T // tr,),
        in_specs=[pl.BlockSpec((tr, D), lambda i: (i, 0)),
                  pl.BlockSpec((1, D), lambda i: (0, 0)),
                  pl.BlockSpec((D, LANES), lambda i: (0, 0)),
                  pl.BlockSpec((1, LANES), lambda i: (0, 0))],
        out_specs=[pl.BlockSpec((tr, LANES), lambda i: (i, 0)),
                   pl.BlockSpec((tr, LANES), lambda i: (i, 0)),
                   pl.BlockSpec((8, LANES), lambda i: (0, 0))],
        out_shape=[jax.ShapeDtypeStruct((T, LANES), jnp.int32),
                   jax.ShapeDtypeStruct((T, LANES), F32),
                   jax.ShapeDtypeStruct((8, LANES), F32)],
        scratch_shapes=[pltpu.VMEM((1, LANES), F32)],
        compiler_params=_cp(("arbitrary",)),
        name="router",
    )(x, norm_w, w_r, b_r)


def _row_copy(src, src_row, dst, dst_row, sem):
    return pltpu.make_async_copy(src.at[pl.ds(src_row, 1), :], dst.at[pl.ds(dst_row, 1), :], sem)


def _dispatch_kernel(e1_ref, r1_ref, e2_ref, r2_ref, off_ref, pad_ref, npad_ref, nu_ref, x_ref, nw_ref, xs_ref,
                     h_scr, z_scr, sem):
    i = pl.program_id(0)
    tr = x_ref.shape[0]
    n_experts = pad_ref.shape[0]
    tg = z_scr.shape[0]

    @pl.when(i == 0)
    def _():
        z_scr[...] = jnp.zeros_like(z_scr)
        n_all = xs_ref.shape[0] // tg

        def fill(start, rows):
            dst = xs_ref.at[pl.ds(pl.multiple_of(start, 8), rows), :]
            return pltpu.make_async_copy(z_scr.at[pl.ds(0, rows), :], dst, sem.at[4])

        def pad_fills(act):
            def per_expert(e, c):
                def piece(k, c2):
                    act(fill(pad_ref[e] + k * 8, 8))
                    return c2
                return lax.fori_loop(0, npad_ref[e], piece, c)
            lax.fori_loop(0, n_experts, per_expert, 0)

        def tail_fills(act):
            def tile(k, c):
                act(fill(k * tg, tg))
                return c
            lax.fori_loop(nu_ref[0], n_all, tile, 0)

        pad_fills(lambda cp: cp.start())
        tail_fills(lambda cp: cp.start())
        pad_fills(lambda cp: cp.wait())
        tail_fills(lambda cp: cp.wait())

    slot = i % 2
    src = h_scr.at[slot]
    x = x_ref[...]
    h_scr[slot] = x * lax.rsqrt(jnp.mean(x * x, axis=-1, keepdims=True) + EPS) * nw_ref[...]
    base = i * tr

    def issue(r, c):
        t = base + r
        _row_copy(src, r, xs_ref, off_ref[e1_ref[t]] + r1_ref[t], sem.at[2 * slot]).start()
        _row_copy(src, r, xs_ref, off_ref[e2_ref[t]] + r2_ref[t], sem.at[2 * slot + 1]).start()
        return c
    lax.fori_loop(0, tr, issue, 0, unroll=ISSUE_UNROLL)

    def wait_slot(s):
        pltpu.make_async_copy(h_scr.at[s], xs_ref.at[pl.ds(0, tr), :], sem.at[2 * s]).wait()
        pltpu.make_async_copy(h_scr.at[s], xs_ref.at[pl.ds(0, tr), :], sem.at[2 * s + 1]).wait()

    @pl.when(i > 0)
    def _():
        wait_slot(1 - slot)

    @pl.when(i == pl.num_programs(0) - 1)
    def _():
        wait_slot(slot)


def _dispatch(route, pad_start, pad_groups, n_used, x, norm_w, *, n_rows, tr, tg):
    T, D = x.shape
    n_pre = len(route) + 3
    grid_spec = pltpu.PrefetchScalarGridSpec(
        num_scalar_prefetch=n_pre,
        grid=(T // tr,),
        in_specs=[pl.BlockSpec((tr, D), lambda i, *_: (i, 0)),
                  pl.BlockSpec((1, D), lambda i, *_: (0, 0))],
        out_specs=pl.BlockSpec(memory_space=pl.ANY),
        scratch_shapes=[pltpu.VMEM((2, tr, D), F32), pltpu.VMEM((tg, D), F32), pltpu.SemaphoreType.DMA((5,))],
    )
    return pl.pallas_call(
        _dispatch_kernel,
        grid_spec=grid_spec,
        out_shape=jax.ShapeDtypeStruct((n_rows, D), F32),
        compiler_params=_cp(("arbitrary",)),
        name="dispatch",
    )(*route, pad_start, pad_groups, n_used, x, norm_w)


def _expert_kernel(te_ref, nu_ref, xs_ref, wg_ref, wu_ref, wd_ref, o_ref, wg_scr, wu_scr, wd_scr):
    i = pl.program_id(0)
    prev = te_ref[jnp.maximum(i - 1, 0)]

    @pl.when((i == 0) | (te_ref[i] != prev))
    def _():
        wg_scr[...] = wg_ref[...].astype(BF16)
        wu_scr[...] = wu_ref[...].astype(BF16)
        wd_scr[...] = wd_ref[...].astype(BF16)

    @pl.when(i < nu_ref[0])
    def _():
        h = xs_ref[...].astype(BF16)
        g = jnp.dot(h, wg_scr[...], preferred_element_type=F32)
        u = jnp.dot(h, wu_scr[...], preferred_element_type=F32)
        a = (_silu(g) * u).astype(BF16)
        o_ref[...] = jnp.dot(a, wd_scr[...], preferred_element_type=F32)

    @pl.when(i >= nu_ref[0])
    def _():
        o_ref[...] = jnp.zeros_like(o_ref)


def _experts(tile_e, n_used, xs, w_gate, w_up, w_down, l, *, n_tiles, tg):
    D = xs.shape[1]
    F = w_gate.shape[-1]

    def xs_map(i, te, nu):
        return (jnp.maximum(jnp.minimum(i, nu[0] - 1), 0), 0)

    grid_spec = pltpu.PrefetchScalarGridSpec(
        num_scalar_prefetch=2,
        grid=(n_tiles,),
        in_specs=[pl.BlockSpec((tg, D), xs_map),
                  pl.BlockSpec((None, None, D, F), lambda i, te, nu: (l, te[i], 0, 0)),
                  pl.BlockSpec((None, None, D, F), lambda i, te, nu: (l, te[i], 0, 0)),
                  pl.BlockSpec((None, None, F, D), lambda i, te, nu: (l, te[i], 0, 0))],
        out_specs=pl.BlockSpec((tg, D), lambda i, te, nu: (i, 0)),
        scratch_shapes=[pltpu.VMEM((D, F), BF16), pltpu.VMEM((D, F), BF16), pltpu.VMEM((F, D), BF16)],
    )
    return pl.pallas_call(
        _expert_kernel,
        grid_spec=grid_spec,
        out_shape=jax.ShapeDtypeStruct((n_tiles * tg, D), F32),
        compiler_params=_cp(("arbitrary",)),
        name="experts",
    )(tile_e, n_used, xs, w_gate, w_up, w_down)


def _combine_kernel(e1_ref, r1_ref, e2_ref, r2_ref, off_ref, x_ref, rw_ref, mask_ref, fw_ref, eo_ref, out_ref,
                    b_scr, sem, *, final):
    i = pl.program_id(0)
    tr = x_ref.shape[0]
    slot = i % 2

    def gather_tile(tile, s):
        base = tile * tr

        def issue(r, c):
            t = base + r
            _row_copy(eo_ref, off_ref[e1_ref[t]] + r1_ref[t], b_scr.at[s, 0], r, sem.at[2 * s]).start()
            _row_copy(eo_ref, off_ref[e2_ref[t]] + r2_ref[t], b_scr.at[s, 1], r, sem.at[2 * s + 1]).start()
            return c
        lax.fori_loop(0, tr, issue, 0, unroll=ISSUE_UNROLL)

    @pl.when(i == 0)
    def _():
        gather_tile(0, 0)

    @pl.when(i + 1 < pl.num_programs(0))
    def _():
        gather_tile(i + 1, 1 - slot)

    pltpu.make_async_copy(eo_ref.at[pl.ds(0, tr), :], b_scr.at[slot, 0], sem.at[2 * slot]).wait()
    pltpu.make_async_copy(eo_ref.at[pl.ds(0, tr), :], b_scr.at[slot, 1], sem.at[2 * slot + 1]).wait()
    rw = rw_ref[...]
    x2 = (x_ref[...] + rw[:, 0:1] * b_scr[slot, 0] + rw[:, 1:2] * b_scr[slot, 1]) * mask_ref[:, 0:1]
    if final:
        out_ref[...] = x2 * lax.rsqrt(jnp.mean(x2 * x2, axis=-1, keepdims=True) + EPS) * fw_ref[...]
    else:
        out_ref[...] = x2


def _combine(route, x, rw, mask, final_w, eo, *, tr, final):
    T, D = x.shape
    row = lambda w: pl.BlockSpec((tr, w), lambda i, *_: (i, 0))
    grid_spec = pltpu.PrefetchScalarGridSpec(
        num_scalar_prefetch=len(route),
        grid=(T // tr,),
        in_specs=[row(D), row(LANES), row(LANES),
                  pl.BlockSpec((1, D), lambda i, *_: (0, 0)),
                  pl.BlockSpec(memory_space=pl.ANY)],
        out_specs=row(D),
        scratch_shapes=[pltpu.VMEM((2, 2, tr, D), F32), pltpu.SemaphoreType.DMA((4,))],
    )
    return pl.pallas_call(
        functools.partial(_combine_kernel, final=final),
        grid_spec=grid_spec,
        out_shape=jax.ShapeDtypeStruct((T, D), F32),
        compiler_params=_cp(("arbitrary",)),
        name="combine",
    )(*route, x, rw, mask, final_w, eo)


def _sample_to_rows(a):
    nb, L, C = a.shape
    return a.reshape(nb // SUBSET, SUBSET, L, C).transpose(0, 2, 1, 3).reshape(nb * L, C)


def _rows_to_sample(r, L):
    n, C = r.shape
    nb = n // L
    return r.reshape(nb // SUBSET, L, SUBSET, C).transpose(0, 2, 1, 3).reshape(nb, L, C)


def kernel(x_prompt, x_sample, state_delta, state_conv_a, state_conv_b, state_conv_c, meta_tokens, norm1_w, w_in, conv_a_w, a_log, dt_bias, onorm_a_w, w_out_a, conv_b_w, w_out_b, conv_c_w, conv_c_b, ln_c_w, ln_c_b, w_out_c, w_o, norm2_w, w_rg, b_rg, w_re, b_re, w_gate_e, w_up_e, w_down_e, final_norm_w):
    B, SEQ, D = x_prompt.shape
    NB, L, _ = x_sample.shape
    depth = w_in.shape[0]
    W_QKV = conv_a_w.shape[-1]
    W_V = H_A * DV
    W_B = conv_b_w.shape[-1]
    W_C = conv_c_w.shape[-1]
    CA, CB, CC = conv_a_w.shape[1], conv_b_w.shape[1], conv_c_w.shape[1]
    G = w_rg.shape[-1]
    E = w_re.shape[-1]
    assert L == DEC_SEQ and NB % SUBSET == 0 and SEQ % CHUNK == 0 and W_QKV == 3 * W_V
    assert W_V == W_B == W_C and max(CA, CB, CC) - 1 <= min(HALO, CHUNK - N_META)
    assert G + E <= LANES and D % LANES == 0

    LP = CHUNK + SEQ
    lead = CHUNK - N_META
    Ts, Tp = NB * L, B * LP
    T = Ts + Tp
    n_chunks = LP // CHUNK
    n_s_steps = Ts // CHUNK

    tr = _pick_tile(_gcd(Ts, Tp), 256, CHUNK)
    tm = _pick_tile(T, 1184, 16)
    tn = 512
    tg = 256
    n_s_tiles = Ts // tr

    dt_ = x_prompt.dtype
    lead_rows = jnp.concatenate([jnp.zeros((lead, D), dt_), meta_tokens.astype(dt_)], axis=0)
    pieces = [_sample_to_rows(x_sample)]
    for b in range(B):
        pieces += [lead_rows, x_prompt[b]]
    x = jnp.concatenate(pieces, axis=0)
    pos_in_seq = jnp.arange(Tp, dtype=jnp.int32) % LP
    real = jnp.concatenate([jnp.ones((Ts,), F32), (pos_in_seq >= lead).astype(F32)])
    mask = jnp.broadcast_to(real[:, None], (T, LANES))

    n_ab = 2 * H_A
    w_t = jnp.swapaxes(w_in, 1, 2)
    w_r = jnp.pad(jnp.concatenate([w_rg, w_re], axis=2), ((0, 0), (0, 0), (0, LANES - G - E)))
    b_r = jnp.pad(jnp.concatenate([b_rg, b_re], axis=1), ((0, 0), (0, LANES - G - E)))[:, None, :]
    adt = jnp.pad(jnp.stack([a_log, dt_bias], axis=1), ((0, 0), (0, 0), (0, LANES - H_A)))
    adt = jnp.pad(adt, ((0, 0), (0, 6), (0, 0)))
    woa, wob, woc, wo = (w.astype(BF16) for w in (w_out_a, w_out_b, w_out_c, w_o))

    col_sga, col_gb, col_u, col_gl, col_gs = W_QKV, W_QKV + W_V, W_QKV + W_V + W_B, W_QKV + W_V + 2 * W_B, W_QKV + W_V + 2 * W_B + W_C
    cw = W_V
    P = TOP_K * T + E * tg
    n_tiles = P // tg

    dp, ap, bpl, cpl, ds, as_, bs, cs = [], [], [], [], [], [], [], []
    for l in range(depth):
        z, ab = _inproj(x, norm1_w[l][None], w_t, l,
                        w_qkv=W_QKV, n_ab=n_ab, w_v=W_V, w_b=W_B, w_c=W_C, tm=tm, tn=tn)

        def hist_rows(st):
            nb, hw, C = st.shape
            return st.reshape(nb // SUBSET, SUBSET, hw, C).transpose(0, 2, 1, 3).reshape(nb * hw, C)
        hist_a, hist_b, hist_c = hist_rows(state_conv_a[l]), hist_rows(state_conv_b[l]), hist_rows(state_conv_c[l])

        xs_, hs_, hi_ = _conv_specs(tr, cw, CA, lambda s: s, n_s_tiles, True)
        qkv, gb = pl.pallas_call(
            functools.partial(_conva_kernel, n_s_tiles=n_s_tiles, width=CA),
            grid=(T // tr, W_QKV // cw),
            in_specs=[xs_, hs_, hi_,
                      pl.BlockSpec((CA, cw), lambda i, s: (0, s)),
                      pl.BlockSpec((tr, LANES), lambda i, s: (i, 0)),
                      pl.BlockSpec((tr, LANES), lambda i, s: (i, 0)),
                      pl.BlockSpec((8, LANES), lambda i, s: (0, 0))],
            out_specs=[pl.BlockSpec((tr, cw), lambda i, s: (i, s)),
                       pl.BlockSpec((tr, LANES), lambda i, s: (i, 0))],
            out_shape=[jax.ShapeDtypeStruct((T, W_QKV), F32), jax.ShapeDtypeStruct((T, LANES), F32)],
            scratch_shapes=[pltpu.VMEM((HALO + tr, cw), F32), pltpu.VMEM((tr, cw), F32)],
            compiler_params=_cp(("arbitrary", "arbitrary")),
            name="conv_a",
        )(z, z, hist_a, conv_a_w[l], ab, mask, adt[l])

        ya, s_s, s_p = _delta(qkv, gb, z, onorm_a_w[l][None], state_delta, l,
                              n_s_steps=n_s_steps, n_batch=B, n_chunks=n_chunks, sga_col=col_sga // cw)

        xs_, hs_, hi_ = _conv_specs(tr, cw, CB, lambda s: col_u // cw, n_s_tiles, False)
        yb = pl.pallas_call(
            functools.partial(_convb_kernel, n_s_tiles=n_s_tiles, width=CB),
            grid=(T // tr,),
            in_specs=[xs_, hs_, hi_,
                      pl.BlockSpec((CB, cw), lambda i: (0, 0)),
                      pl.BlockSpec((tr, cw), lambda i: (i, col_gb // cw))],
            out_specs=pl.BlockSpec((tr, cw), lambda i: (i, 0)),
            out_shape=jax.ShapeDtypeStruct((T, cw), BF16),
            scratch_shapes=[pltpu.VMEM((HALO + tr, cw), F32), pltpu.VMEM((tr, cw), F32)],
            compiler_params=_cp(("arbitrary",)),
            name="conv_b",
        )(z, z, hist_b, conv_b_w[l], z)

        xs_, hs_, hi_ = _conv_specs(tr, cw, CC, lambda s: col_gl // cw, n_s_tiles, False)
        hc_rows = max(HALO + tr, (CC - 1) * SUBSET + CHUNK)
        yc = pl.pallas_call(
            functools.partial(_convc_kernel, n_s_tiles=n_s_tiles, width=CC),
            grid=(T // tr,),
            in_specs=[xs_, hs_, hi_,
                      pl.BlockSpec((CC, cw), lambda i: (0, 0)),
                      pl.BlockSpec((1, cw), lambda i: (0, 0)),
                      pl.BlockSpec((1, cw), lambda i: (0, 0)),
                      pl.BlockSpec((1, cw), lambda i: (0, 0))],
            out_specs=pl.BlockSpec((tr, cw), lambda i: (i, 0)),
            out_shape=jax.ShapeDtypeStruct((T, cw), BF16),
            scratch_shapes=[pltpu.VMEM((hc_rows, cw), F32), pltpu.VMEM((tr, cw), F32),
                            pltpu.VMEM((HALO + tr, cw), F32)],
            compiler_params=_cp(("arbitrary",)),
            name="conv_c",
        )(z, z, hist_c, conv_c_w[l], conv_c_b[l][None], ln_c_w[l][None], ln_c_b[l][None])

        m = _merge(ya, yb, yc, woa[l], wob[l], woc[l], z, gs_col=col_gs, tm=tm, tn=tn)
        x1 = _oproj(m, wo[l], x, tm=tm, tn=tn)

        ri, rw, cnt = _router(x1, norm2_w[l][None], w_r[l], b_r[l], n_groups=G, per_group=E // G, tr=tr)
        counts = cnt[0, G:G + E].astype(jnp.int32)
        padded = ((counts + tg - 1) // tg) * tg
        ends = jnp.cumsum(padded)
        offs = ends - padded
        route = (ri[:, 0], ri[:, 2], ri[:, 1], ri[:, 3], offs.astype(jnp.int32))
        n_used = (ends[-1] // tg).astype(jnp.int32)
        tile_start = jnp.arange(n_tiles, dtype=jnp.int32) * tg
        tile_e = jnp.sum((tile_start[:, None] >= ends[None, :]).astype(jnp.int32), axis=1)
        last_e = jnp.sum((((n_used - 1) * tg) >= ends).astype(jnp.int32))
        tile_e = jnp.where(jnp.arange(n_tiles) < n_used, tile_e, last_e).astype(jnp.int32)

        pad_start = (((offs + counts) // 8) * 8).astype(jnp.int32)
        n_used = n_used.reshape(1)
        pad_groups = ((ends - pad_start) // 8).astype(jnp.int32)
        xs = _dispatch(route, pad_start, pad_groups, n_used, x1, norm2_w[l][None], n_rows=P, tr=tr, tg=tg)
        eo = _experts(tile_e, n_used, xs, w_gate_e, w_up_e, w_down_e, l, n_tiles=n_tiles, tg=tg)
        x = _combine(route, x1, rw, mask, final_norm_w[None], eo, tr=tr, final=(l == depth - 1))

        def prompt_tail(c0, c1, width):
            return jnp.stack([lax.slice(z, (Ts + (b + 1) * LP - (width - 1), c0), (Ts + (b + 1) * LP, c1))
                              for b in range(B)]).astype(dt_)

        def sample_hist(st, c0, c1):
            seq = _rows_to_sample(lax.slice(z, (0, c0), (Ts, c1)), L).astype(dt_)
            return jnp.concatenate([st.astype(dt_), seq], axis=1)[:, L:]
        dp.append(s_p)
        ap.append(prompt_tail(0, W_QKV, CA))
        bpl.append(prompt_tail(col_u, col_u + W_B, CB))
        cpl.append(prompt_tail(col_gl, col_gl + W_C, CC))
        ds.append(s_s)
        as_.append(sample_hist(state_conv_a[l], 0, W_QKV))
        bs.append(sample_hist(state_conv_b[l], col_u, col_u + W_B))
        cs.append(sample_hist(state_conv_c[l], col_gl, col_gl + W_C))

    y_sample = _rows_to_sample(lax.slice(x, (0, 0), (Ts, D)), L)
    y_prompt = jnp.stack([lax.slice(x, (Ts + b * LP + CHUNK, 0), (Ts + (b + 1) * LP, D)) for b in range(B)])
    return (y_prompt, y_sample, jnp.stack(dp), jnp.stack(ap), jnp.stack(bpl), jnp.stack(cpl),
            jnp.stack(ds), jnp.stack(as_), jnp.stack(bs), jnp.stack(cs))


def _gcd(a, b):
    while b:
        a, b = b, a % b
    return a
```

```python
import functools

import jax
import jax.numpy as jnp
from jax import lax
from jax.experimental import pallas as pl
from jax.experimental.pallas import tpu as pltpu

F32 = jnp.float32
BF16 = jnp.bfloat16

EPS = 1e-6
LN_EPS = 1e-5
N_META = 16
H_A = 8
DK = 128
DV = 128
CHUNK = 64
LANES = 128
DEC_SEQ = 8
SUBSET = CHUNK // DEC_SEQ
HALO = 32
TOP_K = 2
NEG = -1e30
ISSUE_UNROLL = 8

VMEM_LIMIT = 52 * 1024 * 1024


def _cp(dims, vmem=VMEM_LIMIT):
    return pltpu.CompilerParams(dimension_semantics=dims, vmem_limit_bytes=vmem)


def _sigmoid(x):
    return 0.5 * jnp.tanh(0.5 * x) + 0.5


def _silu(x):
    return x * _sigmoid(x)


def _pick_tile(n, target, mult):
    best = None
    for t in range(mult, min(n, target) + 1, mult):
        if n % t == 0:
            best = t
    assert best is not None, (n, target, mult)
    return best


def _inproj_kernel(x_ref, nw_ref, wa_ref, wb_ref, wab_ref, z_ref, ab_ref, h_scr, *, bounds, row_chunk):
    j = pl.program_id(1)
    tm = x_ref.shape[0]

    def proj(w_ref):
        return lax.dot_general(h_scr[...], w_ref[0].astype(BF16), (((1,), (1,)), ((), ())),
                               preferred_element_type=F32)

    @pl.when(j == 0)
    def _():
        def body(r, c):
            rs = pl.ds(pl.multiple_of(r * row_chunk, row_chunk), row_chunk)
            x = x_ref[rs, :]
            h = x * lax.rsqrt(jnp.mean(x * x, axis=-1, keepdims=True) + EPS) * nw_ref[...]
            h_scr[rs, :] = h.astype(BF16)
            return c
        lax.fori_loop(0, tm // row_chunk, body, 0)
        ab_ref[...] = proj(wab_ref)

    b_silu, b_id2, b_mul, b_glu, b_sig = bounds

    def put(v):
        z_ref[...] = v.astype(z_ref.dtype)

    @pl.when((j < b_silu) | ((j >= b_id2) & (j < b_mul)))
    def _():
        put(proj(wa_ref))

    @pl.when((j >= b_silu) & (j < b_id2))
    def _():
        put(_silu(proj(wa_ref)))

    @pl.when((j >= b_mul) & (j < b_glu))
    def _():
        put(proj(wa_ref) * proj(wb_ref))

    @pl.when((j >= b_glu) & (j < b_sig))
    def _():
        put(proj(wa_ref) * _sigmoid(proj(wb_ref)))

    @pl.when(j >= b_sig)
    def _():
        put(_sigmoid(proj(wa_ref)))


def _inproj(x, norm_w, w_t, l, *, w_qkv, n_ab, w_v, w_b, w_c, tm, tn):
    T, D = x.shape
    nq, nv, nb, nc = w_qkv // tn, w_v // tn, w_b // tn, w_c // tn
    r0 = w_qkv + n_ab
    b_silu = nq
    b_id2 = nq + nv
    b_mul = b_id2 + nb
    b_glu = b_mul + nb
    b_sig = b_glu + nc
    n_out = nq + (w_t.shape[1] - r0) // tn - nb - nc
    park_lo = b_glu - nq
    park_hi = b_sig - nq + nb + nc - 1
    assert w_qkv % LANES == 0 and r0 % 8 == 0 and (w_t.shape[1] - r0) % tn == 0

    def wa_map(i, j):
        rest = j - nq + jnp.where(j >= b_glu, nb, 0) + jnp.where(j >= b_sig, nc, 0)
        return (l, pl.multiple_of(jnp.where(j < nq, j * tn, r0 + rest * tn), 8), 0)

    def wb_map(i, j):
        jb = jnp.where(j < b_mul, park_lo,
                       jnp.where(j < b_glu, j - nq + nb,
                                 jnp.where(j < b_sig, j - nq + nb + nc, park_hi)))
        return (l, pl.multiple_of(r0 + jb * tn, 8), 0)

    kern = functools.partial(_inproj_kernel, bounds=(b_silu, b_id2, b_mul, b_glu, b_sig), row_chunk=16)
    return pl.pallas_call(
        kern,
        grid=(T // tm, n_out),
        in_specs=[
            pl.BlockSpec((tm, D), lambda i, j: (i, 0), pipeline_mode=pl.Buffered(1)),
            pl.BlockSpec((1, D), lambda i, j: (0, 0)),
            pl.BlockSpec((pl.Element(1), pl.Element(tn), pl.Element(D)), wa_map),
            pl.BlockSpec((pl.Element(1), pl.Element(tn), pl.Element(D)), wb_map),
            pl.BlockSpec((1, LANES, D), lambda i, j: (l, w_qkv // LANES, 0)),
        ],
        out_specs=[
            pl.BlockSpec((tm, tn), lambda i, j: (i, j)),
            pl.BlockSpec((tm, LANES), lambda i, j: (i, 0)),
        ],
        out_shape=[jax.ShapeDtypeStruct((T, n_out * tn), BF16), jax.ShapeDtypeStruct((T, LANES), F32)],
        scratch_shapes=[pltpu.VMEM((tm, D), BF16)],
        compiler_params=_cp(("arbitrary", "arbitrary")),
        name="inproj",
    )(x, norm_w, w_t, w_t, w_t)


def _conv_taps(xp_scr, base, rows, step, w_ref, width):
    acc = None
    for j in range(width):
        off = base - (width - 1 - j) * step
        term = xp_scr[pl.ds(off, rows), :] * w_ref[j:j + 1, :]
        acc = term if acc is None else acc + term
    return acc


def _conv_taps_by_residue(xp_scr, sh_scr, base, rows, w_ref, width):
    lo = base - (width - 1)
    acc = None
    for rho in range(8):
        offs = [lo + j for j in range(width) if (lo + j) % 8 == rho]
        if not offs:
            continue
        first, last = min(offs) - rho, max(offs) - rho + rows
        if rho:
            sh_scr[first:last, :] = xp_scr[pl.ds(first + rho, last - first), :]
        src = sh_scr if rho else xp_scr
        for off in offs:
            term = src[off - rho:off - rho + rows, :] * w_ref[off - lo:off - lo + 1, :]
            acc = term if acc is None else acc + term
    return acc


def _conv_tile(i, n_s_tiles, x_ref, halo_ref, hist_ref, w_ref, xp_scr, y_scr, width, sh_scr=None):
    tr = x_ref.shape[0]
    hsub = (width - 1) * SUBSET
    nsub = tr // CHUNK

    @pl.when(i < n_s_tiles)
    def _():
        for s in range(nsub):
            xp_scr[0:hsub, :] = hist_ref[s * hsub:(s + 1) * hsub, :]
            xp_scr[hsub:hsub + CHUNK, :] = x_ref[s * CHUNK:(s + 1) * CHUNK, :].astype(F32)
            y_scr[s * CHUNK:(s + 1) * CHUNK, :] = _conv_taps(xp_scr, hsub, CHUNK, SUBSET, w_ref, width)

    @pl.when(i >= n_s_tiles)
    def _():
        halo = halo_ref[...].astype(F32)
        xp_scr[0:HALO, :] = jnp.where(i == n_s_tiles, jnp.zeros_like(halo), halo)
        xp_scr[HALO:HALO + tr, :] = x_ref[...].astype(F32)
        if sh_scr is None:
            y_scr[...] = _conv_taps(xp_scr, HALO, tr, 1, w_ref, width)
        else:
            y_scr[...] = _conv_taps_by_residue(xp_scr, sh_scr, HALO, tr, w_ref, width)


def _softplus(x):
    return jnp.maximum(x, 0.0) + jnp.log(1.0 + jnp.exp(-jnp.abs(x)))


def _conva_kernel(x_ref, halo_ref, hist_ref, w_ref, ab_ref, mask_ref, adt_ref, o_ref, gb_ref, xp_scr, y_scr,
                  *, n_s_tiles, width):
    i = pl.program_id(0)
    sec = pl.program_id(1)
    _conv_tile(i, n_s_tiles, x_ref, halo_ref, hist_ref, w_ref, xp_scr, y_scr, width)

    @pl.when(sec == 0)
    def _():
        ab = ab_ref[...]
        lane = lax.broadcasted_iota(jnp.int32, ab.shape, 1)
        g = -jnp.exp(adt_ref[0:1, :]) * _softplus(ab + adt_ref[1:2, :])
        gb_ref[...] = jnp.where(lane < H_A, g, _sigmoid(ab)) * mask_ref[...]

    scale = jnp.where(sec == 0, DK ** -0.5, 1.0).astype(F32)
    for h in range(H_A):
        hs = slice(h * DK, (h + 1) * DK)
        y = _silu(y_scr[:, hs])
        yn = y * (lax.rsqrt(jnp.sum(y * y, axis=-1, keepdims=True) + EPS) * scale)
        o_ref[:, hs] = jnp.where(sec == 2, y, yn)


def _convb_kernel(x_ref, halo_ref, hist_ref, w_ref, gate_ref, o_ref, xp_scr, y_scr, *, n_s_tiles, width):
    i = pl.program_id(0)
    _conv_tile(i, n_s_tiles, x_ref, halo_ref, hist_ref, w_ref, xp_scr, y_scr, width)
    o_ref[...] = (gate_ref[...].astype(F32) * y_scr[...]).astype(o_ref.dtype)


def _convc_kernel(x_ref, halo_ref, hist_ref, w_ref, cb_ref, lnw_ref, lnb_ref, o_ref, xp_scr, y_scr, sh_scr,
                  *, n_s_tiles, width):
    i = pl.program_id(0)
    _conv_tile(i, n_s_tiles, x_ref, halo_ref, hist_ref, w_ref, xp_scr, y_scr, width, sh_scr)
    y = y_scr[...] + cb_ref[...]
    mu = jnp.mean(y, axis=-1, keepdims=True)
    yc = y - mu
    var = jnp.mean(yc * yc, axis=-1, keepdims=True)
    yn = yc * lax.rsqrt(var + LN_EPS) * lnw_ref[...] + lnb_ref[...]
    o_ref[...] = _silu(yn).astype(o_ref.dtype)


def _conv_specs(tr, cw, width, col_of, n_s_tiles, sec_axis):
    hrows = (tr // CHUNK) * (width - 1) * SUBSET
    hb = tr // HALO
    last_hist = max(n_s_tiles - 1, 0)
    if sec_axis:
        x_map = lambda i, s: (i, col_of(s))
        halo_map = lambda i, s: (jnp.maximum(i * hb - 1, 0), col_of(s))
        hist_map = lambda i, s: (jnp.minimum(i, last_hist), s)
    else:
        x_map = lambda i: (i, col_of(0))
        halo_map = lambda i: (jnp.maximum(i * hb - 1, 0), col_of(0))
        hist_map = lambda i: (jnp.minimum(i, last_hist), 0)
    return [pl.BlockSpec((tr, cw), x_map), pl.BlockSpec((HALO, cw), halo_map), pl.BlockSpec((hrows, cw), hist_map)]


def _dot(a, b):
    return jnp.dot(a.astype(BF16), b.astype(BF16), preferred_element_type=F32)


def _dot_nt(a, b):
    return lax.dot_general(a.astype(BF16), b.astype(BF16), (((1,), (1,)), ((), ())), preferred_element_type=F32)


def _split(x):
    hi = x.astype(BF16)
    return hi, (x - hi.astype(F32)).astype(BF16)


def _mask_dot(mask_bf, x):
    hi, lo = _split(x)
    return (jnp.dot(mask_bf, hi, preferred_element_type=F32)
            + jnp.dot(mask_bf, lo, preferred_element_type=F32))


def _mask_dot_nt(xt, mask_bf):
    hi, lo = _split(xt)
    dn = (((1,), (1,)), ((), ()))
    return (lax.dot_general(hi, mask_bf, dn, preferred_element_type=F32)
            + lax.dot_general(lo, mask_bf, dn, preferred_element_type=F32))


def _bf_mask(m):
    return jnp.where(m, 1.0, 0.0).astype(BF16)


def _chunk_common(q_ref, k_ref, v_ref, gb, gc_all, gr_all, lmask, strict):
    heads = range(H_A)
    hs = [slice(h * DK, (h + 1) * DK) for h in heads]
    q = [q_ref[:, hs[h]] for h in heads]
    k = [k_ref[:, hs[h]] for h in heads]
    v = [v_ref[:, hs[h]] for h in heads]
    gc = [gc_all[:, h:h + 1] for h in heads]
    beta = [gb[:, H_A + h:H_A + h + 1] for h in heads]
    decay = [jnp.exp(jnp.where(lmask, gc[h] - gr_all[h:h + 1, :], NEG)) for h in heads]
    C = q[0].shape[0]
    qkk = [_dot_nt(jnp.concatenate([q[h], k[h]], axis=0), k[h]) for h in heads]
    qk = [qkk[h][:C] * decay[h] for h in heads]
    m = [jnp.where(strict, beta[h] * qkk[h][C:] * decay[h], 0.0) for h in heads]
    eg = [jnp.exp(gc[h]) for h in heads]
    rhs = [jnp.concatenate([beta[h] * v[h], (beta[h] * eg[h]) * k[h]], axis=1) for h in heads]
    return q, k, gc, qk, m, eg, rhs


def _delta_kernel(*refs, n_s_steps, n_chunks, n_prev):
    q_ref, k_ref, v_ref, gb_ref, sga_ref, onw_ref, s0s_ref = refs[:7]
    prev_ref = refs[7] if n_prev else None
    ya_ref, ss_ref, sp_ref, s_scr = refs[7 + (1 if n_prev else 0):]
    s = pl.program_id(0)
    C = CHUNK
    row = lax.broadcasted_iota(jnp.int32, (C, C), 0)
    col = lax.broadcasted_iota(jnp.int32, (C, C), 1)
    eye = (row == col).astype(F32)
    gb = gb_ref[...]
    gbt = gb.T[0:H_A, :]

    def finish(o, h):
        hs = slice(h * DV, (h + 1) * DV)
        on = o * lax.rsqrt(jnp.mean(o * o, axis=-1, keepdims=True) + EPS) * onw_ref[...]
        ya_ref[:, hs] = (on * sga_ref[:, hs].astype(F32)).astype(ya_ref.dtype)

    heads = range(H_A)

    @pl.when(s < n_s_steps)
    def _():
        same = ((row - col) & (DEC_SEQ - 1)) == 0
        lmask = same & (row >= col)
        strict = same & (row > col)
        lm = _bf_mask(lmask)
        last = _bf_mask(col == (C - DEC_SEQ) + (row & (DEC_SEQ - 1)))
        gc_all = _mask_dot(lm, gb)
        gr_all = _mask_dot_nt(gbt, lm)
        gl_all = _mask_dot(last, gc_all)
        rsub = lax.broadcasted_iota(jnp.int32, (C, 1), 0) & (DEC_SEQ - 1)
        rsub2 = jnp.concatenate([rsub, rsub], axis=0)
        q, k, gc, qk, m, eg, rhs = _chunk_common(q_ref, k_ref, v_ref, gb, gc_all, gr_all, lmask, strict)
        seqs = range(SUBSET)
        m2 = [_dot(m[h], m[h]) for h in heads]
        b1 = [eye - m[h] for h in heads]
        r1 = [_dot(jnp.concatenate([b1[h], m2[h]], axis=0), m2[h]) for h in heads]
        b2 = [b1[h] + r1[h][:C] for h in heads]
        tinv = [b2[h] + _dot(b2[h], r1[h][C:]) for h in heads]
        x = [_dot(tinv[h], rhs[h]) for h in heads]
        lhs = [jnp.concatenate([x[h][:, DV:], q[h]], axis=0) for h in heads]
        s_cat = [jnp.concatenate([s0s_ref[i, h] for i in seqs], axis=1) for h in heads]
        sk_all = [_dot(lhs[h], s_cat[h]) for h in heads]
        sk = []
        for h in heads:
            acc = jnp.zeros((2 * C, DV), F32)
            for i in seqs:
                acc = acc + jnp.where(rsub2 == i, sk_all[h][:, i * DV:(i + 1) * DV], 0.0)
            sk.append(acc)
        u = [x[h][:, :DV] - sk[h][:C] for h in heads]
        kd = [k[h] * jnp.exp(gl_all[:, h:h + 1] - gc[h]) for h in heads]
        kdt = [jnp.concatenate([jnp.where(rsub == i, kd[h], 0.0).T for i in seqs], axis=0) for h in heads]
        r2 = [_dot(jnp.concatenate([qk[h], kdt[h]], axis=0), u[h]) for h in heads]
        for h in heads:
            for i in seqs:
                gli = gc_all[C - DEC_SEQ + i:C - DEC_SEQ + i + 1, h:h + 1]
                ss_ref[n_prev, i, h] = jnp.exp(gli) * s0s_ref[i, h] + r2[h][C + i * DK:C + (i + 1) * DK]
        for h in heads:
            finish(eg[h] * sk[h][C:] + r2[h][:C], h)
        if n_prev:
            ss_ref[0:n_prev] = prev_ref[...]

    @pl.when(s >= n_s_steps)
    def _():
        c = (s - n_s_steps) % n_chunks

        @pl.when(c == 0)
        def _():
            s_scr[...] = jnp.zeros_like(s_scr)

        lmask = row >= col
        strict = row > col
        lm = _bf_mask(lmask)
        blk = (row >> 4) == (col >> 4)
        gc_all = _mask_dot(lm, gb)
        gr_all = _mask_dot_nt(gbt, lm)
        q, k, gc, qk, m, eg, rhs = _chunk_common(q_ref, k_ref, v_ref, gb, gc_all, gr_all, lmask, strict)
        nd = [jnp.where(blk, -m[h], 0.0) for h in heads]
        lo = [jnp.where(blk, 0.0, m[h]) for h in heads]
        W2 = 2 * DV
        p2 = [_dot(nd[h], nd[h]) for h in heads]
        a1 = [eye + nd[h] for h in heads]
        r1 = [_dot(jnp.concatenate([a1[h], p2[h]], axis=0), p2[h]) for h in heads]
        t1 = [a1[h] + r1[h][:C] for h in heads]
        r2 = [_dot(jnp.concatenate([t1[h], r1[h][C:]], axis=0), r1[h][C:]) for h in heads]
        t2 = [t1[h] + r2[h][:C] for h in heads]
        dinv = [t2[h] + _dot(t2[h], r2[h][C:]) for h in heads]
        ya = [_dot(dinv[h], jnp.concatenate([rhs[h], lo[h]], axis=1)) for h in heads]
        y = [ya[h][:, :W2] for h in heads]
        a = [ya[h][:, W2:] for h in heads]
        r3 = [_dot(a[h], ya[h]) for h in heads]
        w = [y[h] - r3[h][:, :W2] for h in heads]
        x = [w[h] + _dot(r3[h][:, W2:], w[h]) for h in heads]
        S = [s_scr[h] for h in heads]
        sk = [_dot(jnp.concatenate([x[h][:, DV:], q[h]], axis=0), S[h]) for h in heads]
        u = [x[h][:, :DV] - sk[h][:C] for h in heads]
        gl = [gc_all[C - 1:C, h:h + 1] for h in heads]
        kd = [k[h] * jnp.exp(gl[h] - gc[h]) for h in heads]
        r4 = [_dot(jnp.concatenate([qk[h], kd[h].T], axis=0), u[h]) for h in heads]
        for h in heads:
            s_scr[h] = jnp.exp(gl[h]) * S[h] + r4[h][C:]
        for h in heads:
            finish(eg[h] * sk[h][C:] + r4[h][:C], h)

        @pl.when(c == n_chunks - 1)
        def _():
            sp_ref[0] = s_scr[...]


def _delta(qkv, gb, z, onorm_w, state_all, prev_new, l, *, n_s_steps, n_batch, n_chunks, sga_col):
    T = qkv.shape[0]
    W = H_A * DK
    n_steps = n_s_steps + n_batch * n_chunks
    last_s = max(n_s_steps - 1, 0)
    kern = functools.partial(_delta_kernel, n_s_steps=n_s_steps, n_chunks=n_chunks, n_prev=l)
    sample_map = lambda s: (0, jnp.minimum(s, last_s), 0, 0, 0)
    in_specs = [
        pl.BlockSpec((CHUNK, W), lambda s: (s, 0)),
        pl.BlockSpec((CHUNK, W), lambda s: (s, 1)),
        pl.BlockSpec((CHUNK, W), lambda s: (s, 2)),
        pl.BlockSpec((CHUNK, LANES), lambda s: (s, 0)),
        pl.BlockSpec((CHUNK, W), lambda s: (s, sga_col)),
        pl.BlockSpec((1, DV), lambda s: (0, 0)),
        pl.BlockSpec((None, SUBSET, H_A, DK, DV), lambda s: (l, jnp.minimum(s, last_s), 0, 0, 0)),
    ]
    args = [qkv, qkv, qkv, gb, z, onorm_w, state_all]
    if l:
        in_specs.append(pl.BlockSpec((l, SUBSET, H_A, DK, DV), sample_map))
        args.append(prev_new)
    return pl.pallas_call(
        kern,
        grid=(n_steps,),
        in_specs=in_specs,
        out_specs=[
            pl.BlockSpec((CHUNK, W), lambda s: (s, 0)),
            pl.BlockSpec((l + 1, SUBSET, H_A, DK, DV), sample_map),
            pl.BlockSpec((1, H_A, DK, DV), lambda s: (jnp.maximum(s - n_s_steps, 0) // n_chunks, 0, 0, 0)),
        ],
        out_shape=[
            jax.ShapeDtypeStruct((T, W), BF16),
            jax.ShapeDtypeStruct((l + 1,) + state_all.shape[1:], F32),
            jax.ShapeDtypeStruct((n_batch, H_A, DK, DV), F32),
        ],
        scratch_shapes=[pltpu.VMEM((H_A, DK, DV), F32)],
        compiler_params=_cp(("arbitrary",)),
        name="delta",
    )(*args)


def _merge_kernel(ya_ref, yb_ref, yc_ref, wa_ref, wb_ref, wc_ref, g0_ref, g1_ref, g2_ref, m_ref):
    m = g0_ref[...].astype(F32) * jnp.dot(ya_ref[...], wa_ref[...], preferred_element_type=F32)
    m = m + g1_ref[...].astype(F32) * jnp.dot(yb_ref[...], wb_ref[...], preferred_element_type=F32)
    m = m + g2_ref[...].astype(F32) * jnp.dot(yc_ref[...], wc_ref[...], preferred_element_type=F32)
    m_ref[...] = m.astype(m_ref.dtype)


def _merge(ya, yb, yc, wa, wb, wc, z, *, gs_col, tm, tn):
    T, K = ya.shape
    D = wa.shape[1]
    nb = D // tn
    g0 = gs_col // tn
    act = pl.BlockSpec((tm, K), lambda i, j: (i, 0))
    wsp = pl.BlockSpec((K, tn), lambda i, j: (0, j))
    return pl.pallas_call(
        _merge_kernel,
        grid=(T // tm, nb),
        in_specs=[act, act, act, wsp, wsp, wsp,
                  pl.BlockSpec((tm, tn), lambda i, j: (i, g0 + j)),
                  pl.BlockSpec((tm, tn), lambda i, j: (i, g0 + nb + j)),
                  pl.BlockSpec((tm, tn), lambda i, j: (i, g0 + 2 * nb + j))],
        out_specs=pl.BlockSpec((tm, tn), lambda i, j: (i, j)),
        out_shape=jax.ShapeDtypeStruct((T, D), BF16),
        compiler_params=_cp(("arbitrary", "arbitrary")),
        name="merge",
    )(ya, yb, yc, wa, wb, wc, z, z, z)


def _oproj_kernel(m_ref, w_ref, x_ref, o_ref):
    o_ref[...] = x_ref[...] + jnp.dot(m_ref[...], w_ref[...], preferred_element_type=F32)


def _oproj(m, w_o, x, *, tm, tn):
    T, D = x.shape
    return pl.pallas_call(
        _oproj_kernel,
        grid=(T // tm, D // tn),
        in_specs=[pl.BlockSpec((tm, D), lambda i, j: (i, 0)),
                  pl.BlockSpec((D, tn), lambda i, j: (0, j)),
                  pl.BlockSpec((tm, tn), lambda i, j: (i, j))],
        out_specs=pl.BlockSpec((tm, tn), lambda i, j: (i, j)),
        out_shape=jax.ShapeDtypeStruct((T, D), F32),
        compiler_params=_cp(("arbitrary", "arbitrary")),
        name="oproj",
    )(m, w_o, x)


def _router_kernel(x_ref, nw_ref, wr_ref, br_ref, ri_ref, rw_ref, cnt_ref, carry_scr, *, n_groups, per_group):
    i = pl.program_id(0)
    tr = x_ref.shape[0]

    @pl.when(i == 0)
    def _():
        carry_scr[...] = jnp.zeros_like(carry_scr)

    x = x_ref[...]
    h = x * lax.rsqrt(jnp.mean(x * x, axis=-1, keepdims=True) + EPS) * nw_ref[...]
    logits = _dot(h, wr_ref[...]) + br_ref[...]
    lane = lax.broadcasted_iota(jnp.int32, logits.shape, 1)
    big = jnp.int32(1 << 20)

    def argmax_first(vals):
        m = jnp.max(vals, axis=-1, keepdims=True)
        idx = jnp.min(jnp.where(vals == m, lane, big), axis=-1, keepdims=True)
        return m, idx

    is_g = lane < n_groups
    mg, gsel = argmax_first(jnp.where(is_g, logits, NEG))
    pg = 1.0 / jnp.sum(jnp.where(is_g, jnp.exp(logits - mg), 0.0), axis=-1, keepdims=True)
    lo = n_groups + gsel * per_group
    in_group = (lane >= lo) & (lane < lo + per_group)
    le = jnp.where(in_group, logits, NEG)
    m1, i1 = argmax_first(le)
    m2, i2 = argmax_first(jnp.where(lane == i1, NEG, le))
    e21 = jnp.exp(m2 - m1)
    w1 = pg / (1.0 + e21)
    w2 = w1 * e21

    oh = ((lane == i1) | (lane == i2)).astype(F32)
    r_i = lax.broadcasted_iota(jnp.int32, (tr, tr), 0)
    c_i = lax.broadcasted_iota(jnp.int32, (tr, tr), 1)
    before = _dot((r_i > c_i).astype(F32), oh) + carry_scr[...]
    rank1 = jnp.sum(jnp.where(lane == i1, before, 0.0), axis=-1, keepdims=True)
    rank2 = jnp.sum(jnp.where(lane == i2, before, 0.0), axis=-1, keepdims=True)
    carry_scr[...] = carry_scr[...] + jnp.sum(oh, axis=0, keepdims=True)
    cnt_ref[...] = jnp.broadcast_to(carry_scr[...], cnt_ref.shape)

    ri = jnp.where(lane == 0, i1 - n_groups,
                   jnp.where(lane == 1, i2 - n_groups,
                             jnp.where(lane == 2, rank1.astype(jnp.int32),
                                       jnp.where(lane == 3, rank2.astype(jnp.int32), 0))))
    ri_ref[...] = ri
    rw_ref[...] = jnp.where(lane == 0, w1, jnp.where(lane == 1, w2, 0.0))


def _router(x, norm_w, w_r, b_r, *, n_groups, per_group, tr):
    T, D = x.shape
    kern = functools.partial(_router_kernel, n_groups=n_groups, per_group=per_group)
    return pl.pallas_call(
        kern,
        grid=(T // tr,),
        in_specs=[pl.BlockSpec((tr, D), lambda i: (i, 0)),
                  pl.BlockSpec((1, D), lambda i: (0, 0)),
                  pl.BlockSpec((D, LANES), lambda i: (0, 0)),
                  pl.BlockSpec((1, LANES), lambda i: (0, 0))],
        out_specs=[pl.BlockSpec((tr, LANES), lambda i: (i, 0)),
                   pl.BlockSpec((tr, LANES), lambda i: (i, 0)),
                   pl.BlockSpec((8, LANES), lambda i: (0, 0))],
        out_shape=[jax.ShapeDtypeStruct((T, LANES), jnp.int32),
                   jax.ShapeDtypeStruct((T, LANES), F32),
                   jax.ShapeDtypeStruct((8, LANES), F32)],
        scratch_shapes=[pltpu.VMEM((1, LANES), F32)],
        compiler_params=_cp(("arbitrary",)),
        name="router",
    )(x, norm_w, w_r, b_r)


def _row_copy(src, src_row, dst, dst_row, sem):
    return pltpu.make_async_copy(src.at[pl.ds(src_row, 1), :], dst.at[pl.ds(dst_row, 1), :], sem)


def _dispatch_kernel(e1_ref, r1_ref, e2_ref, r2_ref, off_ref, pad_ref, npad_ref, nu_ref, x_ref, nw_ref, xs_ref,
                     h_scr, z_scr, sem):
    i = pl.program_id(0)
    tr = x_ref.shape[0]
    n_experts = pad_ref.shape[0]
    tg = z_scr.shape[0]

    @pl.when(i == 0)
    def _():
        z_scr[...] = jnp.zeros_like(z_scr)
        n_all = xs_ref.shape[0] // tg

        def fill(start, rows):
            dst = xs_ref.at[pl.ds(pl.multiple_of(start, 8), rows), :]
            return pltpu.make_async_copy(z_scr.at[pl.ds(0, rows), :], dst, sem.at[4])

        def pad_fills(act):
            def per_expert(e, c):
                def piece(k, c2):
                    act(fill(pad_ref[e] + k * 8, 8))
                    return c2
                return lax.fori_loop(0, npad_ref[e], piece, c)
            lax.fori_loop(0, n_experts, per_expert, 0)

        def tail_fills(act):
            def tile(k, c):
                act(fill(k * tg, tg))
                return c
            lax.fori_loop(nu_ref[0], n_all, tile, 0)

        pad_fills(lambda cp: cp.start())
        tail_fills(lambda cp: cp.start())
        pad_fills(lambda cp: cp.wait())
        tail_fills(lambda cp: cp.wait())

    slot = i % 2
    src = h_scr.at[slot]
    x = x_ref[...]
    h_scr[slot] = x * lax.rsqrt(jnp.mean(x * x, axis=-1, keepdims=True) + EPS) * nw_ref[...]
    base = i * tr

    def issue(r, c):
        t = base + r
        _row_copy(src, r, xs_ref, off_ref[e1_ref[t]] + r1_ref[t], sem.at[2 * slot]).start()
        _row_copy(src, r, xs_ref, off_ref[e2_ref[t]] + r2_ref[t], sem.at[2 * slot + 1]).start()
        return c
    lax.fori_loop(0, tr, issue, 0, unroll=ISSUE_UNROLL)

    def wait_slot(s):
        pltpu.make_async_copy(h_scr.at[s], xs_ref.at[pl.ds(0, tr), :], sem.at[2 * s]).wait()
        pltpu.make_async_copy(h_scr.at[s], xs_ref.at[pl.ds(0, tr), :], sem.at[2 * s + 1]).wait()

    @pl.when(i > 0)
    def _():
        wait_slot(1 - slot)

    @pl.when(i == pl.num_programs(0) - 1)
    def _():
        wait_slot(slot)


def _dispatch(route, pad_start, pad_groups, n_used, x, norm_w, *, n_rows, tr, tg):
    T, D = x.shape
    n_pre = len(route) + 3
    grid_spec = pltpu.PrefetchScalarGridSpec(
        num_scalar_prefetch=n_pre,
        grid=(T // tr,),
        in_specs=[pl.BlockSpec((tr, D), lambda i, *_: (i, 0)),
                  pl.BlockSpec((1, D), lambda i, *_: (0, 0))],
        out_specs=pl.BlockSpec(memory_space=pl.ANY),
        scratch_shapes=[pltpu.VMEM((2, tr, D), F32), pltpu.VMEM((tg, D), F32), pltpu.SemaphoreType.DMA((5,))],
    )
    return pl.pallas_call(
        _dispatch_kernel,
        grid_spec=grid_spec,
        out_shape=jax.ShapeDtypeStruct((n_rows, D), F32),
        compiler_params=_cp(("arbitrary",)),
        name="dispatch",
    )(*route, pad_start, pad_groups, n_used, x, norm_w)


def _expert_kernel(te_ref, nu_ref, xs_ref, wg_ref, wu_ref, wd_ref, o_ref, wg_scr, wu_scr, wd_scr):
    i = pl.program_id(0)
    prev = te_ref[jnp.maximum(i - 1, 0)]

    @pl.when((i == 0) | (te_ref[i] != prev))
    def _():
        wg_scr[...] = wg_ref[...].astype(BF16)
        wu_scr[...] = wu_ref[...].astype(BF16)
        wd_scr[...] = wd_ref[...].astype(BF16)

    @pl.when(i < nu_ref[0])
    def _():
        h = xs_ref[...].astype(BF16)
        g = jnp.dot(h, wg_scr[...], preferred_element_type=F32)
        u = jnp.dot(h, wu_scr[...], preferred_element_type=F32)
        a = (_silu(g) * u).astype(BF16)
        o_ref[...] = jnp.dot(a, wd_scr[...], preferred_element_type=F32)

    @pl.when(i >= nu_ref[0])
    def _():
        o_ref[...] = jnp.zeros_like(o_ref)


def _experts(tile_e, n_used, xs, w_gate, w_up, w_down, l, *, n_tiles, tg):
    D = xs.shape[1]
    F = w_gate.shape[-1]

    def xs_map(i, te, nu):
        return (jnp.maximum(jnp.minimum(i, nu[0] - 1), 0), 0)

    grid_spec = pltpu.PrefetchScalarGridSpec(
        num_scalar_prefetch=2,
        grid=(n_tiles,),
        in_specs=[pl.BlockSpec((tg, D), xs_map),
                  pl.BlockSpec((None, None, D, F), lambda i, te, nu: (l, te[i], 0, 0)),
                  pl.BlockSpec((None, None, D, F), lambda i, te, nu: (l, te[i], 0, 0)),
                  pl.BlockSpec((None, None, F, D), lambda i, te, nu: (l, te[i], 0, 0))],
        out_specs=pl.BlockSpec((tg, D), lambda i, te, nu: (i, 0)),
        scratch_shapes=[pltpu.VMEM((D, F), BF16), pltpu.VMEM((D, F), BF16), pltpu.VMEM((F, D), BF16)],
    )
    return pl.pallas_call(
        _expert_kernel,
        grid_spec=grid_spec,
        out_shape=jax.ShapeDtypeStruct((n_tiles * tg, D), F32),
        compiler_params=_cp(("arbitrary",)),
        name="experts",
    )(tile_e, n_used, xs, w_gate, w_up, w_down)


def _combine_kernel(e1_ref, r1_ref, e2_ref, r2_ref, off_ref, x_ref, rw_ref, mask_ref, fw_ref, eo_ref, out_ref,
                    b_scr, sem, *, final):
    i = pl.program_id(0)
    tr = x_ref.shape[0]
    slot = i % 2

    def gather_tile(tile, s):
        base = tile * tr

        def issue(r, c):
            t = base + r
            _row_copy(eo_ref, off_ref[e1_ref[t]] + r1_ref[t], b_scr.at[s, 0], r, sem.at[2 * s]).start()
            _row_copy(eo_ref, off_ref[e2_ref[t]] + r2_ref[t], b_scr.at[s, 1], r, sem.at[2 * s + 1]).start()
            return c
        lax.fori_loop(0, tr, issue, 0, unroll=ISSUE_UNROLL)

    @pl.when(i == 0)
    def _():
        gather_tile(0, 0)

    @pl.when(i + 1 < pl.num_programs(0))
    def _():
        gather_tile(i + 1, 1 - slot)

    pltpu.make_async_copy(eo_ref.at[pl.ds(0, tr), :], b_scr.at[slot, 0], sem.at[2 * slot]).wait()
    pltpu.make_async_copy(eo_ref.at[pl.ds(0, tr), :], b_scr.at[slot, 1], sem.at[2 * slot + 1]).wait()
    rw = rw_ref[...]
    x2 = (x_ref[...] + rw[:, 0:1] * b_scr[slot, 0] + rw[:, 1:2] * b_scr[slot, 1]) * mask_ref[:, 0:1]
    if final:
        out_ref[...] = x2 * lax.rsqrt(jnp.mean(x2 * x2, axis=-1, keepdims=True) + EPS) * fw_ref[...]
    else:
        out_ref[...] = x2


def _combine(route, x, rw, mask, final_w, eo, *, tr, final):
    T, D = x.shape
    row = lambda w: pl.BlockSpec((tr, w), lambda i, *_: (i, 0))
    grid_spec = pltpu.PrefetchScalarGridSpec(
        num_scalar_prefetch=len(route),
        grid=(T // tr,),
        in_specs=[row(D), row(LANES), row(LANES),
                  pl.BlockSpec((1, D), lambda i, *_: (0, 0)),
                  pl.BlockSpec(memory_space=pl.ANY)],
        out_specs=row(D),
        scratch_shapes=[pltpu.VMEM((2, 2, tr, D), F32), pltpu.SemaphoreType.DMA((4,))],
    )
    return pl.pallas_call(
        functools.partial(_combine_kernel, final=final),
        grid_spec=grid_spec,
        out_shape=jax.ShapeDtypeStruct((T, D), F32),
        compiler_params=_cp(("arbitrary",)),
        name="combine",
    )(*route, x, rw, mask, final_w, eo)


def _sample_to_rows(a):
    nb, L, C = a.shape
    return a.reshape(nb // SUBSET, SUBSET, L, C).transpose(0, 2, 1, 3).reshape(nb * L, C)


def _rows_to_sample(r, L):
    n, C = r.shape
    nb = n // L
    return r.reshape(nb // SUBSET, L, SUBSET, C).transpose(0, 2, 1, 3).reshape(nb, L, C)


def kernel(x_prompt, x_sample, state_delta, state_conv_a, state_conv_b, state_conv_c, meta_tokens, norm1_w, w_in, conv_a_w, a_log, dt_bias, onorm_a_w, w_out_a, conv_b_w, w_out_b, conv_c_w, conv_c_b, ln_c_w, ln_c_b, w_out_c, w_o, norm2_w, w_rg, b_rg, w_re, b_re, w_gate_e, w_up_e, w_down_e, final_norm_w):
    B, SEQ, D = x_prompt.shape
    NB, L, _ = x_sample.shape
    depth = w_in.shape[0]
    W_QKV = conv_a_w.shape[-1]
    W_V = H_A * DV
    W_B = conv_b_w.shape[-1]
    W_C = conv_c_w.shape[-1]
    CA, CB, CC = conv_a_w.shape[1], conv_b_w.shape[1], conv_c_w.shape[1]
    G = w_rg.shape[-1]
    E = w_re.shape[-1]
    assert L == DEC_SEQ and NB % SUBSET == 0 and SEQ % CHUNK == 0 and W_QKV == 3 * W_V
    assert W_V == W_B == W_C and max(CA, CB, CC) - 1 <= min(HALO, CHUNK - N_META)
    assert G + E <= LANES and D % LANES == 0

    LP = CHUNK + SEQ
    lead = CHUNK - N_META
    Ts, Tp = NB * L, B * LP
    T = Ts + Tp
    n_chunks = LP // CHUNK
    n_s_steps = Ts // CHUNK

    tr = _pick_tile(_gcd(Ts, Tp), 256, CHUNK)
    tm = _pick_tile(T, 1184, 16)
    tn = 512
    tg = 256
    n_s_tiles = Ts // tr

    dt_ = x_prompt.dtype
    lead_rows = jnp.concatenate([jnp.zeros((lead, D), dt_), meta_tokens.astype(dt_)], axis=0)
    pieces = [_sample_to_rows(x_sample)]
    for b in range(B):
        pieces += [lead_rows, x_prompt[b]]
    x = jnp.concatenate(pieces, axis=0)
    pos_in_seq = jnp.arange(Tp, dtype=jnp.int32) % LP
    real = jnp.concatenate([jnp.ones((Ts,), F32), (pos_in_seq >= lead).astype(F32)])
    mask = jnp.broadcast_to(real[:, None], (T, LANES))

    n_ab = 2 * H_A
    w_t = jnp.swapaxes(w_in, 1, 2)
    w_r = jnp.pad(jnp.concatenate([w_rg, w_re], axis=2), ((0, 0), (0, 0), (0, LANES - G - E)))
    b_r = jnp.pad(jnp.concatenate([b_rg, b_re], axis=1), ((0, 0), (0, LANES - G - E)))[:, None, :]
    adt = jnp.pad(jnp.stack([a_log, dt_bias], axis=1), ((0, 0), (0, 0), (0, LANES - H_A)))
    adt = jnp.pad(adt, ((0, 0), (0, 6), (0, 0)))
    woa, wob, woc, wo = (w.astype(BF16) for w in (w_out_a, w_out_b, w_out_c, w_o))

    col_sga, col_gb, col_u, col_gl, col_gs = W_QKV, W_QKV + W_V, W_QKV + W_V + W_B, W_QKV + W_V + 2 * W_B, W_QKV + W_V + 2 * W_B + W_C
    cw = W_V
    P = TOP_K * T + E * tg
    n_tiles = P // tg

    dp, ap, bpl, cpl, as_, bs, cs = [], [], [], [], [], [], []
    ds = None
    for l in range(depth):
        z, ab = _inproj(x, norm1_w[l][None], w_t, l,
                        w_qkv=W_QKV, n_ab=n_ab, w_v=W_V, w_b=W_B, w_c=W_C, tm=tm, tn=tn)

        def hist_rows(st):
            nb, hw, C = st.shape
            return st.reshape(nb // SUBSET, SUBSET, hw, C).transpose(0, 2, 1, 3).reshape(nb * hw, C)
        hist_a, hist_b, hist_c = hist_rows(state_conv_a[l]), hist_rows(state_conv_b[l]), hist_rows(state_conv_c[l])

        xs_, hs_, hi_ = _conv_specs(tr, cw, CA, lambda s: s, n_s_tiles, True)
        qkv, gb = pl.pallas_call(
            functools.partial(_conva_kernel, n_s_tiles=n_s_tiles, width=CA),
            grid=(T // tr, W_QKV // cw),
            in_specs=[xs_, hs_, hi_,
                      pl.BlockSpec((CA, cw), lambda i, s: (0, s)),
                      pl.BlockSpec((tr, LANES), lambda i, s: (i, 0)),
                      pl.BlockSpec((tr, LANES), lambda i, s: (i, 0)),
                      pl.BlockSpec((8, LANES), lambda i, s: (0, 0))],
            out_specs=[pl.BlockSpec((tr, cw), lambda i, s: (i, s)),
                       pl.BlockSpec((tr, LANES), lambda i, s: (i, 0))],
            out_shape=[jax.ShapeDtypeStruct((T, W_QKV), F32), jax.ShapeDtypeStruct((T, LANES), F32)],
            scratch_shapes=[pltpu.VMEM((HALO + tr, cw), F32), pltpu.VMEM((tr, cw), F32)],
            compiler_params=_cp(("arbitrary", "arbitrary")),
            name="conv_a",
        )(z, z, hist_a, conv_a_w[l], ab, mask, adt[l])

        ya, ds, s_p = _delta(qkv, gb, z, onorm_a_w[l][None], state_delta, ds, l,
                             n_s_steps=n_s_steps, n_batch=B, n_chunks=n_chunks, sga_col=col_sga // cw)

        xs_, hs_, hi_ = _conv_specs(tr, cw, CB, lambda s: col_u // cw, n_s_tiles, False)
        yb = pl.pallas_call(
            functools.partial(_convb_kernel, n_s_tiles=n_s_tiles, width=CB),
            grid=(T // tr,),
            in_specs=[xs_, hs_, hi_,
                      pl.BlockSpec((CB, cw), lambda i: (0, 0)),
                      pl.BlockSpec((tr, cw), lambda i: (i, col_gb // cw))],
            out_specs=pl.BlockSpec((tr, cw), lambda i: (i, 0)),
            out_shape=jax.ShapeDtypeStruct((T, cw), BF16),
            scratch_shapes=[pltpu.VMEM((HALO + tr, cw), F32), pltpu.VMEM((tr, cw), F32)],
            compiler_params=_cp(("arbitrary",)),
            name="conv_b",
        )(z, z, hist_b, conv_b_w[l], z)

        xs_, hs_, hi_ = _conv_specs(tr, cw, CC, lambda s: col_gl // cw, n_s_tiles, False)
        hc_rows = max(HALO + tr, (CC - 1) * SUBSET + CHUNK)
        yc = pl.pallas_call(
            functools.partial(_convc_kernel, n_s_tiles=n_s_tiles, width=CC),
            grid=(T // tr,),
            in_specs=[xs_, hs_, hi_,
                      pl.BlockSpec((CC, cw), lambda i: (0, 0)),
                      pl.BlockSpec((1, cw), lambda i: (0, 0)),
                      pl.BlockSpec((1, cw), lambda i: (0, 0)),
                      pl.BlockSpec((1, cw), lambda i: (0, 0))],
            out_specs=pl.BlockSpec((tr, cw), lambda i: (i, 0)),
            out_shape=jax.ShapeDtypeStruct((T, cw), BF16),
            scratch_shapes=[pltpu.VMEM((hc_rows, cw), F32), pltpu.VMEM((tr, cw), F32),
                            pltpu.VMEM((HALO + tr, cw), F32)],
            compiler_params=_cp(("arbitrary",)),
            name="conv_c",
        )(z, z, hist_c, conv_c_w[l], conv_c_b[l][None], ln_c_w[l][None], ln_c_b[l][None])

        m = _merge(ya, yb, yc, woa[l], wob[l], woc[l], z, gs_col=col_gs, tm=tm, tn=tn)
        x1 = _oproj(m, wo[l], x, tm=tm, tn=tn)

        ri, rw, cnt = _router(x1, norm2_w[l][None], w_r[l], b_r[l], n_groups=G, per_group=E // G, tr=tr)
        counts = cnt[0, G:G + E].astype(jnp.int32)
        padded = ((counts + tg - 1) // tg) * tg
        ends = jnp.cumsum(padded)
        offs = ends - padded
        route = (ri[:, 0], ri[:, 2], ri[:, 1], ri[:, 3], offs.astype(jnp.int32))
        n_used = (ends[-1] // tg).astype(jnp.int32)
        tile_start = jnp.arange(n_tiles, dtype=jnp.int32) * tg
        tile_e = jnp.sum((tile_start[:, None] >= ends[None, :]).astype(jnp.int32), axis=1)
        last_e = jnp.sum((((n_used - 1) * tg) >= ends).astype(jnp.int32))
        tile_e = jnp.where(jnp.arange(n_tiles) < n_used, tile_e, last_e).astype(jnp.int32)

        pad_start = (((offs + counts) // 8) * 8).astype(jnp.int32)
        n_used = n_used.reshape(1)
        pad_groups = ((ends - pad_start) // 8).astype(jnp.int32)
        xs = _dispatch(route, pad_start, pad_groups, n_used, x1, norm2_w[l][None], n_rows=P, tr=tr, tg=tg)
        eo = _experts(tile_e, n_used, xs, w_gate_e, w_up_e, w_down_e, l, n_tiles=n_tiles, tg=tg)
        x = _combine(route, x1, rw, mask, final_norm_w[None], eo, tr=tr, final=(l == depth - 1))

        def prompt_tail(c0, c1, width):
            return jnp.stack([lax.slice(z, (Ts + (b + 1) * LP - (width - 1), c0), (Ts + (b + 1) * LP, c1))
                              for b in range(B)]).astype(dt_)

        def sample_hist(st, c0, c1):
            seq = _rows_to_sample(lax.slice(z, (0, c0), (Ts, c1)), L).astype(dt_)
            return jnp.concatenate([st.astype(dt_), seq], axis=1)[:, L:]
        dp.append(s_p)
        ap.append(prompt_tail(0, W_QKV, CA))
        bpl.append(prompt_tail(col_u, col_u + W_B, CB))
        cpl.append(prompt_tail(col_gl, col_gl + W_C, CC))
        as_.append(sample_hist(state_conv_a[l], 0, W_QKV))
        bs.append(sample_hist(state_conv_b[l], col_u, col_u + W_B))
        cs.append(sample_hist(state_conv_c[l], col_gl, col_gl + W_C))

    y_sample = _rows_to_sample(lax.slice(x, (0, 0), (Ts, D)), L)
    y_prompt = jnp.stack([lax.slice(x, (Ts + b * LP + CHUNK, 0), (Ts + (b + 1) * LP, D)) for b in range(B)])
    return (y_prompt, y_sample, jnp.stack(dp), jnp.stack(ap), jnp.stack(bpl), jnp.stack(cpl),
            ds, jnp.stack(as_), jnp.stack(bs), jnp.stack(cs))


def _gcd(a, b):
    while b:
        a, b = b, a % b
    return a
```

```python
import functools

import jax
import jax.numpy as jnp
from jax import lax
from jax.experimental import pallas as pl
from jax.experimental.pallas import tpu as pltpu

F32 = jnp.float32
BF16 = jnp.bfloat16

EPS = 1e-6
LN_EPS = 1e-5
N_META = 16
H_A = 8
DK = 128
DV = 128
CHUNK = 64
LANES = 128
DEC_SEQ = 8
SUBSET = CHUNK // DEC_SEQ
HALO = 32
TOP_K = 2
NEG = -1e30
ISSUE_UNROLL = 8

VMEM_LIMIT = 52 * 1024 * 1024


def _cp(dims, vmem=VMEM_LIMIT):
    return pltpu.CompilerParams(dimension_semantics=dims, vmem_limit_bytes=vmem)


def _sigmoid(x):
    return 0.5 * jnp.tanh(0.5 * x) + 0.5


def _silu(x):
    return x * _sigmoid(x)


def _pick_tile(n, target, mult):
    best = None
    for t in range(mult, min(n, target) + 1, mult):
        if n % t == 0:
            best = t
    assert best is not None, (n, target, mult)
    return best


def _inproj_kernel(x_ref, nw_ref, wa_ref, wb_ref, wab_ref, z_ref, ab_ref, h_scr, *, bounds, row_chunk):
    j = pl.program_id(1)
    tm = x_ref.shape[0]

    def proj(w_ref):
        return lax.dot_general(h_scr[...], w_ref[0].astype(BF16), (((1,), (1,)), ((), ())),
                               preferred_element_type=F32)

    @pl.when(j == 0)
    def _():
        def body(r, c):
            rs = pl.ds(pl.multiple_of(r * row_chunk, row_chunk), row_chunk)
            x = x_ref[rs, :]
            h = x * lax.rsqrt(jnp.mean(x * x, axis=-1, keepdims=True) + EPS) * nw_ref[...]
            h_scr[rs, :] = h.astype(BF16)
            return c
        lax.fori_loop(0, tm // row_chunk, body, 0)
        ab_ref[...] = proj(wab_ref)

    b_silu, b_id2, b_mul, b_glu, b_sig = bounds

    def put(v):
        z_ref[...] = v.astype(z_ref.dtype)

    @pl.when((j < b_silu) | ((j >= b_id2) & (j < b_mul)))
    def _():
        put(proj(wa_ref))

    @pl.when((j >= b_silu) & (j < b_id2))
    def _():
        put(_silu(proj(wa_ref)))

    @pl.when((j >= b_mul) & (j < b_glu))
    def _():
        put(proj(wa_ref) * proj(wb_ref))

    @pl.when((j >= b_glu) & (j < b_sig))
    def _():
        put(proj(wa_ref) * _sigmoid(proj(wb_ref)))

    @pl.when(j >= b_sig)
    def _():
        put(_sigmoid(proj(wa_ref)))


def _inproj(x, norm_w, w_t, l, *, w_qkv, n_ab, w_v, w_b, w_c, tm, tn):
    T, D = x.shape
    nq, nv, nb, nc = w_qkv // tn, w_v // tn, w_b // tn, w_c // tn
    r0 = w_qkv + n_ab
    b_silu = nq
    b_id2 = nq + nv
    b_mul = b_id2 + nb
    b_glu = b_mul + nb
    b_sig = b_glu + nc
    n_out = nq + (w_t.shape[1] - r0) // tn - nb - nc
    park_lo = b_glu - nq
    park_hi = b_sig - nq + nb + nc - 1
    assert w_qkv % LANES == 0 and r0 % 8 == 0 and (w_t.shape[1] - r0) % tn == 0

    def wa_map(i, j):
        rest = j - nq + jnp.where(j >= b_glu, nb, 0) + jnp.where(j >= b_sig, nc, 0)
        return (l, pl.multiple_of(jnp.where(j < nq, j * tn, r0 + rest * tn), 8), 0)

    def wb_map(i, j):
        jb = jnp.where(j < b_mul, park_lo,
                       jnp.where(j < b_glu, j - nq + nb,
                                 jnp.where(j < b_sig, j - nq + nb + nc, park_hi)))
        return (l, pl.multiple_of(r0 + jb * tn, 8), 0)

    kern = functools.partial(_inproj_kernel, bounds=(b_silu, b_id2, b_mul, b_glu, b_sig), row_chunk=16)
    return pl.pallas_call(
        kern,
        grid=(T // tm, n_out),
        in_specs=[
            pl.BlockSpec((tm, D), lambda i, j: (i, 0), pipeline_mode=pl.Buffered(1)),
            pl.BlockSpec((1, D), lambda i, j: (0, 0)),
            pl.BlockSpec((pl.Element(1), pl.Element(tn), pl.Element(D)), wa_map),
            pl.BlockSpec((pl.Element(1), pl.Element(tn), pl.Element(D)), wb_map),
            pl.BlockSpec((1, LANES, D), lambda i, j: (l, w_qkv // LANES, 0)),
        ],
        out_specs=[
            pl.BlockSpec((tm, tn), lambda i, j: (i, j)),
            pl.BlockSpec((tm, LANES), lambda i, j: (i, 0)),
        ],
        out_shape=[jax.ShapeDtypeStruct((T, n_out * tn), BF16), jax.ShapeDtypeStruct((T, LANES), F32)],
        scratch_shapes=[pltpu.VMEM((tm, D), BF16)],
        compiler_params=_cp(("arbitrary", "arbitrary")),
        name="inproj",
    )(x, norm_w, w_t, w_t, w_t)


def _conv_taps(xp_scr, base, rows, step, w_ref, width):
    acc = None
    for j in range(width):
        off = base - (width - 1 - j) * step
        term = xp_scr[pl.ds(off, rows), :] * w_ref[j:j + 1, :]
        acc = term if acc is None else acc + term
    return acc


def _conv_taps_by_residue(xp_scr, sh_scr, base, rows, w_ref, width):
    lo = base - (width - 1)
    acc = None
    for rho in range(8):
        offs = [lo + j for j in range(width) if (lo + j) % 8 == rho]
        if not offs:
            continue
        first, last = min(offs) - rho, max(offs) - rho + rows
        if rho:
            sh_scr[first:last, :] = xp_scr[pl.ds(first + rho, last - first), :]
        src = sh_scr if rho else xp_scr
        for off in offs:
            term = src[off - rho:off - rho + rows, :] * w_ref[off - lo:off - lo + 1, :]
            acc = term if acc is None else acc + term
    return acc


def _conv_tile(i, n_s_tiles, x_ref, halo_ref, hist_ref, w_ref, xp_scr, y_scr, width, sh_scr=None):
    tr = x_ref.shape[0]
    hsub = (width - 1) * SUBSET
    nsub = tr // CHUNK

    @pl.when(i < n_s_tiles)
    def _():
        for s in range(nsub):
            xp_scr[0:hsub, :] = hist_ref[s * hsub:(s + 1) * hsub, :]
            xp_scr[hsub:hsub + CHUNK, :] = x_ref[s * CHUNK:(s + 1) * CHUNK, :].astype(F32)
            y_scr[s * CHUNK:(s + 1) * CHUNK, :] = _conv_taps(xp_scr, hsub, CHUNK, SUBSET, w_ref, width)

    @pl.when(i >= n_s_tiles)
    def _():
        halo = halo_ref[...].astype(F32)
        xp_scr[0:HALO, :] = jnp.where(i == n_s_tiles, jnp.zeros_like(halo), halo)
        xp_scr[HALO:HALO + tr, :] = x_ref[...].astype(F32)
        if sh_scr is None:
            y_scr[...] = _conv_taps(xp_scr, HALO, tr, 1, w_ref, width)
        else:
            y_scr[...] = _conv_taps_by_residue(xp_scr, sh_scr, HALO, tr, w_ref, width)


def _softplus(x):
    return jnp.maximum(x, 0.0) + jnp.log(1.0 + jnp.exp(-jnp.abs(x)))


def _conva_kernel(x_ref, halo_ref, hist_ref, w_ref, ab_ref, mask_ref, adt_ref, o_ref, gb_ref, xp_scr, y_scr,
                  *, n_s_tiles, width):
    i = pl.program_id(0)
    sec = pl.program_id(1)
    _conv_tile(i, n_s_tiles, x_ref, halo_ref, hist_ref, w_ref, xp_scr, y_scr, width)

    @pl.when(sec == 0)
    def _():
        ab = ab_ref[...]
        lane = lax.broadcasted_iota(jnp.int32, ab.shape, 1)
        g = -jnp.exp(adt_ref[0:1, :]) * _softplus(ab + adt_ref[1:2, :])
        gb_ref[...] = jnp.where(lane < H_A, g, _sigmoid(ab)) * mask_ref[...]

    scale = jnp.where(sec == 0, DK ** -0.5, 1.0).astype(F32)
    for h in range(H_A):
        hs = slice(h * DK, (h + 1) * DK)
        y = _silu(y_scr[:, hs])
        yn = y * (lax.rsqrt(jnp.sum(y * y, axis=-1, keepdims=True) + EPS) * scale)
        o_ref[:, hs] = jnp.where(sec == 2, y, yn)


def _convb_kernel(x_ref, halo_ref, hist_ref, w_ref, gate_ref, o_ref, xp_scr, y_scr, *, n_s_tiles, width):
    i = pl.program_id(0)
    _conv_tile(i, n_s_tiles, x_ref, halo_ref, hist_ref, w_ref, xp_scr, y_scr, width)
    o_ref[...] = (gate_ref[...].astype(F32) * y_scr[...]).astype(o_ref.dtype)


def _convc_kernel(x_ref, halo_ref, hist_ref, w_ref, cb_ref, lnw_ref, lnb_ref, o_ref, xp_scr, y_scr, sh_scr,
                  *, n_s_tiles, width):
    i = pl.program_id(0)
    _conv_tile(i, n_s_tiles, x_ref, halo_ref, hist_ref, w_ref, xp_scr, y_scr, width, sh_scr)
    y = y_scr[...] + cb_ref[...]
    mu = jnp.mean(y, axis=-1, keepdims=True)
    yc = y - mu
    var = jnp.mean(yc * yc, axis=-1, keepdims=True)
    yn = yc * lax.rsqrt(var + LN_EPS) * lnw_ref[...] + lnb_ref[...]
    o_ref[...] = _silu(yn).astype(o_ref.dtype)


def _conv_specs(tr, cw, width, col_of, n_s_tiles, sec_axis):
    hrows = (tr // CHUNK) * (width - 1) * SUBSET
    hb = tr // HALO
    last_hist = max(n_s_tiles - 1, 0)
    if sec_axis:
        x_map = lambda i, s: (i, col_of(s))
        halo_map = lambda i, s: (jnp.maximum(i * hb - 1, 0), col_of(s))
        hist_map = lambda i, s: (jnp.minimum(i, last_hist), s)
    else:
        x_map = lambda i: (i, col_of(0))
        halo_map = lambda i: (jnp.maximum(i * hb - 1, 0), col_of(0))
        hist_map = lambda i: (jnp.minimum(i, last_hist), 0)
    return [pl.BlockSpec((tr, cw), x_map), pl.BlockSpec((HALO, cw), halo_map), pl.BlockSpec((hrows, cw), hist_map)]


def _dot(a, b):
    return jnp.dot(a.astype(BF16), b.astype(BF16), preferred_element_type=F32)


def _dot_nt(a, b):
    return lax.dot_general(a.astype(BF16), b.astype(BF16), (((1,), (1,)), ((), ())), preferred_element_type=F32)


def _split(x):
    hi = x.astype(BF16)
    return hi, (x - hi.astype(F32)).astype(BF16)


def _mask_dot(mask_bf, x):
    hi, lo = _split(x)
    return (jnp.dot(mask_bf, hi, preferred_element_type=F32)
            + jnp.dot(mask_bf, lo, preferred_element_type=F32))


def _mask_dot_nt(xt, mask_bf):
    hi, lo = _split(xt)
    dn = (((1,), (1,)), ((), ()))
    return (lax.dot_general(hi, mask_bf, dn, preferred_element_type=F32)
            + lax.dot_general(lo, mask_bf, dn, preferred_element_type=F32))


def _bf_mask(m):
    return jnp.where(m, 1.0, 0.0).astype(BF16)


def _chunk_common(q_ref, k_ref, v_ref, gb, gc_all, gr_all, lmask, strict):
    heads = range(H_A)
    hs = [slice(h * DK, (h + 1) * DK) for h in heads]
    q = [q_ref[:, hs[h]] for h in heads]
    k = [k_ref[:, hs[h]] for h in heads]
    v = [v_ref[:, hs[h]] for h in heads]
    gc = [gc_all[:, h:h + 1] for h in heads]
    beta = [gb[:, H_A + h:H_A + h + 1] for h in heads]
    decay = [jnp.exp(jnp.where(lmask, gc[h] - gr_all[h:h + 1, :], NEG)) for h in heads]
    C = q[0].shape[0]
    qkk = [_dot_nt(jnp.concatenate([q[h], k[h]], axis=0), k[h]) for h in heads]
    qk = [qkk[h][:C] * decay[h] for h in heads]
    m = [jnp.where(strict, beta[h] * qkk[h][C:] * decay[h], 0.0) for h in heads]
    eg = [jnp.exp(gc[h]) for h in heads]
    rhs = [jnp.concatenate([beta[h] * v[h], (beta[h] * eg[h]) * k[h]], axis=1) for h in heads]
    return q, k, gc, qk, m, eg, rhs


def _delta_kernel(*refs, n_s_steps, n_chunks, n_prev):
    q_ref, k_ref, v_ref, gb_ref, sga_ref, onw_ref, s0s_ref = refs[:7]
    prev_ref = refs[7] if n_prev else None
    ya_ref, ss_ref, sp_ref, s_scr = refs[7 + (1 if n_prev else 0):]
    s = pl.program_id(0)
    C = CHUNK
    row = lax.broadcasted_iota(jnp.int32, (C, C), 0)
    col = lax.broadcasted_iota(jnp.int32, (C, C), 1)
    eye = (row == col).astype(F32)
    gb = gb_ref[...]
    gbt = gb.T[0:H_A, :]

    def finish(o, h):
        hs = slice(h * DV, (h + 1) * DV)
        on = o * lax.rsqrt(jnp.mean(o * o, axis=-1, keepdims=True) + EPS) * onw_ref[...]
        ya_ref[:, hs] = (on * sga_ref[:, hs].astype(F32)).astype(ya_ref.dtype)

    heads = range(H_A)

    @pl.when(s < n_s_steps)
    def _():
        same = ((row - col) & (DEC_SEQ - 1)) == 0
        lmask = same & (row >= col)
        strict = same & (row > col)
        lm = _bf_mask(lmask)
        last = _bf_mask(col == (C - DEC_SEQ) + (row & (DEC_SEQ - 1)))
        gc_all = _mask_dot(lm, gb)
        gr_all = _mask_dot_nt(gbt, lm)
        gl_all = _mask_dot(last, gc_all)
        rsub = lax.broadcasted_iota(jnp.int32, (C, 1), 0) & (DEC_SEQ - 1)
        rsub2 = jnp.concatenate([rsub, rsub], axis=0)
        q, k, gc, qk, m, eg, rhs = _chunk_common(q_ref, k_ref, v_ref, gb, gc_all, gr_all, lmask, strict)
        seqs = range(SUBSET)
        m2 = [_dot(m[h], m[h]) for h in heads]
        b1 = [eye - m[h] for h in heads]
        r1 = [_dot(jnp.concatenate([b1[h], m2[h]], axis=0), m2[h]) for h in heads]
        b2 = [b1[h] + r1[h][:C] for h in heads]
        tinv = [b2[h] + _dot(b2[h], r1[h][C:]) for h in heads]
        x = [_dot(tinv[h], rhs[h]) for h in heads]
        lhs = [jnp.concatenate([x[h][:, DV:], q[h]], axis=0) for h in heads]
        s_cat = [jnp.concatenate([s0s_ref[i, h] for i in seqs], axis=1) for h in heads]
        sk_all = [_dot(lhs[h], s_cat[h]) for h in heads]
        sk = []
        for h in heads:
            acc = jnp.zeros((2 * C, DV), F32)
            for i in seqs:
                acc = acc + jnp.where(rsub2 == i, sk_all[h][:, i * DV:(i + 1) * DV], 0.0)
            sk.append(acc)
        u = [x[h][:, :DV] - sk[h][:C] for h in heads]
        kd = [k[h] * jnp.exp(gl_all[:, h:h + 1] - gc[h]) for h in heads]
        kdt = [jnp.concatenate([jnp.where(rsub == i, kd[h], 0.0).T for i in seqs], axis=0) for h in heads]
        r2 = [_dot(jnp.concatenate([qk[h], kdt[h]], axis=0), u[h]) for h in heads]
        for h in heads:
            for i in seqs:
                gli = gc_all[C - DEC_SEQ + i:C - DEC_SEQ + i + 1, h:h + 1]
                ss_ref[n_prev, i, h] = jnp.exp(gli) * s0s_ref[i, h] + r2[h][C + i * DK:C + (i + 1) * DK]
        for h in heads:
            finish(eg[h] * sk[h][C:] + r2[h][:C], h)
        if n_prev:
            ss_ref[0:n_prev] = prev_ref[...]

    @pl.when(s >= n_s_steps)
    def _():
        c = (s - n_s_steps) % n_chunks

        @pl.when(c == 0)
        def _():
            s_scr[...] = jnp.zeros_like(s_scr)

        lmask = row >= col
        strict = row > col
        lm = _bf_mask(lmask)
        blk = (row >> 4) == (col >> 4)
        gc_all = _mask_dot(lm, gb)
        gr_all = _mask_dot_nt(gbt, lm)
        q, k, gc, qk, m, eg, rhs = _chunk_common(q_ref, k_ref, v_ref, gb, gc_all, gr_all, lmask, strict)
        nd = [jnp.where(blk, -m[h], 0.0) for h in heads]
        lo = [jnp.where(blk, 0.0, m[h]) for h in heads]
        W2 = 2 * DV
        p2 = [_dot(nd[h], nd[h]) for h in heads]
        a1 = [eye + nd[h] for h in heads]
        r1 = [_dot(jnp.concatenate([a1[h], p2[h]], axis=0), p2[h]) for h in heads]
        t1 = [a1[h] + r1[h][:C] for h in heads]
        r2 = [_dot(jnp.concatenate([t1[h], r1[h][C:]], axis=0), r1[h][C:]) for h in heads]
        t2 = [t1[h] + r2[h][:C] for h in heads]
        dinv = [t2[h] + _dot(t2[h], r2[h][C:]) for h in heads]
        ya = [_dot(dinv[h], jnp.concatenate([rhs[h], lo[h]], axis=1)) for h in heads]
        y = [ya[h][:, :W2] for h in heads]
        a = [ya[h][:, W2:] for h in heads]
        r3 = [_dot(a[h], ya[h]) for h in heads]
        w = [y[h] - r3[h][:, :W2] for h in heads]
        x = [w[h] + _dot(r3[h][:, W2:], w[h]) for h in heads]
        S = [s_scr[h] for h in heads]
        sk = [_dot(jnp.concatenate([x[h][:, DV:], q[h]], axis=0), S[h]) for h in heads]
        u = [x[h][:, :DV] - sk[h][:C] for h in heads]
        gl = [gc_all[C - 1:C, h:h + 1] for h in heads]
        kd = [k[h] * jnp.exp(gl[h] - gc[h]) for h in heads]
        r4 = [_dot(jnp.concatenate([qk[h], kd[h].T], axis=0), u[h]) for h in heads]
        for h in heads:
            s_scr[h] = jnp.exp(gl[h]) * S[h] + r4[h][C:]
        for h in heads:
            finish(eg[h] * sk[h][C:] + r4[h][:C], h)

        @pl.when(c == n_chunks - 1)
        def _():
            sp_ref[0] = s_scr[...]


def _delta(qkv, gb, z, onorm_w, state_all, prev_new, l, *, n_s_steps, n_batch, n_chunks, sga_col):
    T = qkv.shape[0]
    W = H_A * DK
    n_steps = n_s_steps + n_batch * n_chunks
    last_s = max(n_s_steps - 1, 0)
    kern = functools.partial(_delta_kernel, n_s_steps=n_s_steps, n_chunks=n_chunks, n_prev=l)
    sample_map = lambda s: (0, jnp.minimum(s, last_s), 0, 0, 0)
    in_specs = [
        pl.BlockSpec((CHUNK, W), lambda s: (s, 0)),
        pl.BlockSpec((CHUNK, W), lambda s: (s, 1)),
        pl.BlockSpec((CHUNK, W), lambda s: (s, 2)),
        pl.BlockSpec((CHUNK, LANES), lambda s: (s, 0)),
        pl.BlockSpec((CHUNK, W), lambda s: (s, sga_col)),
        pl.BlockSpec((1, DV), lambda s: (0, 0)),
        pl.BlockSpec((None, SUBSET, H_A, DK, DV), lambda s: (l, jnp.minimum(s, last_s), 0, 0, 0)),
    ]
    args = [qkv, qkv, qkv, gb, z, onorm_w, state_all]
    if l:
        in_specs.append(pl.BlockSpec((l, SUBSET, H_A, DK, DV), sample_map))
        args.append(prev_new)
    return pl.pallas_call(
        kern,
        grid=(n_steps,),
        in_specs=in_specs,
        out_specs=[
            pl.BlockSpec((CHUNK, W), lambda s: (s, 0)),
            pl.BlockSpec((l + 1, SUBSET, H_A, DK, DV), sample_map),
            pl.BlockSpec((1, H_A, DK, DV), lambda s: (jnp.maximum(s - n_s_steps, 0) // n_chunks, 0, 0, 0)),
        ],
        out_shape=[
            jax.ShapeDtypeStruct((T, W), BF16),
            jax.ShapeDtypeStruct((l + 1,) + state_all.shape[1:], F32),
            jax.ShapeDtypeStruct((n_batch, H_A, DK, DV), F32),
        ],
        scratch_shapes=[pltpu.VMEM((H_A, DK, DV), F32)],
        compiler_params=_cp(("arbitrary",)),
        name="delta",
    )(*args)


def _merge_kernel(ya_ref, yb_ref, yc_ref, wa_ref, wb_ref, wc_ref, g0_ref, g1_ref, g2_ref, m_ref):
    m = g0_ref[...].astype(F32) * jnp.dot(ya_ref[...], wa_ref[...], preferred_element_type=F32)
    m = m + g1_ref[...].astype(F32) * jnp.dot(yb_ref[...], wb_ref[...], preferred_element_type=F32)
    m = m + g2_ref[...].astype(F32) * jnp.dot(yc_ref[...], wc_ref[...], preferred_element_type=F32)
    m_ref[...] = m.astype(m_ref.dtype)


def _merge(ya, yb, yc, wa, wb, wc, z, *, gs_col, tm, tn):
    T, K = ya.shape
    D = wa.shape[1]
    nb = D // tn
    g0 = gs_col // tn
    act = pl.BlockSpec((tm, K), lambda i, j: (i, 0))
    wsp = pl.BlockSpec((K, tn), lambda i, j: (0, j))
    return pl.pallas_call(
        _merge_kernel,
        grid=(T // tm, nb),
        in_specs=[act, act, act, wsp, wsp, wsp,
                  pl.BlockSpec((tm, tn), lambda i, j: (i, g0 + j)),
                  pl.BlockSpec((tm, tn), lambda i, j: (i, g0 + nb + j)),
                  pl.BlockSpec((tm, tn), lambda i, j: (i, g0 + 2 * nb + j))],
        out_specs=pl.BlockSpec((tm, tn), lambda i, j: (i, j)),
        out_shape=jax.ShapeDtypeStruct((T, D), BF16),
        compiler_params=_cp(("arbitrary", "arbitrary")),
        name="merge",
    )(ya, yb, yc, wa, wb, wc, z, z, z)


def _oproj_kernel(m_ref, w_ref, x_ref, o_ref):
    o_ref[...] = x_ref[...] + jnp.dot(m_ref[...], w_ref[...], preferred_element_type=F32)


def _oproj(m, w_o, x, *, tm, tn):
    T, D = x.shape
    return pl.pallas_call(
        _oproj_kernel,
        grid=(T // tm, D // tn),
        in_specs=[pl.BlockSpec((tm, D), lambda i, j: (i, 0)),
                  pl.BlockSpec((D, tn), lambda i, j: (0, j)),
                  pl.BlockSpec((tm, tn), lambda i, j: (i, j))],
        out_specs=pl.BlockSpec((tm, tn), lambda i, j: (i, j)),
        out_shape=jax.ShapeDtypeStruct((T, D), F32),
        compiler_params=_cp(("arbitrary", "arbitrary")),
        name="oproj",
    )(m, w_o, x)


def _router_kernel(x_ref, nw_ref, wr_ref, br_ref, ri_ref, rw_ref, cnt_ref, carry_scr, *, n_groups, per_group):
    i = pl.program_id(0)
    tr = x_ref.shape[0]

    @pl.when(i == 0)
    def _():
        carry_scr[...] = jnp.zeros_like(carry_scr)

    x = x_ref[...]
    h = x * lax.rsqrt(jnp.mean(x * x, axis=-1, keepdims=True) + EPS) * nw_ref[...]
    logits = _dot(h, wr_ref[...]) + br_ref[...]
    lane = lax.broadcasted_iota(jnp.int32, logits.shape, 1)
    big = jnp.int32(1 << 20)

    def argmax_first(vals):
        m = jnp.max(vals, axis=-1, keepdims=True)
        idx = jnp.min(jnp.where(vals == m, lane, big), axis=-1, keepdims=True)
        return m, idx

    is_g = lane < n_groups
    mg, gsel = argmax_first(jnp.where(is_g, logits, NEG))
    pg = 1.0 / jnp.sum(jnp.where(is_g, jnp.exp(logits - mg), 0.0), axis=-1, keepdims=True)
    lo = n_groups + gsel * per_group
    in_group = (lane >= lo) & (lane < lo + per_group)
    le = jnp.where(in_group, logits, NEG)
    m1, i1 = argmax_first(le)
    m2, i2 = argmax_first(jnp.where(lane == i1, NEG, le))
    e21 = jnp.exp(m2 - m1)
    w1 = pg / (1.0 + e21)
    w2 = w1 * e21

    oh = ((lane == i1) | (lane == i2)).astype(F32)
    r_i = lax.broadcasted_iota(jnp.int32, (tr, tr), 0)
    c_i = lax.broadcasted_iota(jnp.int32, (tr, tr), 1)
    before = _dot((r_i > c_i).astype(F32), oh) + carry_scr[...]
    rank1 = jnp.sum(jnp.where(lane == i1, before, 0.0), axis=-1, keepdims=True)
    rank2 = jnp.sum(jnp.where(lane == i2, before, 0.0), axis=-1, keepdims=True)
    carry_scr[...] = carry_scr[...] + jnp.sum(oh, axis=0, keepdims=True)
    cnt_ref[...] = jnp.broadcast_to(carry_scr[...], cnt_ref.shape)

    ri = jnp.where(lane == 0, i1 - n_groups,
                   jnp.where(lane == 1, i2 - n_groups,
                             jnp.where(lane == 2, rank1.astype(jnp.int32),
                                       jnp.where(lane == 3, rank2.astype(jnp.int32), 0))))
    ri_ref[...] = ri
    rw_ref[...] = jnp.where(lane == 0, w1, jnp.where(lane == 1, w2, 0.0))


def _router(x, norm_w, w_r, b_r, *, n_groups, per_group, tr):
    T, D = x.shape
    kern = functools.partial(_router_kernel, n_groups=n_groups, per_group=per_group)
    return pl.pallas_call(
        kern,
        grid=(T // tr,),
        in_specs=[pl.BlockSpec((tr, D), lambda i: (i, 0)),
                  pl.BlockSpec((1, D), lambda i: (0, 0)),
                  pl.BlockSpec((D, LANES), lambda i: (0, 0)),
                  pl.BlockSpec((1, LANES), lambda i: (0, 0))],
        out_specs=[pl.BlockSpec((tr, LANES), lambda i: (i, 0)),
                   pl.BlockSpec((tr, LANES), lambda i: (i, 0)),
                   pl.BlockSpec((8, LANES), lambda i: (0, 0))],
        out_shape=[jax.ShapeDtypeStruct((T, LANES), jnp.int32),
                   jax.ShapeDtypeStruct((T, LANES), F32),
                   jax.ShapeDtypeStruct((8, LANES), F32)],
        scratch_shapes=[pltpu.VMEM((1, LANES), F32)],
        compiler_params=_cp(("arbitrary",)),
        name="router",
    )(x, norm_w, w_r, b_r)


def _row_copy(src, src_row, dst, dst_row, sem):
    return pltpu.make_async_copy(src.at[pl.ds(src_row, 1), :], dst.at[pl.ds(dst_row, 1), :], sem)


def _dispatch_kernel(e1_ref, r1_ref, e2_ref, r2_ref, off_ref, pad_ref, npad_ref, nu_ref, x_ref, nw_ref, xs_ref,
                     h_scr, z_scr, sem):
    i = pl.program_id(0)
    tr = x_ref.shape[0]
    n_experts = pad_ref.shape[0]
    tg = z_scr.shape[0]

    @pl.when(i == 0)
    def _():
        z_scr[...] = jnp.zeros_like(z_scr)
        n_all = xs_ref.shape[0] // tg

        def fill(start, rows):
            dst = xs_ref.at[pl.ds(pl.multiple_of(start, 8), rows), :]
            return pltpu.make_async_copy(z_scr.at[pl.ds(0, rows), :], dst, sem.at[4])

        def pad_fills(act):
            def per_expert(e, c):
                def piece(k, c2):
                    act(fill(pad_ref[e] + k * 8, 8))
                    return c2
                return lax.fori_loop(0, npad_ref[e], piece, c)
            lax.fori_loop(0, n_experts, per_expert, 0)

        def tail_fills(act):
            def tile(k, c):
                act(fill(k * tg, tg))
                return c
            lax.fori_loop(nu_ref[0], n_all, tile, 0)

        pad_fills(lambda cp: cp.start())
        tail_fills(lambda cp: cp.start())
        pad_fills(lambda cp: cp.wait())
        tail_fills(lambda cp: cp.wait())

    slot = i % 2
    src = h_scr.at[slot]
    x = x_ref[...]
    h_scr[slot] = x * lax.rsqrt(jnp.mean(x * x, axis=-1, keepdims=True) + EPS) * nw_ref[...]
    base = i * tr

    def issue(r, c):
        t = base + r
        _row_copy(src, r, xs_ref, off_ref[e1_ref[t]] + r1_ref[t], sem.at[2 * slot]).start()
        _row_copy(src, r, xs_ref, off_ref[e2_ref[t]] + r2_ref[t], sem.at[2 * slot + 1]).start(priority=1)
        return c
    lax.fori_loop(0, tr, issue, 0, unroll=ISSUE_UNROLL)

    def wait_slot(s):
        pltpu.make_async_copy(h_scr.at[s], xs_ref.at[pl.ds(0, tr), :], sem.at[2 * s]).wait()
        pltpu.make_async_copy(h_scr.at[s], xs_ref.at[pl.ds(0, tr), :], sem.at[2 * s + 1]).wait()

    @pl.when(i > 0)
    def _():
        wait_slot(1 - slot)

    @pl.when(i == pl.num_programs(0) - 1)
    def _():
        wait_slot(slot)


def _dispatch(route, pad_start, pad_groups, n_used, x, norm_w, *, n_rows, tr, tg):
    T, D = x.shape
    n_pre = len(route) + 3
    grid_spec = pltpu.PrefetchScalarGridSpec(
        num_scalar_prefetch=n_pre,
        grid=(T // tr,),
        in_specs=[pl.BlockSpec((tr, D), lambda i, *_: (i, 0)),
                  pl.BlockSpec((1, D), lambda i, *_: (0, 0))],
        out_specs=pl.BlockSpec(memory_space=pl.ANY),
        scratch_shapes=[pltpu.VMEM((2, tr, D), F32), pltpu.VMEM((tg, D), F32), pltpu.SemaphoreType.DMA((5,))],
    )
    return pl.pallas_call(
        _dispatch_kernel,
        grid_spec=grid_spec,
        out_shape=jax.ShapeDtypeStruct((n_rows, D), F32),
        compiler_params=_cp(("arbitrary",)),
        name="dispatch",
    )(*route, pad_start, pad_groups, n_used, x, norm_w)


def _expert_kernel(te_ref, nu_ref, xs_ref, wg_ref, wu_ref, wd_ref, o_ref, wg_scr, wu_scr, wd_scr):
    i = pl.program_id(0)
    prev = te_ref[jnp.maximum(i - 1, 0)]

    @pl.when((i == 0) | (te_ref[i] != prev))
    def _():
        wg_scr[...] = wg_ref[...].astype(BF16)
        wu_scr[...] = wu_ref[...].astype(BF16)
        wd_scr[...] = wd_ref[...].astype(BF16)

    @pl.when(i < nu_ref[0])
    def _():
        h = xs_ref[...].astype(BF16)
        g = jnp.dot(h, wg_scr[...], preferred_element_type=F32)
        u = jnp.dot(h, wu_scr[...], preferred_element_type=F32)
        a = (_silu(g) * u).astype(BF16)
        o_ref[...] = jnp.dot(a, wd_scr[...], preferred_element_type=F32)

    @pl.when(i >= nu_ref[0])
    def _():
        o_ref[...] = jnp.zeros_like(o_ref)


def _experts(tile_e, n_used, xs, w_gate, w_up, w_down, l, *, n_tiles, tg):
    D = xs.shape[1]
    F = w_gate.shape[-1]

    def xs_map(i, te, nu):
        return (jnp.maximum(jnp.minimum(i, nu[0] - 1), 0), 0)

    grid_spec = pltpu.PrefetchScalarGridSpec(
        num_scalar_prefetch=2,
        grid=(n_tiles,),
        in_specs=[pl.BlockSpec((tg, D), xs_map),
                  pl.BlockSpec((None, None, D, F), lambda i, te, nu: (l, te[i], 0, 0)),
                  pl.BlockSpec((None, None, D, F), lambda i, te, nu: (l, te[i], 0, 0)),
                  pl.BlockSpec((None, None, F, D), lambda i, te, nu: (l, te[i], 0, 0))],
        out_specs=pl.BlockSpec((tg, D), lambda i, te, nu: (i, 0)),
        scratch_shapes=[pltpu.VMEM((D, F), BF16), pltpu.VMEM((D, F), BF16), pltpu.VMEM((F, D), BF16)],
    )
    return pl.pallas_call(
        _expert_kernel,
        grid_spec=grid_spec,
        out_shape=jax.ShapeDtypeStruct((n_tiles * tg, D), F32),
        compiler_params=_cp(("arbitrary",)),
        name="experts",
    )(tile_e, n_used, xs, w_gate, w_up, w_down)


def _combine_kernel(e1_ref, r1_ref, e2_ref, r2_ref, off_ref, x_ref, rw_ref, mask_ref, fw_ref, eo_ref, out_ref,
                    b_scr, sem, *, final):
    i = pl.program_id(0)
    tr = x_ref.shape[0]
    slot = i % 2

    def gather_tile(tile, s):
        base = tile * tr

        def issue(r, c):
            t = base + r
            _row_copy(eo_ref, off_ref[e1_ref[t]] + r1_ref[t], b_scr.at[s, 0], r, sem.at[2 * s]).start()
            _row_copy(eo_ref, off_ref[e2_ref[t]] + r2_ref[t], b_scr.at[s, 1], r, sem.at[2 * s + 1]).start(priority=1)
            return c
        lax.fori_loop(0, tr, issue, 0, unroll=ISSUE_UNROLL)

    @pl.when(i == 0)
    def _():
        gather_tile(0, 0)

    @pl.when(i + 1 < pl.num_programs(0))
    def _():
        gather_tile(i + 1, 1 - slot)

    pltpu.make_async_copy(eo_ref.at[pl.ds(0, tr), :], b_scr.at[slot, 0], sem.at[2 * slot]).wait()
    pltpu.make_async_copy(eo_ref.at[pl.ds(0, tr), :], b_scr.at[slot, 1], sem.at[2 * slot + 1]).wait()
    rw = rw_ref[...]
    x2 = (x_ref[...] + rw[:, 0:1] * b_scr[slot, 0] + rw[:, 1:2] * b_scr[slot, 1]) * mask_ref[:, 0:1]
    if final:
        out_ref[...] = x2 * lax.rsqrt(jnp.mean(x2 * x2, axis=-1, keepdims=True) + EPS) * fw_ref[...]
    else:
        out_ref[...] = x2


def _combine(route, x, rw, mask, final_w, eo, *, tr, final):
    T, D = x.shape
    row = lambda w: pl.BlockSpec((tr, w), lambda i, *_: (i, 0))
    grid_spec = pltpu.PrefetchScalarGridSpec(
        num_scalar_prefetch=len(route),
        grid=(T // tr,),
        in_specs=[row(D), row(LANES), row(LANES),
                  pl.BlockSpec((1, D), lambda i, *_: (0, 0)),
                  pl.BlockSpec(memory_space=pl.ANY)],
        out_specs=row(D),
        scratch_shapes=[pltpu.VMEM((2, 2, tr, D), F32), pltpu.SemaphoreType.DMA((4,))],
    )
    return pl.pallas_call(
        functools.partial(_combine_kernel, final=final),
        grid_spec=grid_spec,
        out_shape=jax.ShapeDtypeStruct((T, D), F32),
        compiler_params=_cp(("arbitrary",)),
        name="combine",
    )(*route, x, rw, mask, final_w, eo)


def _sample_to_rows(a):
    nb, L, C = a.shape
    return a.reshape(nb // SUBSET, SUBSET, L, C).transpose(0, 2, 1, 3).reshape(nb * L, C)


def _rows_to_sample(r, L):
    n, C = r.shape
    nb = n // L
    return r.reshape(nb // SUBSET, L, SUBSET, C).transpose(0, 2, 1, 3).reshape(nb, L, C)


def kernel(x_prompt, x_sample, state_delta, state_conv_a, state_conv_b, state_conv_c, meta_tokens, norm1_w, w_in, conv_a_w, a_log, dt_bias, onorm_a_w, w_out_a, conv_b_w, w_out_b, conv_c_w, conv_c_b, ln_c_w, ln_c_b, w_out_c, w_o, norm2_w, w_rg, b_rg, w_re, b_re, w_gate_e, w_up_e, w_down_e, final_norm_w):
    B, SEQ, D = x_prompt.shape
    NB, L, _ = x_sample.shape
    depth = w_in.shape[0]
    W_QKV = conv_a_w.shape[-1]
    W_V = H_A * DV
    W_B = conv_b_w.shape[-1]
    W_C = conv_c_w.shape[-1]
    CA, CB, CC = conv_a_w.shape[1], conv_b_w.shape[1], conv_c_w.shape[1]
    G = w_rg.shape[-1]
    E = w_re.shape[-1]
    assert L == DEC_SEQ and NB % SUBSET == 0 and SEQ % CHUNK == 0 and W_QKV == 3 * W_V
    assert W_V == W_B == W_C and max(CA, CB, CC) - 1 <= min(HALO, CHUNK - N_META)
    assert G + E <= LANES and D % LANES == 0

    LP = CHUNK + SEQ
    lead = CHUNK - N_META
    Ts, Tp = NB * L, B * LP
    T = Ts + Tp
    n_chunks = LP // CHUNK
    n_s_steps = Ts // CHUNK

    tr = _pick_tile(_gcd(Ts, Tp), 256, CHUNK)
    tm = _pick_tile(T, 1184, 16)
    tn = 512
    tg = 256
    n_s_tiles = Ts // tr

    dt_ = x_prompt.dtype
    lead_rows = jnp.concatenate([jnp.zeros((lead, D), dt_), meta_tokens.astype(dt_)], axis=0)
    pieces = [_sample_to_rows(x_sample)]
    for b in range(B):
        pieces += [lead_rows, x_prompt[b]]
    x = jnp.concatenate(pieces, axis=0)
    pos_in_seq = jnp.arange(Tp, dtype=jnp.int32) % LP
    real = jnp.concatenate([jnp.ones((Ts,), F32), (pos_in_seq >= lead).astype(F32)])
    mask = jnp.broadcast_to(real[:, None], (T, LANES))

    n_ab = 2 * H_A
    w_t = jnp.swapaxes(w_in, 1, 2)
    w_r = jnp.pad(jnp.concatenate([w_rg, w_re], axis=2), ((0, 0), (0, 0), (0, LANES - G - E)))
    b_r = jnp.pad(jnp.concatenate([b_rg, b_re], axis=1), ((0, 0), (0, LANES - G - E)))[:, None, :]
    adt = jnp.pad(jnp.stack([a_log, dt_bias], axis=1), ((0, 0), (0, 0), (0, LANES - H_A)))
    adt = jnp.pad(adt, ((0, 0), (0, 6), (0, 0)))
    woa, wob, woc, wo = (w.astype(BF16) for w in (w_out_a, w_out_b, w_out_c, w_o))

    col_sga, col_gb, col_u, col_gl, col_gs = W_QKV, W_QKV + W_V, W_QKV + W_V + W_B, W_QKV + W_V + 2 * W_B, W_QKV + W_V + 2 * W_B + W_C
    cw = W_V
    P = TOP_K * T + E * tg
    n_tiles = P // tg

    dp, ap, bpl, cpl, as_, bs, cs = [], [], [], [], [], [], []
    ds = None
    for l in range(depth):
        z, ab = _inproj(x, norm1_w[l][None], w_t, l,
                        w_qkv=W_QKV, n_ab=n_ab, w_v=W_V, w_b=W_B, w_c=W_C, tm=tm, tn=tn)

        def hist_rows(st):
            nb, hw, C = st.shape
            return st.reshape(nb // SUBSET, SUBSET, hw, C).transpose(0, 2, 1, 3).reshape(nb * hw, C)
        hist_a, hist_b, hist_c = hist_rows(state_conv_a[l]), hist_rows(state_conv_b[l]), hist_rows(state_conv_c[l])

        xs_, hs_, hi_ = _conv_specs(tr, cw, CA, lambda s: s, n_s_tiles, True)
        qkv, gb = pl.pallas_call(
            functools.partial(_conva_kernel, n_s_tiles=n_s_tiles, width=CA),
            grid=(T // tr, W_QKV // cw),
            in_specs=[xs_, hs_, hi_,
                      pl.BlockSpec((CA, cw), lambda i, s: (0, s)),
                      pl.BlockSpec((tr, LANES), lambda i, s: (i, 0)),
                      pl.BlockSpec((tr, LANES), lambda i, s: (i, 0)),
                      pl.BlockSpec((8, LANES), lambda i, s: (0, 0))],
            out_specs=[pl.BlockSpec((tr, cw), lambda i, s: (i, s)),
                       pl.BlockSpec((tr, LANES), lambda i, s: (i, 0))],
            out_shape=[jax.ShapeDtypeStruct((T, W_QKV), F32), jax.ShapeDtypeStruct((T, LANES), F32)],
            scratch_shapes=[pltpu.VMEM((HALO + tr, cw), F32), pltpu.VMEM((tr, cw), F32)],
            compiler_params=_cp(("arbitrary", "arbitrary")),
            name="conv_a",
        )(z, z, hist_a, conv_a_w[l], ab, mask, adt[l])

        ya, ds, s_p = _delta(qkv, gb, z, onorm_a_w[l][None], state_delta, ds, l,
                             n_s_steps=n_s_steps, n_batch=B, n_chunks=n_chunks, sga_col=col_sga // cw)

        xs_, hs_, hi_ = _conv_specs(tr, cw, CB, lambda s: col_u // cw, n_s_tiles, False)
        yb = pl.pallas_call(
            functools.partial(_convb_kernel, n_s_tiles=n_s_tiles, width=CB),
            grid=(T // tr,),
            in_specs=[xs_, hs_, hi_,
                      pl.BlockSpec((CB, cw), lambda i: (0, 0)),
                      pl.BlockSpec((tr, cw), lambda i: (i, col_gb // cw))],
            out_specs=pl.BlockSpec((tr, cw), lambda i: (i, 0)),
            out_shape=jax.ShapeDtypeStruct((T, cw), BF16),
            scratch_shapes=[pltpu.VMEM((HALO + tr, cw), F32), pltpu.VMEM((tr, cw), F32)],
            compiler_params=_cp(("arbitrary",)),
            name="conv_b",
        )(z, z, hist_b, conv_b_w[l], z)

        xs_, hs_, hi_ = _conv_specs(tr, cw, CC, lambda s: col_gl // cw, n_s_tiles, False)
        hc_rows = max(HALO + tr, (CC - 1) * SUBSET + CHUNK)
        yc = pl.pallas_call(
            functools.partial(_convc_kernel, n_s_tiles=n_s_tiles, width=CC),
            grid=(T // tr,),
            in_specs=[xs_, hs_, hi_,
                      pl.BlockSpec((CC, cw), lambda i: (0, 0)),
                      pl.BlockSpec((1, cw), lambda i: (0, 0)),
                      pl.BlockSpec((1, cw), lambda i: (0, 0)),
                      pl.BlockSpec((1, cw), lambda i: (0, 0))],
            out_specs=pl.BlockSpec((tr, cw), lambda i: (i, 0)),
            out_shape=jax.ShapeDtypeStruct((T, cw), BF16),
            scratch_shapes=[pltpu.VMEM((hc_rows, cw), F32), pltpu.VMEM((tr, cw), F32),
                            pltpu.VMEM((HALO + tr, cw), F32)],
            compiler_params=_cp(("arbitrary",)),
            name="conv_c",
        )(z, z, hist_c, conv_c_w[l], conv_c_b[l][None], ln_c_w[l][None], ln_c_b[l][None])

        m = _merge(ya, yb, yc, woa[l], wob[l], woc[l], z, gs_col=col_gs, tm=tm, tn=tn)
        x1 = _oproj(m, wo[l], x, tm=tm, tn=tn)

        ri, rw, cnt = _router(x1, norm2_w[l][None], w_r[l], b_r[l], n_groups=G, per_group=E // G, tr=tr)
        counts = cnt[0, G:G + E].astype(jnp.int32)
        padded = ((counts + tg - 1) // tg) * tg
        ends = jnp.cumsum(padded)
        offs = ends - padded
        route = (ri[:, 0], ri[:, 2], ri[:, 1], ri[:, 3], offs.astype(jnp.int32))
        n_used = (ends[-1] // tg).astype(jnp.int32)
        tile_start = jnp.arange(n_tiles, dtype=jnp.int32) * tg
        tile_e = jnp.sum((tile_start[:, None] >= ends[None, :]).astype(jnp.int32), axis=1)
        last_e = jnp.sum((((n_used - 1) * tg) >= ends).astype(jnp.int32))
        tile_e = jnp.where(jnp.arange(n_tiles) < n_used, tile_e, last_e).astype(jnp.int32)

        pad_start = (((offs + counts) // 8) * 8).astype(jnp.int32)
        n_used = n_used.reshape(1)
        pad_groups = ((ends - pad_start) // 8).astype(jnp.int32)
        xs = _dispatch(route, pad_start, pad_groups, n_used, x1, norm2_w[l][None], n_rows=P, tr=tr, tg=tg)
        eo = _experts(tile_e, n_used, xs, w_gate_e, w_up_e, w_down_e, l, n_tiles=n_tiles, tg=tg)
        x = _combine(route, x1, rw, mask, final_norm_w[None], eo, tr=tr, final=(l == depth - 1))

        def prompt_tail(c0, c1, width):
            return jnp.stack([lax.slice(z, (Ts + (b + 1) * LP - (width - 1), c0), (Ts + (b + 1) * LP, c1))
                              for b in range(B)]).astype(dt_)

        def sample_hist(st, c0, c1):
            seq = _rows_to_sample(lax.slice(z, (0, c0), (Ts, c1)), L).astype(dt_)
            return jnp.concatenate([st.astype(dt_), seq], axis=1)[:, L:]
        dp.append(s_p)
        ap.append(prompt_tail(0, W_QKV, CA))
        bpl.append(prompt_tail(col_u, col_u + W_B, CB))
        cpl.append(prompt_tail(col_gl, col_gl + W_C, CC))
        as_.append(sample_hist(state_conv_a[l], 0, W_QKV))
        bs.append(sample_hist(state_conv_b[l], col_u, col_u + W_B))
        cs.append(sample_hist(state_conv_c[l], col_gl, col_gl + W_C))

    y_sample = _rows_to_sample(lax.slice(x, (0, 0), (Ts, D)), L)
    y_prompt = jnp.stack([lax.slice(x, (Ts + b * LP + CHUNK, 0), (Ts + (b + 1) * LP, D)) for b in range(B)])
    return (y_prompt, y_sample, jnp.stack(dp), jnp.stack(ap), jnp.stack(bpl), jnp.stack(cpl),
            ds, jnp.stack(as_), jnp.stack(bs), jnp.stack(cs))


def _gcd(a, b):
    while b:
        a, b = b, a % b
    return a
```
